```python
import math
import functools
import jax
import jax.numpy as jnp
from jax import lax
import numpy as np

D_MODEL = 2048
BATCH = 1
SEQ = 8192
DEPTH = 1
DEC_BATCH = 128
DEC_SEQ = 8
PAST_LEN = 16384
PAGE_SIZE = 128

D_A = D_MODEL // 2
CHUNK = 128
A_GROUP = 128
A_GROUPS = D_A // A_GROUP
N_HEADS = 16
Q_LORA = D_MODEL // 4
KV_LORA = D_MODEL // 8
NOPE_DIM = 128
ROPE_DIM = 64
V_DIM = 128
QK_DIM = NOPE_DIM + ROPE_DIM
ROPE_THETA = 10000.0
QUERY_BLOCK = 128
SCALE = 1.0 / math.sqrt(QK_DIM)
N_EXPERTS = 32
TOP_K = 4
D_FF = D_MODEL
SWIGLU_LIMIT = 7.0
SWIGLU_ALPHA = 1.702
MOE_BLOCK = 128
EPS = 1e-6
NEG_INF = -1e30
OFF_U = 0
OFF_V = OFF_U + D_A
OFF_Q = OFF_V + D_A
OFF_KV = OFF_Q + Q_LORA
OFF_KR = OFF_KV + KV_LORA
OFF_GA = OFF_KR + ROPE_DIM
OFF_GB = OFF_GA + D_MODEL
D_IN = OFF_GB + D_MODEL

kernel_name = 'hybrid_gmlp_mla_moe_adaln_step'


def _rms_norm(x, g):
    xf = x.astype(jnp.float32)
    y = xf * lax.rsqrt(jnp.mean(xf * xf, axis=-1, keepdims=True) + EPS)
    return (y * g.astype(jnp.float32)).astype(x.dtype)


def _layer_norm(x, g, b):
    xf = x.astype(jnp.float32)
    mu = jnp.mean(xf, axis=-1, keepdims=True)
    xc = xf - mu
    y = xc * lax.rsqrt(jnp.mean(xc * xc, axis=-1, keepdims=True) + EPS)
    return (y * g.astype(jnp.float32) + b.astype(jnp.float32)).astype(x.dtype)


def _rope(x, pos):
    half = ROPE_DIM // 2
    inv_freq = jnp.exp(-math.log(ROPE_THETA) * jnp.arange(half, dtype=jnp.float32) / half)
    ang = pos.astype(jnp.float32)[:, None] * inv_freq[None, :]
    ang = ang.reshape((pos.shape[0],) + (1,) * (x.ndim - 3) + (half,))
    cos, sin = jnp.cos(ang), jnp.sin(ang)
    xf = x.astype(jnp.float32)
    x1, x2 = xf[..., :half], xf[..., half:]
    return jnp.concatenate([x1 * cos - x2 * sin, x2 * cos + x1 * sin], axis=-1).astype(x.dtype)


def _spatial_gating(u, v_n, w_s, b_s):
    b, L, _ = u.shape
    lc = min(L, CHUNK)
    mask = jnp.tril(jnp.ones((lc, lc), dtype=bool))
    ws = jnp.where(mask[None], w_s[:, :lc, :lc], 0)
    vc = v_n.reshape(b, L // lc, lc, A_GROUPS, A_GROUP)
    s = jnp.einsum('gts,bcsgd->bctgd', ws, vc) + b_s[:, :lc].T[None, None, :, :, None]
    return u * s.reshape(b, L, D_A)


def _attend_prompt(p, q_nope, q_pe, c_kv, k_pe, k_nope_raw, rinv):
    b, s, h, _ = q_nope.shape
    q = jnp.concatenate([q_nope, q_pe], axis=-1) * SCALE
    k_nope = k_nope_raw * rinv[..., None] * p['g_k_nope']
    k = jnp.concatenate([k_nope, jnp.broadcast_to(k_pe[:, :, None, :], (b, s, h, ROPE_DIM)).astype(k_nope.dtype)], axis=-1)
    v = jnp.einsum('bsc,chd->bshd', c_kv, p['w_uv'])
    qb = min(s, QUERY_BLOCK)
    nq = s // qb
    q_blocks = jnp.moveaxis(q.reshape(b, nq, qb, h, QK_DIM), 1, 0)
    k_pos = jnp.arange(s)

    def block(args):
        q_blk, i = args
        sc = jnp.einsum('bqhd,bkhd->bhqk', q_blk, k, preferred_element_type=jnp.float32)
        q_pos = i * qb + jnp.arange(qb)
        sc = jnp.where(k_pos[None, :] <= q_pos[:, None], sc, NEG_INF)
        pr = jax.nn.softmax(sc, axis=-1)
        return jnp.einsum('bhqk,bkhd->bqhd', pr.astype(v.dtype), v)

    o = lax.map(block, (q_blocks, jnp.arange(nq)))
    return jnp.moveaxis(o, 0, 1).reshape(b, s, h * V_DIM)


def _attend_sample(p, q_nope, q_pe, c_kv, k_pe, k_nope_raw, rinv, caches, page_table, layer):
    cache_latent, cache_k_rope, cache_k_rinv = caches
    t = q_nope.shape[1]
    q_abs = jnp.einsum('bthd,chd->bthc', q_nope * p['g_k_nope'], p['w_uk']) * SCALE
    q_rot = q_pe * SCALE

    def scores(ck, kr, ri):
        s_lat = jnp.einsum('bthc,bsc->bhts', q_abs, ck, preferred_element_type=jnp.float32)
        s_rot = jnp.einsum('bthr,bsr->bhts', q_rot, kr, preferred_element_type=jnp.float32)
        return s_lat * jnp.swapaxes(ri.astype(jnp.float32), 1, 2)[:, :, None, :] + s_rot

    causal = jnp.tril(jnp.ones((t, t), dtype=bool))
    sc = jnp.where(causal, scores(c_kv, k_pe, rinv), NEG_INF)
    m = jnp.max(sc, axis=-1)
    pr = jnp.exp(sc - m[..., None])
    l = jnp.sum(pr, axis=-1)
    acc = jnp.einsum('bhts,bsc->bhtc', pr, c_kv.astype(jnp.float32))

    def page_step(carry, phys):
        m, l, acc = carry
        ck = cache_latent[layer, phys]
        kr = cache_k_rope[layer, phys]
        ri = cache_k_rinv[layer, phys]
        sc = scores(ck, kr, ri)
        m_new = jnp.maximum(m, jnp.max(sc, axis=-1))
        corr = jnp.exp(m - m_new)
        pr = jnp.exp(sc - m_new[..., None])
        l = l * corr + jnp.sum(pr, axis=-1)
        acc = acc * corr[..., None] + jnp.einsum('bhts,bsc->bhtc', pr, ck.astype(jnp.float32))
        return (m_new, l, acc), None

    (m, l, acc), _ = lax.scan(page_step, (m, l, acc), page_table.T)
    o_lat = (acc / l[..., None]).astype(c_kv.dtype)
    o = jnp.einsum('bhtc,chd->bthd', o_lat, p['w_uv'])
    return o.reshape(o.shape[0], t, N_HEADS * V_DIM)


def _moe(x2d, w_router, b_router, w_gate_up, b_gate_up, w_down, b_down):
    t, d = x2d.shape
    n_assign = t * TOP_K
    logits = jnp.matmul(x2d, w_router, preferred_element_type=jnp.float32) + b_router.astype(jnp.float32)
    top_val, top_idx = lax.top_k(logits, TOP_K)
    gates = jax.nn.softmax(top_val, axis=-1)
    flat_e = top_idx.reshape(-1).astype(jnp.int32)
    flat_tok = jnp.arange(n_assign, dtype=jnp.int32) // TOP_K
    order = jnp.argsort(flat_e)
    sorted_e = flat_e[order]
    counts = jnp.bincount(flat_e, length=N_EXPERTS).astype(jnp.int32)
    padded = (counts + MOE_BLOCK - 1) // MOE_BLOCK * MOE_BLOCK
    pad_end = jnp.cumsum(padded)
    pad_start = pad_end - padded
    start = jnp.cumsum(counts) - counts
    dest = pad_start[sorted_e] + jnp.arange(n_assign, dtype=jnp.int32) - start[sorted_e]
    n_blocks = -(-(n_assign + N_EXPERTS * (MOE_BLOCK - 1)) // MOE_BLOCK)
    n_rows = n_blocks * MOE_BLOCK
    row_tok = jnp.full((n_rows,), t, jnp.int32).at[dest].set(flat_tok[order])
    block_e = jnp.minimum(jnp.searchsorted(pad_end, jnp.arange(n_blocks, dtype=jnp.int32) * MOE_BLOCK, side='right'), N_EXPERTS - 1)
    x_pad = jnp.concatenate([x2d, jnp.zeros((1, d), x2d.dtype)], axis=0)

    def expert_block(args):
        tok, e = args
        xb = x_pad[tok]
        hgu = xb @ w_gate_up[e] + b_gate_up[e]
        glu = jnp.minimum(hgu[:, :D_FF], SWIGLU_LIMIT)
        lin = jnp.clip(hgu[:, D_FF:], -SWIGLU_LIMIT, SWIGLU_LIMIT)
        act = glu * jax.nn.sigmoid(SWIGLU_ALPHA * glu) * (lin + 1)
        return act @ w_down[e] + b_down[e]

    out_rows = lax.map(expert_block, (row_tok.reshape(n_blocks, MOE_BLOCK), block_e)).reshape(n_rows, d)
    dest_by_assign = jnp.zeros((n_assign,), jnp.int32).at[order].set(dest)
    y = out_rows[dest_by_assign].reshape(t, TOP_K, d)
    return jnp.einsum('tkd,tk->td', y.astype(jnp.float32), gates).astype(x2d.dtype)


def _layer(x, c, pos, p, attend):
    mod = (jax.nn.silu(c) @ p['w_ada'] + p['b_ada'])[:, None, :]
    shift1, scale1, gate1, shift2, scale2, gate2 = jnp.split(mod, 6, axis=-1)
    h = _rms_norm(x, p['g_norm1']) * (1 + scale1) + shift1
    z = h @ p['w_in']
    u = jax.nn.gelu(z[..., OFF_U:OFF_V])
    v_n = _layer_norm(jax.nn.gelu(z[..., OFF_V:OFF_Q]), p['g_v_ln'], p['b_v_ln'])
    a_out = _spatial_gating(u, v_n, p['w_s'], p['b_s'])
    q_lat = _rms_norm(z[..., OFF_Q:OFF_KV], p['g_q_lat'])
    q = jnp.einsum('btr,rhd->bthd', q_lat, p['w_uq'])
    q_nope = _rms_norm(q[..., :NOPE_DIM], p['g_q_nope'])
    q_pe = _rope(_rms_norm(q[..., NOPE_DIM:], p['g_q_rope']), pos)
    c_kv = _rms_norm(z[..., OFF_KV:OFF_KR], p['g_kv_lat'])
    k_pe = _rope(_rms_norm(z[..., OFF_KR:OFF_GA], p['g_k_rope']), pos)
    k_nope_raw = jnp.einsum('btc,chd->bthd', c_kv, p['w_uk'])
    kf = k_nope_raw.astype(jnp.float32)
    rinv = lax.rsqrt(jnp.mean(kf * kf, axis=-1) + EPS).astype(x.dtype)
    b_out = attend(p, q_nope, q_pe, c_kv, k_pe, k_nope_raw, rinv)
    gate_a = jax.nn.sigmoid(z[..., OFF_GA:OFF_GB])
    gate_b = jax.nn.sigmoid(z[..., OFF_GB:D_IN])
    merged = gate_a * (a_out @ p['w_proj_a']) + gate_b * (b_out @ p['w_proj_b'])
    x = x + gate1 * (merged @ p['w_out'])
    h2 = _rms_norm(x, p['g_norm2']) * (1 + scale2) + shift2
    ff = _moe(h2.reshape(-1, D_MODEL), p['w_router'], p['b_router'], p['w_gate_up'],
              p['b_gate_up'], p['w_down'], p['b_down']).reshape(x.shape)
    x = x + gate2 * ff
    return x, c_kv, k_pe, rinv, v_n


def setup_inputs(seed: int = 0) -> dict:
    key = jax.random.key(seed)
    ks = iter(jax.random.split(key, 48))
    f32 = jnp.float32

    def nrm(shape, fan_in, scale=1.0):
        return jax.random.normal(next(ks), shape, f32) * (scale * fan_in ** -0.5)

    def gain(shape):
        return 1.0 + 0.05 * jax.random.normal(next(ks), shape, f32)

    def bias(shape, s=0.02):
        return s * jax.random.normal(next(ks), shape, f32)

    n_pages = PAST_LEN // PAGE_SIZE
    n_used = DEC_BATCH * n_pages
    n_phys = n_used + n_used // 4
    L = DEPTH
    x_prompt = jax.random.normal(next(ks), (BATCH, SEQ, D_MODEL), f32)
    x_sample = jax.random.normal(next(ks), (DEC_BATCH, DEC_SEQ, D_MODEL), f32)
    c_prompt = jax.random.normal(next(ks), (BATCH, D_MODEL), f32)
    c_sample = jax.random.normal(next(ks), (DEC_BATCH, D_MODEL), f32)
    cache_latent = jax.random.normal(next(ks), (L, n_phys, PAGE_SIZE, KV_LORA), f32)
    cache_k_rope = jax.random.normal(next(ks), (L, n_phys, PAGE_SIZE, ROPE_DIM), f32)
    cache_k_rinv = jax.random.uniform(next(ks), (L, n_phys, PAGE_SIZE, N_HEADS), f32, 0.8, 1.25)
    page_table = jax.random.permutation(next(ks), n_phys)[:n_used].reshape(DEC_BATCH, n_pages).astype(jnp.int32)
    return {
        'x_prompt': x_prompt,
        'x_sample': x_sample,
        'c_prompt': c_prompt,
        'c_sample': c_sample,
        'cache_latent': cache_latent,
        'cache_k_rope': cache_k_rope,
        'cache_k_rinv': cache_k_rinv,
        'page_table': page_table,
        'w_ada': nrm((L, D_MODEL, 6 * D_MODEL), D_MODEL, 0.5),
        'b_ada': bias((L, 6 * D_MODEL)),
        'g_norm1': gain((L, D_MODEL)),
        'w_in': nrm((L, D_MODEL, D_IN), D_MODEL),
        'g_v_ln': gain((L, D_A)),
        'b_v_ln': bias((L, D_A)),
        'w_s': nrm((L, A_GROUPS, CHUNK, CHUNK), CHUNK),
        'b_s': gain((L, A_GROUPS, CHUNK)),
        'g_q_lat': gain((L, Q_LORA)),
        'w_uq': nrm((L, Q_LORA, N_HEADS, QK_DIM), Q_LORA),
        'g_q_nope': gain((L, NOPE_DIM)),
        'g_q_rope': gain((L, ROPE_DIM)),
        'g_kv_lat': gain((L, KV_LORA)),
        'g_k_rope': gain((L, ROPE_DIM)),
        'w_uk': nrm((L, KV_LORA, N_HEADS, NOPE_DIM), KV_LORA),
        'w_uv': nrm((L, KV_LORA, N_HEADS, V_DIM), KV_LORA),
        'g_k_nope': gain((L, NOPE_DIM)),
        'w_proj_a': nrm((L, D_A, D_MODEL), D_A),
        'w_proj_b': nrm((L, N_HEADS * V_DIM, D_MODEL), N_HEADS * V_DIM),
        'w_out': nrm((L, D_MODEL, D_MODEL), D_MODEL),
        'g_norm2': gain((L, D_MODEL)),
        'w_router': nrm((L, D_MODEL, N_EXPERTS), D_MODEL),
        'b_router': bias((L, N_EXPERTS), 0.01),
        'w_gate_up': nrm((L, N_EXPERTS, D_MODEL, 2 * D_FF), D_MODEL),
        'b_gate_up': bias((L, N_EXPERTS, 2 * D_FF)),
        'w_down': nrm((L, N_EXPERTS, D_FF, D_MODEL), D_FF),
        'b_down': bias((L, N_EXPERTS, D_MODEL)),
    }


def reference(x_prompt, x_sample, c_prompt, c_sample, cache_latent, cache_k_rope, cache_k_rinv, page_table,
              w_ada, b_ada, g_norm1, w_in, g_v_ln, b_v_ln, w_s, b_s, g_q_lat, w_uq, g_q_nope, g_q_rope,
              g_kv_lat, g_k_rope, w_uk, w_uv, g_k_nope, w_proj_a, w_proj_b, w_out, g_norm2,
              w_router, b_router, w_gate_up, b_gate_up, w_down, b_down):
    pos_prompt = jnp.arange(x_prompt.shape[1], dtype=jnp.int32)
    pos_sample = PAST_LEN + jnp.arange(x_sample.shape[1], dtype=jnp.int32)
    y_p, y_s = x_prompt, x_sample
    lat_p, kr_p, ri_p, lat_s, kr_s, ri_s, v_s = [], [], [], [], [], [], []
    for l in range(DEPTH):
        p = {
            'w_ada': w_ada[l], 'b_ada': b_ada[l], 'g_norm1': g_norm1[l], 'w_in': w_in[l],
            'g_v_ln': g_v_ln[l], 'b_v_ln': b_v_ln[l], 'w_s': w_s[l], 'b_s': b_s[l],
            'g_q_lat': g_q_lat[l], 'w_uq': w_uq[l], 'g_q_nope': g_q_nope[l], 'g_q_rope': g_q_rope[l],
            'g_kv_lat': g_kv_lat[l], 'g_k_rope': g_k_rope[l], 'w_uk': w_uk[l], 'w_uv': w_uv[l],
            'g_k_nope': g_k_nope[l], 'w_proj_a': w_proj_a[l], 'w_proj_b': w_proj_b[l], 'w_out': w_out[l],
            'g_norm2': g_norm2[l], 'w_router': w_router[l], 'b_router': b_router[l],
            'w_gate_up': w_gate_up[l], 'b_gate_up': b_gate_up[l], 'w_down': w_down[l], 'b_down': b_down[l],
        }
        y_p, ckv_p, kpe_p, rinv_p, _ = _layer(y_p, c_prompt, pos_prompt, p, _attend_prompt)
        attend_s = functools.partial(_attend_sample, caches=(cache_latent, cache_k_rope, cache_k_rinv),
                                     page_table=page_table, layer=l)
        y_s, ckv_s, kpe_s, rinv_s, vn_s = _layer(y_s, c_sample, pos_sample, p, attend_s)
        lat_p.append(ckv_p)
        kr_p.append(kpe_p)
        ri_p.append(rinv_p)
        lat_s.append(ckv_s)
        kr_s.append(kpe_s)
        ri_s.append(rinv_s)
        v_s.append(vn_s)
    new_latent_prompt = jnp.stack(lat_p)
    new_k_rope_prompt = jnp.stack(kr_p)
    new_k_rinv_prompt = jnp.stack(ri_p)
    new_latent_sample = jnp.stack(lat_s)
    new_k_rope_sample = jnp.stack(kr_s)
    new_k_rinv_sample = jnp.stack(ri_s)
    new_v_sample = jnp.stack(v_s)
    return (y_p, y_s, new_latent_prompt, new_k_rope_prompt, new_k_rinv_prompt,
            new_latent_sample, new_k_rope_sample, new_k_rinv_sample, new_v_sample)
```

```python
import functools
import math

import jax
import jax.numpy as jnp
from jax import lax
from jax.experimental import pallas as pl
from jax.experimental.pallas import tpu as pltpu

F32 = jnp.float32
BF16 = jnp.bfloat16

D_MODEL = 2048
D_A = D_MODEL // 2
A_GROUP = 128
A_GROUPS = D_A // A_GROUP
CHUNK = 128
N_HEADS = 16
Q_LORA = D_MODEL // 4
KV_LORA = D_MODEL // 8
NOPE_DIM = 128
ROPE_DIM = 64
V_DIM = 128
QK_DIM = NOPE_DIM + ROPE_DIM
ROPE_THETA = 10000.0
SCALE = 1.0 / math.sqrt(QK_DIM)
N_EXPERTS = 32
TOP_K = 4
D_FF = D_MODEL
SWIGLU_LIMIT = 7.0
SWIGLU_ALPHA = 1.702
EPS = 1e-6
NEG_INF = -1e30
PAGE_SIZE = 128

LANES = 128
MOE_ROWS = 128
PAGES_PER_STEP = 16
VMEM_LIMIT = 56 * 1024 * 1024

ZO_U = 0
ZO_V = ZO_U + D_A
ZO_GA = ZO_V + D_A
ZO_GB = ZO_GA + D_MODEL
ZO_Q = ZO_GB + D_MODEL
ZO_KV = ZO_Q + Q_LORA
ZO_KR = ZO_KV + KV_LORA
Z_COLS = ZO_KR + 2 * ROPE_DIM


def _params(sem):
    return pltpu.CompilerParams(dimension_semantics=sem, vmem_limit_bytes=VMEM_LIMIT)


def _dot(a, b):
    return jnp.dot(a, b, preferred_element_type=F32)


def _dot_nt(a, b):
    return lax.dot_general(a, b, (((1,), (1,)), ((), ())), preferred_element_type=F32)


def _sigmoid(x):
    return 1.0 / (1.0 + jnp.exp(-x))


def _gelu(x):
    c = math.sqrt(2.0 / math.pi)
    return 0.5 * x * (1.0 + jnp.tanh(c * (x + 0.044715 * (x * x * x))))


def _rms(x):
    return x * lax.rsqrt(jnp.mean(x * x, axis=-1, keepdims=True) + EPS)


def _ada_kernel(c_ref, w_ref, b_ref, o_ref):
    c = c_ref[...]
    a = (c * _sigmoid(c)).astype(BF16)
    o_ref[...] = _dot(a, w_ref[...].astype(BF16)) + b_ref[...]


def _ada(c_all, w_ada, b_ada):
    rows, d = c_all.shape
    n = w_ada.shape[1]
    tn = 1024
    return pl.pallas_call(
        _ada_kernel,
        out_shape=jax.ShapeDtypeStruct((rows, n), F32),
        grid=(n // tn,),
        in_specs=[pl.BlockSpec((rows, d), lambda j: (0, 0)),
                  pl.BlockSpec((d, tn), lambda j: (0, j)),
                  pl.BlockSpec((1, tn), lambda j: (0, j))],
        out_specs=pl.BlockSpec((rows, tn), lambda j: (0, j)),
        compiler_params=_params(("arbitrary",)),
    )(c_all, w_ada, b_ada.reshape(1, n))


def _normmod_kernel(x_ref, g_ref, sc_ref, sh_ref, o_ref):
    x = x_ref[...]
    y = _rms(x) * g_ref[...]
    y = y * (1.0 + sc_ref[...]) + sh_ref[...]
    o_ref[...] = y.reshape(o_ref.shape).astype(o_ref.dtype)


def _normmod(x, g, mod3, sc_idx, sh_idx, bb, tt):
    b, t, d = x.shape
    return pl.pallas_call(
        _normmod_kernel,
        out_shape=jax.ShapeDtypeStruct((b * t, d), BF16),
        grid=(b // bb, t // tt),
        in_specs=[pl.BlockSpec((bb, tt, d), lambda i, j: (i, j, 0)),
                  pl.BlockSpec((1, 1, d), lambda i, j: (0, 0, 0)),
                  pl.BlockSpec((bb, 1, d), lambda i, j: (i, 0, sc_idx)),
                  pl.BlockSpec((bb, 1, d), lambda i, j: (i, 0, sh_idx))],
        out_specs=pl.BlockSpec((bb * tt, d), lambda i, j: (i * (t // tt) + j, 0)),
        compiler_params=_params(("arbitrary", "arbitrary")),
    )(x, g.reshape(1, 1, d), mod3, mod3)


def _mm_kernel(x_ref, w_ref, o_ref):
    o_ref[...] = _dot(x_ref[...], w_ref[...]).astype(o_ref.dtype)


def _mm(x, w, tm, tn, out_dtype=F32):
    m, k = x.shape
    n = w.shape[1]
    return pl.pallas_call(
        _mm_kernel,
        out_shape=jax.ShapeDtypeStruct((m, n), out_dtype),
        grid=(m // tm, n // tn),
        in_specs=[pl.BlockSpec((tm, k), lambda i, j: (i, 0)),
                  pl.BlockSpec((k, tn), lambda i, j: (0, j))],
        out_specs=pl.BlockSpec((tm, tn), lambda i, j: (i, j)),
        compiler_params=_params(("arbitrary", "arbitrary")),
    )(x, w)


def _gmlp_kernel(zu_ref, zv_ref, gln_ref, bln_ref, ws_ref, bs_ref, a_ref, *maybe_vn_ref, lc):
    u = _gelu(zu_ref[...])
    gv = _gelu(zv_ref[...])
    xc = gv - jnp.mean(gv, axis=-1, keepdims=True)
    vn = xc * lax.rsqrt(jnp.mean(xc * xc, axis=-1, keepdims=True) + EPS)
    vn = vn * gln_ref[...] + bln_ref[...]
    if maybe_vn_ref:
        maybe_vn_ref[0][...] = vn
    row = lax.broadcasted_iota(jnp.int32, (CHUNK, CHUNK), 0)
    col = lax.broadcasted_iota(jnp.int32, (CHUNK, CHUNK), 1)
    mask = (col <= row) & ((row // lc) == (col // lc))
    vb = vn.astype(BF16)
    for g in range(A_GROUPS):
        w = jnp.where(mask, ws_ref[g], 0.0).astype(BF16)
        sl = slice(g * A_GROUP, (g + 1) * A_GROUP)
        s = _dot(w, vb[:, sl]) + bs_ref[g]
        a_ref[:, sl] = (u[:, sl] * s).astype(a_ref.dtype)


def _gmlp(z, g_ln, b_ln, ws_t, bs_t, lc, want_vn):
    t = z.shape[0]
    out_shape = [jax.ShapeDtypeStruct((t, D_A), BF16)]
    out_specs = [pl.BlockSpec((CHUNK, D_A), lambda i: (i, 0))]
    if want_vn:
        out_shape.append(jax.ShapeDtypeStruct((t, D_A), F32))
        out_specs.append(pl.BlockSpec((CHUNK, D_A), lambda i: (i, 0)))
    return pl.pallas_call(
        functools.partial(_gmlp_kernel, lc=lc),
        out_shape=out_shape,
        grid=(t // CHUNK,),
        in_specs=[pl.BlockSpec((CHUNK, D_A), lambda i: (i, ZO_U // D_A)),
                  pl.BlockSpec((CHUNK, D_A), lambda i: (i, ZO_V // D_A)),
                  pl.BlockSpec((1, D_A), lambda i: (0, 0)),
                  pl.BlockSpec((1, D_A), lambda i: (0, 0)),
                  pl.BlockSpec((A_GROUPS, CHUNK, CHUNK), lambda i: (0, 0, 0)),
                  pl.BlockSpec((A_GROUPS, CHUNK, CHUNK), lambda i: (0, 0, 0))],
        out_specs=out_specs,
        compiler_params=_params(("arbitrary",)),
    )(z, z, g_ln.reshape(1, D_A), b_ln.reshape(1, D_A), ws_t, bs_t)


def _rope_pair(y2, gain2, tab):
    p = y2 * gain2 * tab
    return p + pltpu.roll(p, ROPE_DIM, 1)


def _mla_kernel(zq_ref, zkv_ref, zkr_ref, tab_ref, gql_ref, gkv_ref, gqn_ref, gq2_ref, gk2_ref, gkn_ref,
                wqn_ref, wqr_ref, wuk_ref, wuv_ref,
                qn_ref, qp_ref, ckv_ref, kpe_ref, rinv_ref, *prompt_refs, absorb):
    tab = tab_ref[...]
    q_lat = (_rms(zq_ref[...]) * gql_ref[...]).astype(BF16)
    c_kv = _rms(zkv_ref[...]) * gkv_ref[...]
    ckv_ref[...] = c_kv
    kr2 = zkr_ref[...]
    kr2 = kr2 * lax.rsqrt(jnp.sum(kr2 * kr2, axis=-1, keepdims=True) / (2 * ROPE_DIM) + EPS)
    kpe2 = _rope_pair(kr2, gk2_ref[...], tab)
    kpe_ref[...] = kpe2[:, :ROPE_DIM]
    cb = c_kv.astype(BF16)
    tm = cb.shape[0]
    lane = lax.broadcasted_iota(jnp.int32, (tm, N_HEADS), 1)
    rinv_all = jnp.zeros((tm, N_HEADS), F32)
    for h in range(N_HEADS):
        sl = slice(h * NOPE_DIM, (h + 1) * NOPE_DIM)
        qn = _rms(_dot(q_lat, wqn_ref[:, sl])) * gqn_ref[...]
        qr2 = _dot(q_lat, wqr_ref[:, sl])
        qr2 = qr2 * lax.rsqrt(jnp.sum(qr2 * qr2, axis=-1, keepdims=True) / (2 * ROPE_DIM) + EPS)
        qp = _rope_pair(qr2, gq2_ref[...], tab)
        if absorb:
            qn_ref[:, sl] = (qn * gkn_ref[...]).astype(BF16)
        else:
            qn_ref[:, sl] = (qn * SCALE).astype(BF16)
        qp_ref[:, sl] = (qp * SCALE).astype(BF16)
        k_raw = _dot(cb, wuk_ref[:, sl])
        rinv = lax.rsqrt(jnp.mean(k_raw * k_raw, axis=-1, keepdims=True) + EPS)
        rinv_all = jnp.where(lane == h, rinv, rinv_all)
        if prompt_refs:
            kn_ref, v_ref, kp_ref = prompt_refs
            kn_ref[:, sl] = (k_raw * rinv * gkn_ref[...]).astype(BF16)
            v_ref[:, sl] = _dot(cb, wuv_ref[:, sl]).astype(BF16)
    rinv_ref[...] = rinv_all
    if prompt_refs:
        col = lax.broadcasted_iota(jnp.int32, kpe2.shape, 1)
        prompt_refs[2][...] = jnp.where(col < ROPE_DIM, kpe2, 0.0).astype(BF16)


def _mla(z, tab, gains, weights, tm, prompt):
    t = z.shape[0]
    hd = N_HEADS * NOPE_DIM
    row = lambda w: pl.BlockSpec((tm, w), lambda i: (i, 0))
    full = lambda a: pl.BlockSpec(a.shape, lambda i: (0,) * a.ndim)
    out_shape = [jax.ShapeDtypeStruct((t, hd), BF16), jax.ShapeDtypeStruct((t, hd), BF16),
                 jax.ShapeDtypeStruct((t, KV_LORA), F32), jax.ShapeDtypeStruct((t, ROPE_DIM), F32),
                 jax.ShapeDtypeStruct((t, N_HEADS), F32)]
    out_specs = [row(hd), row(hd), row(KV_LORA), row(ROPE_DIM), row(N_HEADS)]
    if prompt:
        out_shape += [jax.ShapeDtypeStruct((t, hd), BF16), jax.ShapeDtypeStruct((t, hd), BF16),
                      jax.ShapeDtypeStruct((t, LANES), BF16)]
        out_specs += [row(hd), row(hd), row(LANES)]
    return pl.pallas_call(
        functools.partial(_mla_kernel, absorb=not prompt),
        out_shape=out_shape,
        grid=(t // tm,),
        in_specs=[pl.BlockSpec((tm, Q_LORA), lambda i: (i, ZO_Q // Q_LORA)),
                  pl.BlockSpec((tm, KV_LORA), lambda i: (i, ZO_KV // KV_LORA)),
                  pl.BlockSpec((tm, LANES), lambda i: (i, ZO_KR // LANES)),
                  row(LANES)] + [full(a) for a in gains] + [full(a) for a in weights],
        out_specs=out_specs,
        compiler_params=_params(("arbitrary",)),
    )(z, z, z, tab, *gains, *weights)


def _fa_kernel(qn_ref, qp_ref, kn_ref, kp_ref, v_ref, o_ref, *, tq):
    i = pl.program_id(1)
    qn = qn_ref[...]
    qp = qp_ref[...]

    def step(j, carry, masked):
        m, l, acc = carry
        ks = pl.ds(pl.multiple_of(j * tq, tq), tq)
        s = _dot_nt(qn, kn_ref[ks, :]) + _dot_nt(qp, kp_ref[ks, :])
        if masked:
            row = lax.broadcasted_iota(jnp.int32, (tq, tq), 0)
            col = lax.broadcasted_iota(jnp.int32, (tq, tq), 1)
            s = jnp.where(col <= row, s, NEG_INF)
        m_new = jnp.maximum(m, jnp.max(s, axis=-1, keepdims=True))
        corr = jnp.exp(m - m_new)
        p = jnp.exp(s - m_new)
        l = l * corr + jnp.sum(p, axis=-1, keepdims=True)
        acc = acc * corr + _dot(p.astype(BF16), v_ref[ks, :])
        return m_new, l, acc

    init = (jnp.full((tq, 1), NEG_INF, F32), jnp.zeros((tq, 1), F32), jnp.zeros((tq, V_DIM), F32))
    carry = lax.fori_loop(0, i, functools.partial(step, masked=False), init)
    m, l, acc = step(i, carry, True)
    o_ref[...] = (acc / l).astype(o_ref.dtype)


def _attend_prompt(qn, qp, kn, kp, v, tq):
    t = qn.shape[0]
    head_blk = lambda rows: pl.BlockSpec((rows, NOPE_DIM), lambda h, i: (0, h))
    q_blk = pl.BlockSpec((tq, NOPE_DIM), lambda h, i: (i, h))
    return pl.pallas_call(
        functools.partial(_fa_kernel, tq=tq),
        out_shape=jax.ShapeDtypeStruct((t, N_HEADS * V_DIM), BF16),
        grid=(N_HEADS, t // tq),
        in_specs=[q_blk, q_blk, head_blk(t), pl.BlockSpec((t, LANES), lambda h, i: (0, 0)), head_blk(t)],
        out_specs=pl.BlockSpec((tq, V_DIM), lambda h, i: (i, h)),
        compiler_params=_params(("arbitrary", "arbitrary")),
    )(qn, qp, kn, kp, v)


def _absorb_kernel(qn_ref, qp_ref, wuk_ref, qa_ref, qr_ref):
    qa = _dot_nt(qn_ref[...], wuk_ref[...]) * SCALE
    qa_ref[...] = qa.reshape(qa_ref.shape).astype(qa_ref.dtype)
    qr_ref[...] = qp_ref[...].astype(F32).reshape(qr_ref.shape).astype(qr_ref.dtype)


def _absorb(qn, qp, w_uk_b, b, t):
    return pl.pallas_call(
        _absorb_kernel,
        out_shape=[jax.ShapeDtypeStruct((b, N_HEADS, t, KV_LORA), F32),
                   jax.ShapeDtypeStruct((b, N_HEADS, t, LANES), F32)],
        grid=(N_HEADS,),
        in_specs=[pl.BlockSpec((b * t, NOPE_DIM), lambda h: (0, h)),
                  pl.BlockSpec((b * t, NOPE_DIM), lambda h: (0, h)),
                  pl.BlockSpec((KV_LORA, NOPE_DIM), lambda h: (0, h))],
        out_specs=[pl.BlockSpec((b, None, t, KV_LORA), lambda h: (0, h, 0, 0)),
                   pl.BlockSpec((b, None, t, LANES), lambda h: (0, h, 0, 0))],
        compiler_params=_params(("arbitrary",)),
    )(qn, qp, w_uk_b)


def _decode_kernel(pt_ref, qa_ref, qr_ref, nlat_ref, nrope_ref, nrinv_ref, *refs, t_new):
    del pt_ref
    pc = PAGES_PER_STEP
    lat_refs, rope_refs, rinv_refs = refs[:pc], refs[pc:2 * pc], refs[2 * pc:3 * pc]
    o_ref = refs[3 * pc]
    m_ref, l_ref, acc_ref = refs[3 * pc + 1:]
    c = pl.program_id(1)
    rows = N_HEADS * t_new
    qa = qa_ref[...].astype(BF16)
    qr = qr_ref[...][:, :ROPE_DIM].astype(BF16)

    def scores(ck, kr, rit):
        s_lat = _dot_nt(qa, ck)
        scale = jnp.broadcast_to(rit[:, None, :], (N_HEADS, t_new, PAGE_SIZE)).reshape(rows, PAGE_SIZE)
        return s_lat * scale + _dot_nt(qr, kr)

    def fold(s_list, ck_list):
        m_old = m_ref[...]
        m_blk = s_list[0]
        for s in s_list[1:]:
            m_blk = jnp.maximum(m_blk, s)
        m_new = jnp.maximum(m_old, jnp.max(m_blk, axis=-1, keepdims=True))
        corr = jnp.exp(m_old - m_new)
        p_sum = None
        acc = acc_ref[...] * corr
        for s, ck in zip(s_list, ck_list):
            p = jnp.exp(s - m_new)
            p_sum = p if p_sum is None else p_sum + p
            acc = acc + _dot(p.astype(BF16), ck)
        l_ref[...] = l_ref[...] * corr + jnp.sum(p_sum, axis=-1, keepdims=True)
        m_ref[...] = m_new
        acc_ref[...] = acc

    @pl.when(c == 0)
    def _():
        m_ref[...] = jnp.full(m_ref.shape, NEG_INF, F32)
        l_ref[...] = jnp.zeros(l_ref.shape, F32)
        acc_ref[...] = jnp.zeros(acc_ref.shape, F32)
        ck = nlat_ref[...].astype(BF16)
        s = scores(ck, nrope_ref[...].astype(BF16), nrinv_ref[...])
        row = lax.broadcasted_iota(jnp.int32, s.shape, 0)
        col = lax.broadcasted_iota(jnp.int32, s.shape, 1)
        s = jnp.where(col <= (row % t_new), s, NEG_INF)
        fold([s], [ck])

    s_list, ck_list = [], []
    for j in range(pc):
        ck = lat_refs[j][...].astype(BF16)
        ck_list.append(ck)
        s_list.append(scores(ck, rope_refs[j][...].astype(BF16), rinv_refs[j][...]))
    fold(s_list, ck_list)

    @pl.when(c == pl.num_programs(1) - 1)
    def _():
        o_ref[...] = (acc_ref[...] / l_ref[...]).astype(o_ref.dtype)


def _decode(page_table, qa, qr, nlat, nrope, nrinv_t, cache_latent, cache_k_rope, cache_rinv_t, t_new):
    b, n_pages = page_table.shape
    pc = PAGES_PER_STEP
    rows = N_HEADS * t_new
    seq = lambda w: pl.BlockSpec((None, rows, w), lambda i, c, pt: (i, 0, 0))

    def page_spec(shape, j):
        return pl.BlockSpec((None, None) + shape, lambda i, c, pt: (0, pt[i, c * pc + j], 0, 0))

    in_specs = [seq(KV_LORA), seq(LANES),
                pl.BlockSpec((None, PAGE_SIZE, KV_LORA), lambda i, c, pt: (i, 0, 0)),
                pl.BlockSpec((None, PAGE_SIZE, ROPE_DIM), lambda i, c, pt: (i, 0, 0)),
                pl.BlockSpec((None, N_HEADS, PAGE_SIZE), lambda i, c, pt: (i, 0, 0))]
    in_specs += [page_spec((PAGE_SIZE, KV_LORA), j) for j in range(pc)]
    in_specs += [page_spec((PAGE_SIZE, ROPE_DIM), j) for j in range(pc)]
    in_specs += [page_spec((N_HEADS, PAGE_SIZE), j) for j in range(pc)]
    grid_spec = pltpu.PrefetchScalarGridSpec(
        num_scalar_prefetch=1,
        grid=(b, n_pages // pc),
        in_specs=in_specs,
        out_specs=pl.BlockSpec((None, rows, KV_LORA), lambda i, c, pt: (i, 0, 0)),
        scratch_shapes=[pltpu.VMEM((rows, 1), F32), pltpu.VMEM((rows, 1), F32), pltpu.VMEM((rows, KV_LORA), F32)],
    )
    return pl.pallas_call(
        functools.partial(_decode_kernel, t_new=t_new),
        out_shape=jax.ShapeDtypeStruct((b, rows, KV_LORA), F32),
        grid_spec=grid_spec,
        compiler_params=_params(("arbitrary", "arbitrary")),
    )(page_table, qa, qr, nlat, nrope, nrinv_t,
      *([cache_latent] * pc), *([cache_k_rope] * pc), *([cache_rinv_t] * pc))


def _upv_kernel(o_ref, wuv_ref, out_ref):
    o = o_ref[...]
    o2 = o.reshape(o.shape[0] * o.shape[1], o.shape[2]).astype(BF16)
    out_ref[...] = _dot(o2, wuv_ref[...]).astype(out_ref.dtype)


def _upv(o_lat, w_uv_b, b, t):
    return pl.pallas_call(
        _upv_kernel,
        out_shape=jax.ShapeDtypeStruct((b * t, N_HEADS * V_DIM), BF16),
        grid=(N_HEADS,),
        in_specs=[pl.BlockSpec((b, None, t, KV_LORA), lambda h: (0, h, 0, 0)),
                  pl.BlockSpec((KV_LORA, V_DIM), lambda h: (0, h))],
        out_specs=pl.BlockSpec((b * t, V_DIM), lambda h: (0, h)),
        compiler_params=_params(("arbitrary",)),
    )(o_lat, w_uv_b)


def _merge_kernel(a_ref, b_ref, wa_ref, wb_ref, ga_ref, gb_ref, o_ref):
    pa = _dot(a_ref[...], wa_ref[...])
    pb = _dot(b_ref[...], wb_ref[...])
    o_ref[...] = (_sigmoid(ga_ref[...]) * pa + _sigmoid(gb_ref[...]) * pb).astype(o_ref.dtype)


def _merge(a_out, b_out, wpa, wpb, z, tm, tn):
    t = a_out.shape[0]
    return pl.pallas_call(
        _merge_kernel,
        out_shape=jax.ShapeDtypeStruct((t, D_MODEL), BF16),
        grid=(t // tm, D_MODEL // tn),
        in_specs=[pl.BlockSpec((tm, D_A), lambda i, j: (i, 0)),
                  pl.BlockSpec((tm, D_MODEL), lambda i, j: (i, 0)),
                  pl.BlockSpec((D_A, tn), lambda i, j: (0, j)),
                  pl.BlockSpec((D_MODEL, tn), lambda i, j: (0, j)),
                  pl.BlockSpec((tm, tn), lambda i, j: (i, ZO_GA // tn + j)),
                  pl.BlockSpec((tm, tn), lambda i, j: (i, ZO_GB // tn + j))],
        out_specs=pl.BlockSpec((tm, tn), lambda i, j: (i, j)),
        compiler_params=_params(("arbitrary", "arbitrary")),
    )(a_out, b_out, wpa, wpb, z, z)


def _outproj_kernel(m_ref, wo_ref, x_ref, g1_ref, gn_ref, sc_ref, sh_ref, wr_ref, br_ref,
                    x1_ref, h2_ref, idx_ref, gate_ref):
    shape3 = x_ref.shape
    y = _dot(m_ref[...], wo_ref[...])
    x1 = x_ref[...] + g1_ref[...] * y.reshape(shape3)
    x1_ref[...] = x1
    h2 = _rms(x1) * gn_ref[...]
    h2 = (h2 * (1.0 + sc_ref[...]) + sh_ref[...]).reshape(y.shape).astype(BF16)
    h2_ref[...] = h2
    logits = _dot(h2, wr_ref[...]) + br_ref[...]
    lane = lax.broadcasted_iota(jnp.int32, logits.shape, 1).astype(F32)
    vals, idxs = [], []
    for _ in range(TOP_K):
        mx = jnp.max(logits, axis=-1, keepdims=True)
        am = jnp.min(jnp.where(logits == mx, lane, float(LANES)), axis=-1, keepdims=True)
        vals.append(mx)
        idxs.append(am)
        logits = jnp.where(lane == am, -3.0e38, logits)
    es = [jnp.exp(v - vals[0]) for v in vals]
    den = es[0] + es[1] + es[2] + es[3]
    idx_out = jnp.zeros(lane.shape, F32)
    gate_out = jnp.zeros(lane.shape, F32)
    for k in range(TOP_K):
        idx_out = jnp.where(lane == float(k), idxs[k], idx_out)
        gate_out = jnp.where(lane == float(k), es[k] / den, gate_out)
    idx_ref[...] = idx_out.astype(jnp.int32)
    gate_ref[...] = gate_out


def _outproj(merged, w_out_b, x, mod3, g_norm2, wr_pad, br_pad, bb, tt):
    b, t, d = x.shape
    tm = bb * tt
    nt = t // tt
    tok = lambda w: pl.BlockSpec((tm, w), lambda i, j: (i * nt + j, 0))
    modspec = lambda k: pl.BlockSpec((bb, 1, d), lambda i, j: (i, 0, k))
    return pl.pallas_call(
        _outproj_kernel,
        out_shape=[jax.ShapeDtypeStruct((b, t, d), F32), jax.ShapeDtypeStruct((b * t, d), BF16),
                   jax.ShapeDtypeStruct((b * t, LANES), jnp.int32), jax.ShapeDtypeStruct((b * t, LANES), F32)],
        grid=(b // bb, nt),
        in_specs=[tok(d),
                  pl.BlockSpec((d, d), lambda i, j: (0, 0)),
                  pl.BlockSpec((bb, tt, d), lambda i, j: (i, j, 0)),
                  modspec(2),
                  pl.BlockSpec((1, 1, d), lambda i, j: (0, 0, 0)),
                  modspec(4), modspec(3),
                  pl.BlockSpec((d, LANES), lambda i, j: (0, 0)),
                  pl.BlockSpec((1, LANES), lambda i, j: (0, 0))],
        out_specs=[pl.BlockSpec((bb, tt, d), lambda i, j: (i, j, 0)), tok(d), tok(LANES), tok(LANES)],
        compiler_params=_params(("arbitrary", "arbitrary")),
    )(merged, w_out_b, x, mod3, g_norm2.reshape(1, 1, d), mod3, mod3, wr_pad, br_pad)


def _moe_up_kernel(be_ref, x_ref, wg_ref, wl_ref, bg_ref, bl_ref, o_ref, wg_s, wl_s):
    i = pl.program_id(1)
    e = be_ref[i]
    e_prev = be_ref[jnp.maximum(i - 1, 0)]

    @pl.when((i == 0) | (e != e_prev))
    def _():
        wg_s[...] = wg_ref[...].astype(BF16)
        wl_s[...] = wl_ref[...].astype(BF16)

    x = x_ref[...]
    glu = jnp.minimum(_dot(x, wg_s[...]) + bg_ref[...], SWIGLU_LIMIT)
    lin = jnp.clip(_dot(x, wl_s[...]) + bl_ref[...], -SWIGLU_LIMIT, SWIGLU_LIMIT)
    o_ref[...] = (glu * _sigmoid(SWIGLU_ALPHA * glu) * (lin + 1.0)).astype(o_ref.dtype)


def _moe_down_kernel(be_ref, a_ref, w_ref, b_ref, o_ref, w_s):
    i = pl.program_id(0)
    e = be_ref[i]
    e_prev = be_ref[jnp.maximum(i - 1, 0)]

    @pl.when((i == 0) | (e != e_prev))
    def _():
        w_s[...] = w_ref[...].astype(BF16)

    o_ref[...] = _dot(a_ref[...], w_s[...]) + b_ref[...]


def _moe_experts(block_e, x_sorted, w_gate_up, b_gate_up, w_down, b_down):
    n_rows, d = x_sorted.shape
    n_blocks = n_rows // MOE_ROWS
    tf = 1024
    nf = D_FF // tf
    bgu3 = b_gate_up.reshape(N_EXPERTS, 1, 2 * D_FF)
    act = pl.pallas_call(
        _moe_up_kernel,
        out_shape=jax.ShapeDtypeStruct((n_rows, D_FF), BF16),
        grid_spec=pltpu.PrefetchScalarGridSpec(
            num_scalar_prefetch=1,
            grid=(nf, n_blocks),
            in_specs=[pl.BlockSpec((MOE_ROWS, d), lambda j, i, be: (i, 0)),
                      pl.BlockSpec((None, d, tf), lambda j, i, be: (be[i], 0, j)),
                      pl.BlockSpec((None, d, tf), lambda j, i, be: (be[i], 0, nf + j)),
                      pl.BlockSpec((None, 1, tf), lambda j, i, be: (be[i], 0, j)),
                      pl.BlockSpec((None, 1, tf), lambda j, i, be: (be[i], 0, nf + j))],
            out_specs=pl.BlockSpec((MOE_ROWS, tf), lambda j, i, be: (i, j)),
            scratch_shapes=[pltpu.VMEM((d, tf), BF16), pltpu.VMEM((d, tf), BF16)],
        ),
        compiler_params=_params(("arbitrary", "arbitrary")),
    )(block_e, x_sorted, w_gate_up, w_gate_up, bgu3, bgu3)
    return pl.pallas_call(
        _moe_down_kernel,
        out_shape=jax.ShapeDtypeStruct((n_rows, d), F32),
        grid_spec=pltpu.PrefetchScalarGridSpec(
            num_scalar_prefetch=1,
            grid=(n_blocks,),
            in_specs=[pl.BlockSpec((MOE_ROWS, D_FF), lambda i, be: (i, 0)),
                      pl.BlockSpec((None, D_FF, d), lambda i, be: (be[i], 0, 0)),
                      pl.BlockSpec((None, 1, d), lambda i, be: (be[i], 0, 0))],
            out_specs=pl.BlockSpec((MOE_ROWS, d), lambda i, be: (i, 0)),
            scratch_shapes=[pltpu.VMEM((D_FF, d), BF16)],
        ),
        compiler_params=_params(("arbitrary",)),
    )(block_e, act, w_down, b_down.reshape(N_EXPERTS, 1, d))


def _combine_kernel(x1_ref, g2_ref, y_ref, gate_ref, o_ref):
    d = x1_ref.shape[-1]
    gates = gate_ref[...]
    ff = y_ref[:, 0:d] * gates[:, 0:1]
    for k in range(1, TOP_K):
        ff = ff + y_ref[:, k * d:(k + 1) * d] * gates[:, k:k + 1]
    o_ref[...] = x1_ref[...] + g2_ref[...] * ff.reshape(x1_ref.shape)


def _combine(x1, mod3, yk, gates, bb, tt):
    b, t, d = x1.shape
    nt = t // tt
    return pl.pallas_call(
        _combine_kernel,
        out_shape=jax.ShapeDtypeStruct((b, t, d), F32),
        grid=(b // bb, nt),
        in_specs=[pl.BlockSpec((bb, tt, d), lambda i, j: (i, j, 0)),
                  pl.BlockSpec((bb, 1, d), lambda i, j: (i, 0, 5)),
                  pl.BlockSpec((bb * tt, TOP_K * d), lambda i, j: (i * nt + j, 0)),
                  pl.BlockSpec((bb * tt, LANES), lambda i, j: (i * nt + j, 0))],
        out_specs=pl.BlockSpec((bb, tt, d), lambda i, j: (i, j, 0)),
        compiler_params=_params(("arbitrary", "arbitrary")),
    )(x1, mod3, yk.reshape(b * t, TOP_K * d), gates)


def _moe(h2, top_idx, gates, x1, mod3, w_gate_up, b_gate_up, w_down, b_down, bb, tt):
    t = h2.shape[0]
    n_assign = t * TOP_K
    flat_e = top_idx[:, :TOP_K].reshape(-1)
    flat_tok = jnp.arange(n_assign, dtype=jnp.int32) // TOP_K
    order = jnp.argsort(flat_e)
    sorted_e = flat_e[order]
    counts = jnp.bincount(flat_e, length=N_EXPERTS).astype(jnp.int32)
    padded = (counts + MOE_ROWS - 1) // MOE_ROWS * MOE_ROWS
    pad_end = jnp.cumsum(padded)
    pad_start = pad_end - padded
    start = jnp.cumsum(counts) - counts
    dest = pad_start[sorted_e] + jnp.arange(n_assign, dtype=jnp.int32) - start[sorted_e]
    n_blocks = -(-(n_assign + N_EXPERTS * (MOE_ROWS - 1)) // MOE_ROWS)
    n_rows = n_blocks * MOE_ROWS
    row_tok = jnp.zeros((n_rows,), jnp.int32).at[dest].set(flat_tok[order])
    block_e = jnp.minimum(jnp.searchsorted(pad_end, jnp.arange(n_blocks, dtype=jnp.int32) * MOE_ROWS, side='right'),
                          N_EXPERTS - 1).astype(jnp.int32)
    x_sorted = h2[row_tok]
    out_rows = _moe_experts(block_e, x_sorted, w_gate_up, b_gate_up, w_down, b_down)
    dest_by_assign = jnp.zeros((n_assign,), jnp.int32).at[order].set(dest)
    yk = out_rows[dest_by_assign]
    return _combine(x1, mod3, yk, gates, bb, tt)


def _layer(x, mod, pos, p, *, prompt, caches=None, page_table=None):
    b, t, d = x.shape
    tokens = b * t
    mod3 = mod.reshape(b, 1, 6 * d)
    bb, tt = (1, 512) if prompt else (32, t)
    h = _normmod(x, p['g_norm1'], mod3, 1, 0, bb, tt)
    z = _mm(h, p['w_in'], min(tokens, 1024), Z_COLS // 5)
    lc = min(t, CHUNK)
    reps = CHUNK // lc
    ws_t = jnp.tile(p['w_s'][:, :lc, :lc], (1, reps, reps))
    bs_t = jnp.broadcast_to(jnp.tile(p['b_s'][:, :lc], (1, reps))[:, :, None], (A_GROUPS, CHUNK, CHUNK))
    gm = _gmlp(z, p['g_v_ln'], p['b_v_ln'], ws_t, bs_t, lc, want_vn=not prompt)
    a_out = gm[0]

    half = ROPE_DIM // 2
    inv_freq = jnp.exp(-math.log(ROPE_THETA) * jnp.arange(half, dtype=F32) / half)
    ang = pos.astype(F32)[:, None] * inv_freq[None, :]
    cos, sin = jnp.cos(ang), jnp.sin(ang)
    tab = jnp.concatenate([cos, cos, -sin, sin], axis=-1)
    tab = jnp.broadcast_to(tab[None], (b, t, LANES)).reshape(tokens, LANES)
    swap = lambda g: jnp.concatenate([g[half:], g[:half]])
    gains = [p['g_q_lat'].reshape(1, -1), p['g_kv_lat'].reshape(1, -1), p['g_q_nope'].reshape(1, -1),
             jnp.concatenate([p['g_q_rope'], swap(p['g_q_rope'])]).reshape(1, -1),
             jnp.concatenate([p['g_k_rope'], swap(p['g_k_rope'])]).reshape(1, -1),
             p['g_k_nope'].reshape(1, -1)]
    weights = [p['wq_n'], p['wq_r'], p['w_uk'], p['w_uv']]
    mla = _mla(z, tab, gains, weights, 256, prompt)
    qn, qp, c_kv, k_pe, rinv = mla[:5]
    if prompt:
        kn, v, kp = mla[5:]
        b_out = _attend_prompt(qn, qp, kn, kp, v, 512)
    else:
        cache_latent, cache_k_rope, cache_rinv_t = caches
        qa, qr = _absorb(qn, qp, p['w_uk'], b, t)
        rows = N_HEADS * t
        pad = PAGE_SIZE - t
        nlat = jnp.pad(c_kv.reshape(b, t, KV_LORA), ((0, 0), (0, pad), (0, 0)))
        nrope = jnp.pad(k_pe.reshape(b, t, ROPE_DIM), ((0, 0), (0, pad), (0, 0)))
        nrinv_t = jnp.pad(jnp.swapaxes(rinv.reshape(b, t, N_HEADS), 1, 2), ((0, 0), (0, 0), (0, pad)))
        o_lat = _decode(page_table, qa.reshape(b, rows, KV_LORA), qr.reshape(b, rows, LANES), nlat, nrope, nrinv_t,
                        cache_latent, cache_k_rope, cache_rinv_t, t)
        b_out = _upv(o_lat.reshape(b, N_HEADS, t, KV_LORA), p['w_uv'], b, t)
    merged = _merge(a_out, b_out, p['w_proj_a'], p['w_proj_b'], z, min(tokens, 512), 1024)
    bb2, tt2 = (1, 256) if prompt else (32, t)
    x1, h2, top_idx, gates = _outproj(merged, p['w_out'], x, mod3, p['g_norm2'], p['wr_pad'], p['br_pad'], bb2, tt2)
    y = _moe(h2, top_idx, gates, x1, mod3, p['w_gate_up'], p['b_gate_up'], p['w_down'], p['b_down'], bb2, tt2)
    vn = None if prompt else gm[1]
    return y, c_kv, k_pe, rinv, vn


def kernel(x_prompt, x_sample, c_prompt, c_sample, cache_latent, cache_k_rope, cache_k_rinv, page_table,
           w_ada, b_ada, g_norm1, w_in, g_v_ln, b_v_ln, w_s, b_s, g_q_lat, w_uq, g_q_nope, g_q_rope,
           g_kv_lat, g_k_rope, w_uk, w_uv, g_k_nope, w_proj_a, w_proj_b, w_out, g_norm2,
           w_router, b_router, w_gate_up, b_gate_up, w_down, b_down):
    depth = w_ada.shape[0]
    bp, sp, d = x_prompt.shape
    bs, ss, _ = x_sample.shape
    past_len = page_table.shape[1] * PAGE_SIZE
    pos_p = jnp.arange(sp, dtype=jnp.int32)
    pos_s = past_len + jnp.arange(ss, dtype=jnp.int32)
    half = ROPE_DIM // 2
    y_p, y_s = x_prompt, x_sample
    outs = [[] for _ in range(7)]
    for l in range(depth):
        wi = w_in[l]
        kr = wi[:, 2816:2880]
        w_in_perm = jnp.concatenate(
            [wi[:, 0:2048], wi[:, 2880:6976], wi[:, 2048:2816], kr, kr[:, half:], kr[:, :half]], axis=1).astype(BF16)
        wq = w_uq[l]
        wq_rope = wq[:, :, NOPE_DIM:]
        p = {
            'g_norm1': g_norm1[l], 'w_in': w_in_perm, 'g_v_ln': g_v_ln[l], 'b_v_ln': b_v_ln[l],
            'w_s': w_s[l], 'b_s': b_s[l], 'g_q_lat': g_q_lat[l], 'g_q_nope': g_q_nope[l], 'g_q_rope': g_q_rope[l],
            'g_kv_lat': g_kv_lat[l], 'g_k_rope': g_k_rope[l], 'g_k_nope': g_k_nope[l],
            'wq_n': wq[:, :, :NOPE_DIM].reshape(Q_LORA, -1).astype(BF16),
            'wq_r': jnp.concatenate([wq_rope, wq_rope[:, :, half:], wq_rope[:, :, :half]],
                                    axis=-1).reshape(Q_LORA, -1).astype(BF16),
            'w_uk': w_uk[l].reshape(KV_LORA, -1).astype(BF16), 'w_uv': w_uv[l].reshape(KV_LORA, -1).astype(BF16),
            'w_proj_a': w_proj_a[l].astype(BF16), 'w_proj_b': w_proj_b[l].astype(BF16),
            'w_out': w_out[l].astype(BF16), 'g_norm2': g_norm2[l],
            'wr_pad': jnp.pad(w_router[l], ((0, 0), (0, LANES - N_EXPERTS))).astype(BF16),
            'br_pad': jnp.pad(b_router[l], (0, LANES - N_EXPERTS), constant_values=NEG_INF).reshape(1, LANES),
            'w_gate_up': w_gate_up[l], 'b_gate_up': b_gate_up[l], 'w_down': w_down[l], 'b_down': b_down[l],
        }
        n_c = bp + bs
        c_all = jnp.pad(jnp.concatenate([c_prompt, c_sample], axis=0), ((0, -n_c % 8), (0, 0)))
        mod = _ada(c_all, w_ada[l], b_ada[l])
        caches = (cache_latent[l:l + 1], cache_k_rope[l:l + 1], jnp.swapaxes(cache_k_rinv[l:l + 1], 2, 3))
        y_p, ckv_p, kpe_p, rinv_p, _ = _layer(y_p, mod[:bp], pos_p, p, prompt=True)
        y_s, ckv_s, kpe_s, rinv_s, vn_s = _layer(y_s, mod[bp:n_c], pos_s, p, prompt=False,
                                                 caches=caches, page_table=page_table)
        outs[0].append(ckv_p.reshape(bp, sp, KV_LORA))
        outs[1].append(kpe_p.reshape(bp, sp, ROPE_DIM))
        outs[2].append(rinv_p.reshape(bp, sp, N_HEADS))
        outs[3].append(ckv_s.reshape(bs, ss, KV_LORA))
        outs[4].append(kpe_s.reshape(bs, ss, ROPE_DIM))
        outs[5].append(rinv_s.reshape(bs, ss, N_HEADS))
        outs[6].append(vn_s.reshape(bs, ss, D_A))
    return (y_p, y_s) + tuple(jnp.stack(o) for o in outs)
```

```python
import functools
import math

import jax
import jax.numpy as jnp
from jax import lax
from jax.experimental import pallas as pl
from jax.experimental.pallas import tpu as pltpu

F32 = jnp.float32
BF16 = jnp.bfloat16

D_MODEL = 2048
D_A = D_MODEL // 2
A_GROUP = 128
A_GROUPS = D_A // A_GROUP
CHUNK = 128
N_HEADS = 16
Q_LORA = D_MODEL // 4
KV_LORA = D_MODEL // 8
NOPE_DIM = 128
ROPE_DIM = 64
V_DIM = 128
QK_DIM = NOPE_DIM + ROPE_DIM
ROPE_THETA = 10000.0
SCALE = 1.0 / math.sqrt(QK_DIM)
N_EXPERTS = 32
TOP_K = 4
D_FF = D_MODEL
SWIGLU_LIMIT = 7.0
SWIGLU_ALPHA = 1.702
EPS = 1e-6
NEG_INF = -1e30
PAGE_SIZE = 128

LANES = 128
MOE_ROWS = 128
GATHER_ROWS = 512
COMBINE_TOKENS = 128
PAGES_PER_STEP = 16
VMEM_LIMIT = 56 * 1024 * 1024

ZO_U = 0
ZO_V = ZO_U + D_A
ZO_GA = ZO_V + D_A
ZO_GB = ZO_GA + D_MODEL
ZO_Q = ZO_GB + D_MODEL
ZO_KV = ZO_Q + Q_LORA
ZO_KR = ZO_KV + KV_LORA
Z_COLS = ZO_KR + 2 * ROPE_DIM


def _params(sem):
    return pltpu.CompilerParams(dimension_semantics=sem, vmem_limit_bytes=VMEM_LIMIT)


def _dot(a, b):
    return jnp.dot(a, b, preferred_element_type=F32)


def _dot_nt(a, b):
    return lax.dot_general(a, b, (((1,), (1,)), ((), ())), preferred_element_type=F32)


def _sigmoid(x):
    return 1.0 / (1.0 + jnp.exp(-x))


def _gelu(x):
    c = math.sqrt(2.0 / math.pi)
    return 0.5 * x * (1.0 + jnp.tanh(c * (x + 0.044715 * (x * x * x))))


def _rms(x):
    return x * lax.rsqrt(jnp.mean(x * x, axis=-1, keepdims=True) + EPS)


def _ada_kernel(c_ref, w_ref, b_ref, o_ref):
    c = c_ref[...]
    a = (c * _sigmoid(c)).astype(BF16)
    o_ref[...] = _dot(a, w_ref[...].astype(BF16)) + b_ref[...]


def _ada(c_all, w_ada, b_ada):
    rows, d = c_all.shape
    n = w_ada.shape[1]
    tn = 1024
    return pl.pallas_call(
        _ada_kernel,
        out_shape=jax.ShapeDtypeStruct((rows, n), F32),
        grid=(n // tn,),
        in_specs=[pl.BlockSpec((rows, d), lambda j: (0, 0)),
                  pl.BlockSpec((d, tn), lambda j: (0, j)),
                  pl.BlockSpec((1, tn), lambda j: (0, j))],
        out_specs=pl.BlockSpec((rows, tn), lambda j: (0, j)),
        compiler_params=_params(("arbitrary",)),
    )(c_all, w_ada, b_ada.reshape(1, n))


def _normmod_kernel(x_ref, g_ref, sc_ref, sh_ref, o_ref):
    x = x_ref[...]
    y = _rms(x) * g_ref[...]
    y = y * (1.0 + sc_ref[...]) + sh_ref[...]
    o_ref[...] = y.reshape(o_ref.shape).astype(o_ref.dtype)


def _normmod(x, g, mod3, sc_idx, sh_idx, bb, tt):
    b, t, d = x.shape
    return pl.pallas_call(
        _normmod_kernel,
        out_shape=jax.ShapeDtypeStruct((b * t, d), BF16),
        grid=(b // bb, t // tt),
        in_specs=[pl.BlockSpec((bb, tt, d), lambda i, j: (i, j, 0)),
                  pl.BlockSpec((1, 1, d), lambda i, j: (0, 0, 0)),
                  pl.BlockSpec((bb, 1, d), lambda i, j: (i, 0, sc_idx)),
                  pl.BlockSpec((bb, 1, d), lambda i, j: (i, 0, sh_idx))],
        out_specs=pl.BlockSpec((bb * tt, d), lambda i, j: (i * (t // tt) + j, 0)),
        compiler_params=_params(("arbitrary", "arbitrary")),
    )(x, g.reshape(1, 1, d), mod3, mod3)


def _mm_kernel(x_ref, w_ref, o_ref):
    o_ref[...] = _dot(x_ref[...], w_ref[...]).astype(o_ref.dtype)


def _mm(x, w, tm, tn, out_dtype=F32):
    m, k = x.shape
    n = w.shape[1]
    return pl.pallas_call(
        _mm_kernel,
        out_shape=jax.ShapeDtypeStruct((m, n), out_dtype),
        grid=(m // tm, n // tn),
        in_specs=[pl.BlockSpec((tm, k), lambda i, j: (i, 0)),
                  pl.BlockSpec((k, tn), lambda i, j: (0, j))],
        out_specs=pl.BlockSpec((tm, tn), lambda i, j: (i, j)),
        compiler_params=_params(("arbitrary", "arbitrary")),
    )(x, w)


def _gmlp_kernel(zu_ref, zv_ref, gln_ref, bln_ref, ws_ref, bs_ref, a_ref, *maybe_vn_ref, lc):
    u = _gelu(zu_ref[...])
    gv = _gelu(zv_ref[...])
    xc = gv - jnp.mean(gv, axis=-1, keepdims=True)
    vn = xc * lax.rsqrt(jnp.mean(xc * xc, axis=-1, keepdims=True) + EPS)
    vn = vn * gln_ref[...] + bln_ref[...]
    if maybe_vn_ref:
        maybe_vn_ref[0][...] = vn
    row = lax.broadcasted_iota(jnp.int32, (CHUNK, CHUNK), 0)
    col = lax.broadcasted_iota(jnp.int32, (CHUNK, CHUNK), 1)
    mask = (col <= row) & ((row // lc) == (col // lc))
    vb = vn.astype(BF16)
    for g in range(A_GROUPS):
        w = jnp.where(mask, ws_ref[g], 0.0).astype(BF16)
        sl = slice(g * A_GROUP, (g + 1) * A_GROUP)
        s = _dot(w, vb[:, sl]) + bs_ref[g]
        a_ref[:, sl] = (u[:, sl] * s).astype(a_ref.dtype)


def _gmlp(z, g_ln, b_ln, ws_t, bs_t, lc, want_vn):
    t = z.shape[0]
    out_shape = [jax.ShapeDtypeStruct((t, D_A), BF16)]
    out_specs = [pl.BlockSpec((CHUNK, D_A), lambda i: (i, 0))]
    if want_vn:
        out_shape.append(jax.ShapeDtypeStruct((t, D_A), F32))
        out_specs.append(pl.BlockSpec((CHUNK, D_A), lambda i: (i, 0)))
    return pl.pallas_call(
        functools.partial(_gmlp_kernel, lc=lc),
        out_shape=out_shape,
        grid=(t // CHUNK,),
        in_specs=[pl.BlockSpec((CHUNK, D_A), lambda i: (i, ZO_U // D_A)),
                  pl.BlockSpec((CHUNK, D_A), lambda i: (i, ZO_V // D_A)),
                  pl.BlockSpec((1, D_A), lambda i: (0, 0)),
                  pl.BlockSpec((1, D_A), lambda i: (0, 0)),
                  pl.BlockSpec((A_GROUPS, CHUNK, CHUNK), lambda i: (0, 0, 0)),
                  pl.BlockSpec((A_GROUPS, CHUNK, CHUNK), lambda i: (0, 0, 0))],
        out_specs=out_specs,
        compiler_params=_params(("arbitrary",)),
    )(z, z, g_ln.reshape(1, D_A), b_ln.reshape(1, D_A), ws_t, bs_t)


def _rope_pair(y2, gain2, tab):
    p = y2 * gain2 * tab
    return p + pltpu.roll(p, ROPE_DIM, 1)


def _mla_kernel(zq_ref, zkv_ref, zkr_ref, tab_ref, gql_ref, gkv_ref, gqn_ref, gq2_ref, gk2_ref, gkn_ref,
                wqn_ref, wqr_ref, wuk_ref, wuv_ref, ckv_ref, kpe_ref, rinv_ref, *qkv_refs, prompt):
    tab = tab_ref[...]
    q_lat = (_rms(zq_ref[...]) * gql_ref[...]).astype(BF16)
    c_kv = _rms(zkv_ref[...]) * gkv_ref[...]
    ckv_ref[...] = c_kv
    kr2 = zkr_ref[...]
    kr2 = kr2 * lax.rsqrt(jnp.sum(kr2 * kr2, axis=-1, keepdims=True) / (2 * ROPE_DIM) + EPS)
    kpe2 = _rope_pair(kr2, gk2_ref[...], tab)
    kpe_ref[...] = kpe2[:, :ROPE_DIM]
    cb = c_kv.astype(BF16)
    tm = cb.shape[0]
    lane = lax.broadcasted_iota(jnp.int32, (tm, N_HEADS), 1)
    rinv_all = jnp.zeros((tm, N_HEADS), F32)
    if prompt:
        col = lax.broadcasted_iota(jnp.int32, kpe2.shape, 1)
        kp_b = jnp.where(col < ROPE_DIM, kpe2, 0.0).astype(BF16)
    for h in range(N_HEADS):
        sl = slice(h * NOPE_DIM, (h + 1) * NOPE_DIM)
        qn = _rms(_dot(q_lat, wqn_ref[:, sl])) * gqn_ref[...]
        qr2 = _dot(q_lat, wqr_ref[:, sl])
        qr2 = qr2 * lax.rsqrt(jnp.sum(qr2 * qr2, axis=-1, keepdims=True) / (2 * ROPE_DIM) + EPS)
        qp = (_rope_pair(qr2, gq2_ref[...], tab) * SCALE).astype(BF16)
        k_raw = _dot(cb, wuk_ref[:, sl])
        rinv = lax.rsqrt(jnp.mean(k_raw * k_raw, axis=-1, keepdims=True) + EPS)
        rinv_all = jnp.where(lane == h, rinv, rinv_all)
        if prompt:
            q_ref, k_ref, v_ref = qkv_refs
            lo = slice(2 * h * NOPE_DIM, (2 * h + 1) * NOPE_DIM)
            hi = slice((2 * h + 1) * NOPE_DIM, (2 * h + 2) * NOPE_DIM)
            q_ref[:, lo] = (qn * SCALE).astype(BF16)
            q_ref[:, hi] = qp
            k_ref[:, lo] = (k_raw * rinv * gkn_ref[...]).astype(BF16)
            k_ref[:, hi] = kp_b
            v_ref[:, sl] = _dot(cb, wuv_ref[:, sl]).astype(BF16)
        else:
            qn_ref, qp_ref = qkv_refs
            qn_ref[:, sl] = (qn * gkn_ref[...]).astype(BF16)
            qp_ref[:, sl] = qp
    rinv_ref[...] = rinv_all


def _mla(z, tab, gains, weights, tm, prompt):
    t = z.shape[0]
    hd = N_HEADS * NOPE_DIM
    row = lambda w: pl.BlockSpec((tm, w), lambda i: (i, 0))
    full = lambda a: pl.BlockSpec(a.shape, lambda i: (0,) * a.ndim)
    out_shape = [jax.ShapeDtypeStruct((t, KV_LORA), F32), jax.ShapeDtypeStruct((t, ROPE_DIM), F32),
                 jax.ShapeDtypeStruct((t, N_HEADS), F32)]
    out_specs = [row(KV_LORA), row(ROPE_DIM), row(N_HEADS)]
    widths = (2 * hd, 2 * hd, hd) if prompt else (hd, hd)
    out_shape += [jax.ShapeDtypeStruct((t, w), BF16) for w in widths]
    out_specs += [row(w) for w in widths]
    return pl.pallas_call(
        functools.partial(_mla_kernel, prompt=prompt),
        out_shape=out_shape,
        grid=(t // tm,),
        in_specs=[pl.BlockSpec((tm, Q_LORA), lambda i: (i, ZO_Q // Q_LORA)),
                  pl.BlockSpec((tm, KV_LORA), lambda i: (i, ZO_KV // KV_LORA)),
                  pl.BlockSpec((tm, LANES), lambda i: (i, ZO_KR // LANES)),
                  row(LANES)] + [full(a) for a in gains] + [full(a) for a in weights],
        out_specs=out_specs,
        compiler_params=_params(("arbitrary",)),
    )(z, z, z, tab, *gains, *weights)


def _fa_kernel(q_ref, k_ref, v_ref, o_ref, *, tq):
    i = pl.program_id(1)
    q = q_ref[...]

    def step(j, carry, masked):
        m, l, acc = carry
        ks = pl.ds(pl.multiple_of(j * tq, tq), tq)
        s = _dot_nt(q, k_ref[ks, :])
        if masked:
            row = lax.broadcasted_iota(jnp.int32, (tq, tq), 0)
            col = lax.broadcasted_iota(jnp.int32, (tq, tq), 1)
            s = jnp.where(col <= row, s, NEG_INF)
        m_new = jnp.maximum(m, jnp.max(s, axis=-1, keepdims=True))
        corr = jnp.exp(m - m_new)
        p = jnp.exp(s - m_new)
        l = l * corr + jnp.sum(p, axis=-1, keepdims=True)
        acc = acc * corr + _dot(p.astype(BF16), v_ref[ks, :])
        return m_new, l, acc

    init = (jnp.full((tq, 1), NEG_INF, F32), jnp.zeros((tq, 1), F32), jnp.zeros((tq, V_DIM), F32))
    carry = lax.fori_loop(0, i, functools.partial(step, masked=False), init)
    m, l, acc = step(i, carry, True)
    o_ref[...] = (acc / l).astype(o_ref.dtype)


def _attend_prompt(q, k, v, tq):
    t = q.shape[0]
    return pl.pallas_call(
        functools.partial(_fa_kernel, tq=tq),
        out_shape=jax.ShapeDtypeStruct((t, N_HEADS * V_DIM), BF16),
        grid=(N_HEADS, t // tq),
        in_specs=[pl.BlockSpec((tq, 2 * NOPE_DIM), lambda h, i: (i, h)),
                  pl.BlockSpec((t, 2 * NOPE_DIM), lambda h, i: (0, h)),
                  pl.BlockSpec((t, V_DIM), lambda h, i: (0, h))],
        out_specs=pl.BlockSpec((tq, V_DIM), lambda h, i: (i, h)),
        compiler_params=_params(("arbitrary", "arbitrary")),
    )(q, k, v)


def _absorb_kernel(qn_ref, wuk_ref, qa_ref):
    qa_ref[...] = (_dot_nt(qn_ref[...], wuk_ref[...]) * SCALE).astype(qa_ref.dtype)


def _absorb(qn, w_uk_b):
    tokens = qn.shape[0]
    return pl.pallas_call(
        _absorb_kernel,
        out_shape=jax.ShapeDtypeStruct((tokens, N_HEADS * KV_LORA), BF16),
        grid=(N_HEADS,),
        in_specs=[pl.BlockSpec((tokens, NOPE_DIM), lambda h: (0, h)),
                  pl.BlockSpec((KV_LORA, NOPE_DIM), lambda h: (0, h))],
        out_specs=pl.BlockSpec((tokens, KV_LORA), lambda h: (0, h)),
        compiler_params=_params(("arbitrary",)),
    )(qn, w_uk_b)


def _decode_kernel(pt_ref, qa_ref, qr_ref, nlat_ref, nrope_ref, nrinv_ref, lat_hbm, rope_hbm, rinv_hbm, o_ref,
                   lat_buf, rope_buf, rinv_buf, sems, ck_s, kr_s, ri_s, m_ref, l_ref, acc_ref, *, t_new):
    pc = PAGES_PER_STEP
    b, c = pl.program_id(0), pl.program_id(1)
    nb, nc = pl.num_programs(0), pl.num_programs(1)
    n = b * nc + c
    slot = n % 2
    rows = N_HEADS * t_new

    def page_copies(bi, ci, sl):
        copies = []
        for j in range(pc):
            page = pt_ref[bi, ci * pc + j]
            copies.append(pltpu.make_async_copy(lat_hbm.at[page], lat_buf.at[sl, j], sems.at[sl, 0]))
            copies.append(pltpu.make_async_copy(rope_hbm.at[page], rope_buf.at[sl, j], sems.at[sl, 1]))
            copies.append(pltpu.make_async_copy(rinv_hbm.at[page], rinv_buf.at[sl, j], sems.at[sl, 2]))
        return copies

    @pl.when(n == 0)
    def _():
        for cp in page_copies(b, c, slot):
            cp.start()

    @pl.when(n + 1 < nb * nc)
    def _():
        wrap = c + 1 == nc
        for cp in page_copies(jnp.where(wrap, b + 1, b), jnp.where(wrap, 0, c + 1), 1 - slot):
            cp.start()

    for cp in page_copies(b, c, slot):
        cp.wait()
    qa = qa_ref[...]
    qr = qr_ref[...][:, :ROPE_DIM]

    def attend(ck, krt, rit, causal):
        n = ck.shape[0]
        s = (_dot_nt(qa, ck).reshape(t_new, N_HEADS, n) * rit[None]).reshape(rows, n) + _dot(qr, krt)
        if causal:
            row = lax.broadcasted_iota(jnp.int32, s.shape, 0)
            col = lax.broadcasted_iota(jnp.int32, s.shape, 1)
            s = jnp.where(col <= row // N_HEADS, s, NEG_INF)
        m_old = m_ref[...]
        m_new = jnp.maximum(m_old, jnp.max(s, axis=-1, keepdims=True))
        corr = jnp.exp(m_old - m_new)
        p = jnp.exp(s - m_new)
        l_ref[...] = l_ref[...] * corr + jnp.sum(p, axis=-1, keepdims=True)
        m_ref[...] = m_new
        acc_ref[...] = acc_ref[...] * corr + _dot(p.astype(BF16), ck)

    @pl.when(c == 0)
    def _():
        m_ref[...] = jnp.full(m_ref.shape, NEG_INF, F32)
        l_ref[...] = jnp.zeros(l_ref.shape, F32)
        acc_ref[...] = jnp.zeros(acc_ref.shape, F32)
        attend(nlat_ref[...].astype(BF16), nrope_ref[...].astype(BF16), nrinv_ref[...], True)

    for j in range(pc):
        ck_s[j * PAGE_SIZE:(j + 1) * PAGE_SIZE, :] = lat_buf[slot, j].astype(BF16)
        kr_s[:, j * PAGE_SIZE:(j + 1) * PAGE_SIZE] = rope_buf[slot, j].astype(BF16)
        ri_s[:, j * PAGE_SIZE:(j + 1) * PAGE_SIZE] = rinv_buf[slot, j]
    attend(ck_s[...], kr_s[...], ri_s[...], False)

    @pl.when(c == nc - 1)
    def _():
        o_ref[...] = (acc_ref[...] / l_ref[...]).astype(o_ref.dtype)


def _decode(page_table, qa, qr, nlat, nrope_t, nrinv_t, cache_latent, cache_rope_t, cache_rinv_t, t_new):
    b, n_pages = page_table.shape
    pc = PAGES_PER_STEP
    rows = N_HEADS * t_new
    seq = lambda r, w: pl.BlockSpec((None, r, w), lambda i, c, pt: (i, 0, 0))
    hbm = pl.BlockSpec(memory_space=pl.ANY)
    keys = pc * PAGE_SIZE
    grid_spec = pltpu.PrefetchScalarGridSpec(
        num_scalar_prefetch=1,
        grid=(b, n_pages // pc),
        in_specs=[seq(rows, KV_LORA), seq(rows, LANES),
                  seq(PAGE_SIZE, KV_LORA), seq(ROPE_DIM, PAGE_SIZE), seq(N_HEADS, PAGE_SIZE), hbm, hbm, hbm],
        out_specs=seq(rows, KV_LORA),
        scratch_shapes=[pltpu.VMEM((2, pc, PAGE_SIZE, KV_LORA), F32), pltpu.VMEM((2, pc, ROPE_DIM, PAGE_SIZE), F32),
                        pltpu.VMEM((2, pc, N_HEADS, PAGE_SIZE), F32), pltpu.SemaphoreType.DMA((2, 3)),
                        pltpu.VMEM((keys, KV_LORA), BF16), pltpu.VMEM((ROPE_DIM, keys), BF16),
                        pltpu.VMEM((N_HEADS, keys), F32),
                        pltpu.VMEM((rows, 1), F32), pltpu.VMEM((rows, 1), F32), pltpu.VMEM((rows, KV_LORA), F32)],
    )
    return pl.pallas_call(
        functools.partial(_decode_kernel, t_new=t_new),
        out_shape=jax.ShapeDtypeStruct((b, rows, KV_LORA), F32),
        grid_spec=grid_spec,
        compiler_params=_params(("arbitrary", "arbitrary")),
    )(page_table, qa, qr, nlat, nrope_t, nrinv_t, cache_latent, cache_rope_t, cache_rinv_t)


def _upv_kernel(o_ref, wuv_ref, out_ref):
    out_ref[...] = _dot(o_ref[...].astype(BF16), wuv_ref[...]).astype(out_ref.dtype)


def _upv(o_lat, w_uv_b):
    tokens = o_lat.shape[0]
    return pl.pallas_call(
        _upv_kernel,
        out_shape=jax.ShapeDtypeStruct((tokens, N_HEADS * V_DIM), BF16),
        grid=(N_HEADS,),
        in_specs=[pl.BlockSpec((tokens, KV_LORA), lambda h: (0, h)),
                  pl.BlockSpec((KV_LORA, V_DIM), lambda h: (0, h))],
        out_specs=pl.BlockSpec((tokens, V_DIM), lambda h: (0, h)),
        compiler_params=_params(("arbitrary",)),
    )(o_lat, w_uv_b)


def _merge_kernel(a_ref, b_ref, wa_ref, wb_ref, ga_ref, gb_ref, o_ref):
    pa = _dot(a_ref[...], wa_ref[...])
    pb = _dot(b_ref[...], wb_ref[...])
    o_ref[...] = (_sigmoid(ga_ref[...]) * pa + _sigmoid(gb_ref[...]) * pb).astype(o_ref.dtype)


def _merge(a_out, b_out, wpa, wpb, z, tm, tn):
    t = a_out.shape[0]
    return pl.pallas_call(
        _merge_kernel,
        out_shape=jax.ShapeDtypeStruct((t, D_MODEL), BF16),
        grid=(t // tm, D_MODEL // tn),
        in_specs=[pl.BlockSpec((tm, D_A), lambda i, j: (i, 0)),
                  pl.BlockSpec((tm, D_MODEL), lambda i, j: (i, 0)),
                  pl.BlockSpec((D_A, tn), lambda i, j: (0, j)),
                  pl.BlockSpec((D_MODEL, tn), lambda i, j: (0, j)),
                  pl.BlockSpec((tm, tn), lambda i, j: (i, ZO_GA // tn + j)),
                  pl.BlockSpec((tm, tn), lambda i, j: (i, ZO_GB // tn + j))],
        out_specs=pl.BlockSpec((tm, tn), lambda i, j: (i, j)),
        compiler_params=_params(("arbitrary", "arbitrary")),
    )(a_out, b_out, wpa, wpb, z, z)


def _outproj_kernel(m_ref, wo_ref, x_ref, g1_ref, gn_ref, sc_ref, sh_ref, wr_ref, br_ref,
                    x1_ref, h2_ref, idx_ref, gate_ref):
    shape3 = x_ref.shape
    y = _dot(m_ref[...], wo_ref[...])
    x1 = x_ref[...] + g1_ref[...] * y.reshape(shape3)
    x1_ref[...] = x1
    h2 = _rms(x1) * gn_ref[...]
    h2 = (h2 * (1.0 + sc_ref[...]) + sh_ref[...]).reshape(y.shape)
    h2_ref[...] = h2
    logits = _dot(h2.astype(BF16), wr_ref[...]) + br_ref[...]
    lane = lax.broadcasted_iota(jnp.int32, logits.shape, 1).astype(F32)
    vals, idxs = [], []
    for _ in range(TOP_K):
        mx = jnp.max(logits, axis=-1, keepdims=True)
        am = jnp.min(jnp.where(logits == mx, lane, float(LANES)), axis=-1, keepdims=True)
        vals.append(mx)
        idxs.append(am)
        logits = jnp.where(lane == am, -3.0e38, logits)
    es = [jnp.exp(v - vals[0]) for v in vals]
    den = es[0] + es[1] + es[2] + es[3]
    idx_out = jnp.zeros(lane.shape, F32)
    gate_out = jnp.zeros(lane.shape, F32)
    for k in range(TOP_K):
        idx_out = jnp.where(lane == float(k), idxs[k], idx_out)
        gate_out = jnp.where(lane == float(k), es[k] / den, gate_out)
    idx_ref[...] = idx_out.astype(jnp.int32)
    gate_ref[...] = gate_out


def _outproj(merged, w_out_b, x, mod3, g_norm2, wr_pad, br_pad, bb, tt):
    b, t, d = x.shape
    tm = bb * tt
    nt = t // tt
    tok = lambda w: pl.BlockSpec((tm, w), lambda i, j: (i * nt + j, 0))
    modspec = lambda k: pl.BlockSpec((bb, 1, d), lambda i, j: (i, 0, k))
    return pl.pallas_call(
        _outproj_kernel,
        out_shape=[jax.ShapeDtypeStruct((b, t, d), F32), jax.ShapeDtypeStruct((b * t, d), F32),
                   jax.ShapeDtypeStruct((b * t, LANES), jnp.int32), jax.ShapeDtypeStruct((b * t, LANES), F32)],
        grid=(b // bb, nt),
        in_specs=[tok(d),
                  pl.BlockSpec((d, d), lambda i, j: (0, 0)),
                  pl.BlockSpec((bb, tt, d), lambda i, j: (i, j, 0)),
                  modspec(2),
                  pl.BlockSpec((1, 1, d), lambda i, j: (0, 0, 0)),
                  modspec(4), modspec(3),
                  pl.BlockSpec((d, LANES), lambda i, j: (0, 0)),
                  pl.BlockSpec((1, LANES), lambda i, j: (0, 0))],
        out_specs=[pl.BlockSpec((bb, tt, d), lambda i, j: (i, j, 0)), tok(d), tok(LANES), tok(LANES)],
        compiler_params=_params(("arbitrary", "arbitrary")),
    )(merged, w_out_b, x, mod3, g_norm2.reshape(1, 1, d), mod3, mod3, wr_pad, br_pad)


def _row_copy(src_ref, dst_ref, src_row, dst_row, sem):
    return pltpu.make_async_copy(src_ref.at[pl.ds(src_row, 1)], dst_ref.at[pl.ds(dst_row, 1)], sem)


def _dispatch_kernel(tok_ref, h_ref, o_ref, buf, sem):
    n = buf.shape[0]
    base = pl.program_id(0) * n

    def start(r, carry):
        _row_copy(h_ref, buf, tok_ref[base + r], r, sem).start()
        return carry

    def wait(r, carry):
        _row_copy(h_ref, buf, 0, r, sem).wait()
        return carry

    lax.fori_loop(0, n, start, 0)
    lax.fori_loop(0, n, wait, 0)
    o_ref[...] = buf[...].astype(o_ref.dtype)


def _dispatch(row_tok, h2):
    n_rows = row_tok.shape[0]
    d = h2.shape[1]
    return pl.pallas_call(
        _dispatch_kernel,
        out_shape=jax.ShapeDtypeStruct((n_rows, d), BF16),
        grid_spec=pltpu.PrefetchScalarGridSpec(
            num_scalar_prefetch=1,
            grid=(n_rows // GATHER_ROWS,),
            in_specs=[pl.BlockSpec(memory_space=pl.ANY)],
            out_specs=pl.BlockSpec((GATHER_ROWS, d), lambda i, tok: (i, 0)),
            scratch_shapes=[pltpu.VMEM((GATHER_ROWS, d), F32), pltpu.SemaphoreType.DMA(())],
        ),
        compiler_params=_params(("arbitrary",)),
    )(row_tok, h2)


def _moe_up_kernel(be_ref, x_ref, wg_ref, wl_ref, bg_ref, bl_ref, o_ref, wg_s, wl_s):
    i = pl.program_id(1)
    e = be_ref[i]
    e_prev = be_ref[jnp.maximum(i - 1, 0)]

    @pl.when((i == 0) | (e != e_prev))
    def _():
        wg_s[...] = wg_ref[...].astype(BF16)
        wl_s[...] = wl_ref[...].astype(BF16)

    x = x_ref[...]
    glu = jnp.minimum(_dot(x, wg_s[...]) + bg_ref[...], SWIGLU_LIMIT)
    lin = jnp.clip(_dot(x, wl_s[...]) + bl_ref[...], -SWIGLU_LIMIT, SWIGLU_LIMIT)
    o_ref[...] = (glu * _sigmoid(SWIGLU_ALPHA * glu) * (lin + 1.0)).astype(o_ref.dtype)


def _moe_down_kernel(be_ref, a_ref, w_ref, b_ref, o_ref, w_s):
    i = pl.program_id(0)
    e = be_ref[i]
    e_prev = be_ref[jnp.maximum(i - 1, 0)]

    @pl.when((i == 0) | (e != e_prev))
    def _():
        w_s[...] = w_ref[...].astype(BF16)

    o_ref[...] = _dot(a_ref[...], w_s[...]) + b_ref[...]


def _moe_experts(block_e, x_sorted, w_gate_up, b_gate_up, w_down, b_down):
    n_rows, d = x_sorted.shape
    n_blocks = n_rows // MOE_ROWS
    tf = 1024
    nf = D_FF // tf
    bgu3 = b_gate_up.reshape(N_EXPERTS, 1, 2 * D_FF)
    act = pl.pallas_call(
        _moe_up_kernel,
        out_shape=jax.ShapeDtypeStruct((n_rows, D_FF), BF16),
        grid_spec=pltpu.PrefetchScalarGridSpec(
            num_scalar_prefetch=1,
            grid=(nf, n_blocks),
            in_specs=[pl.BlockSpec((MOE_ROWS, d), lambda j, i, be: (i, 0)),
                      pl.BlockSpec((None, d, tf), lambda j, i, be: (be[i], 0, j)),
                      pl.BlockSpec((None, d, tf), lambda j, i, be: (be[i], 0, nf + j)),
                      pl.BlockSpec((None, 1, tf), lambda j, i, be: (be[i], 0, j)),
                      pl.BlockSpec((None, 1, tf), lambda j, i, be: (be[i], 0, nf + j))],
            out_specs=pl.BlockSpec((MOE_ROWS, tf), lambda j, i, be: (i, j)),
            scratch_shapes=[pltpu.VMEM((d, tf), BF16), pltpu.VMEM((d, tf), BF16)],
        ),
        compiler_params=_params(("arbitrary", "arbitrary")),
    )(block_e, x_sorted, w_gate_up, w_gate_up, bgu3, bgu3)
    return pl.pallas_call(
        _moe_down_kernel,
        out_shape=jax.ShapeDtypeStruct((n_rows, d), F32),
        grid_spec=pltpu.PrefetchScalarGridSpec(
            num_scalar_prefetch=1,
            grid=(n_blocks,),
            in_specs=[pl.BlockSpec((MOE_ROWS, D_FF), lambda i, be: (i, 0)),
                      pl.BlockSpec((None, D_FF, d), lambda i, be: (be[i], 0, 0)),
                      pl.BlockSpec((None, 1, d), lambda i, be: (be[i], 0, 0))],
            out_specs=pl.BlockSpec((MOE_ROWS, d), lambda i, be: (i, 0)),
            scratch_shapes=[pltpu.VMEM((D_FF, d), BF16)],
        ),
        compiler_params=_params(("arbitrary",)),
    )(block_e, act, w_down, b_down.reshape(N_EXPERTS, 1, d))


def _combine_kernel(dest_ref, x1_ref, g2_ref, gate_ref, rows_ref, o_ref, buf, sem):
    tm = buf.shape[1]
    nt = pl.num_programs(1)
    base = (pl.program_id(0) * nt + pl.program_id(1)) * (tm * TOP_K)

    def start(t, carry):
        for k in range(TOP_K):
            _row_copy(rows_ref, buf.at[k], dest_ref[base + t * TOP_K + k], t, sem).start()
        return carry

    def wait(t, carry):
        for k in range(TOP_K):
            _row_copy(rows_ref, buf.at[k], 0, t, sem).wait()
        return carry

    lax.fori_loop(0, tm, start, 0)
    lax.fori_loop(0, tm, wait, 0)
    gates = gate_ref[...]
    ff = buf[0] * gates[:, 0:1]
    for k in range(1, TOP_K):
        ff = ff + buf[k] * gates[:, k:k + 1]
    o_ref[...] = x1_ref[...] + g2_ref[...] * ff.reshape(x1_ref.shape)


def _combine(dest, x1, mod3, gates, out_rows, bb, tt):
    b, t, d = x1.shape
    nt = t // tt
    tm = bb * tt
    return pl.pallas_call(
        _combine_kernel,
        out_shape=jax.ShapeDtypeStruct((b, t, d), F32),
        grid_spec=pltpu.PrefetchScalarGridSpec(
            num_scalar_prefetch=1,
            grid=(b // bb, nt),
            in_specs=[pl.BlockSpec((bb, tt, d), lambda i, j, ds: (i, j, 0)),
                      pl.BlockSpec((bb, 1, d), lambda i, j, ds: (i, 0, 5)),
                      pl.BlockSpec((tm, LANES), lambda i, j, ds: (i * nt + j, 0)),
                      pl.BlockSpec(memory_space=pl.ANY)],
            out_specs=pl.BlockSpec((bb, tt, d), lambda i, j, ds: (i, j, 0)),
            scratch_shapes=[pltpu.VMEM((TOP_K, tm, d), F32), pltpu.SemaphoreType.DMA(())],
        ),
        compiler_params=_params(("arbitrary", "arbitrary")),
    )(dest, x1, mod3, gates, out_rows)


def _route(top_idx):
    t = top_idx.shape[0]
    n_assign = t * TOP_K
    flat_e = top_idx[:, :TOP_K].reshape(-1)
    flat_tok = jnp.arange(n_assign, dtype=jnp.int32) // TOP_K
    order = jnp.argsort(flat_e)
    sorted_e = flat_e[order]
    counts = jnp.bincount(flat_e, length=N_EXPERTS).astype(jnp.int32)
    padded = (counts + MOE_ROWS - 1) // MOE_ROWS * MOE_ROWS
    pad_end = jnp.cumsum(padded)
    pad_start = pad_end - padded
    start = jnp.cumsum(counts) - counts
    dest = pad_start[sorted_e] + jnp.arange(n_assign, dtype=jnp.int32) - start[sorted_e]
    n_blocks = -(-(n_assign + N_EXPERTS * (MOE_ROWS - 1)) // MOE_ROWS)
    n_blocks = -(-n_blocks * MOE_ROWS // GATHER_ROWS) * GATHER_ROWS // MOE_ROWS
    row_tok = jnp.zeros((n_blocks * MOE_ROWS,), jnp.int32).at[dest].set(flat_tok[order])
    block_e = jnp.minimum(jnp.searchsorted(pad_end, jnp.arange(n_blocks, dtype=jnp.int32) * MOE_ROWS, side='right'),
                          N_EXPERTS - 1).astype(jnp.int32)
    dest_by_assign = jnp.zeros((n_assign,), jnp.int32).at[order].set(dest)
    return row_tok, block_e, dest_by_assign


def _mixers(x, mod, pos, p, *, prompt, caches=None, page_table=None):
    b, t, d = x.shape
    tokens = b * t
    mod3 = mod.reshape(b, 1, 6 * d)
    bb, tt = (1, 512) if prompt else (32, t)
    h = _normmod(x, p['g_norm1'], mod3, 1, 0, bb, tt)
    z = _mm(h, p['w_in'], min(tokens, 1024), Z_COLS // 5)
    lc = min(t, CHUNK)
    reps = CHUNK // lc
    ws_t = jnp.tile(p['w_s'][:, :lc, :lc], (1, reps, reps))
    bs_t = jnp.broadcast_to(jnp.tile(p['b_s'][:, :lc], (1, reps))[:, :, None], (A_GROUPS, CHUNK, CHUNK))
    gm = _gmlp(z, p['g_v_ln'], p['b_v_ln'], ws_t, bs_t, lc, want_vn=not prompt)
    a_out = gm[0]

    half = ROPE_DIM // 2
    inv_freq = jnp.exp(-math.log(ROPE_THETA) * jnp.arange(half, dtype=F32) / half)
    ang = pos.astype(F32)[:, None] * inv_freq[None, :]
    cos, sin = jnp.cos(ang), jnp.sin(ang)
    tab = jnp.concatenate([cos, cos, -sin, sin], axis=-1)
    tab = jnp.broadcast_to(tab[None], (b, t, LANES)).reshape(tokens, LANES)
    swap = lambda g: jnp.concatenate([g[half:], g[:half]])
    gains = [p['g_q_lat'].reshape(1, -1), p['g_kv_lat'].reshape(1, -1), p['g_q_nope'].reshape(1, -1),
             jnp.concatenate([p['g_q_rope'], swap(p['g_q_rope'])]).reshape(1, -1),
             jnp.concatenate([p['g_k_rope'], swap(p['g_k_rope'])]).reshape(1, -1),
             p['g_k_nope'].reshape(1, -1)]
    weights = [p['wq_n'], p['wq_r'], p['w_uk'], p['w_uv']]
    mla = _mla(z, tab, gains, weights, 256, prompt)
    c_kv, k_pe, rinv = mla[:3]
    if prompt:
        b_out = _attend_prompt(mla[3], mla[4], mla[5], 512)
    else:
        cache_latent, cache_rope_t, cache_rinv_t = caches
        rows = N_HEADS * t
        qa = _absorb(mla[3], p['w_uk']).reshape(b, rows, KV_LORA)
        qr = mla[4].reshape(b, rows, LANES)
        pad = PAGE_SIZE - t
        nlat = jnp.pad(c_kv.reshape(b, t, KV_LORA), ((0, 0), (0, pad), (0, 0)))
        nrope_t = jnp.pad(jnp.swapaxes(k_pe.reshape(b, t, ROPE_DIM), 1, 2), ((0, 0), (0, 0), (0, pad)))
        nrinv_t = jnp.pad(jnp.swapaxes(rinv.reshape(b, t, N_HEADS), 1, 2), ((0, 0), (0, 0), (0, pad)))
        o_lat = _decode(page_table, qa, qr, nlat, nrope_t, nrinv_t, cache_latent, cache_rope_t, cache_rinv_t, t)
        b_out = _upv(o_lat.reshape(tokens, N_HEADS * KV_LORA), p['w_uv'])
    merged = _merge(a_out, b_out, p['w_proj_a'], p['w_proj_b'], z, min(tokens, 512), 1024)
    bb2, tt2 = (1, 256) if prompt else (32, t)
    x1, h2, top_idx, gates = _outproj(merged, p['w_out'], x, mod3, p['g_norm2'], p['wr_pad'], p['br_pad'], bb2, tt2)
    vn = None if prompt else gm[1]
    return (x1, h2, top_idx, gates, mod3), (c_kv, k_pe, rinv, vn)


def kernel(x_prompt, x_sample, c_prompt, c_sample, cache_latent, cache_k_rope, cache_k_rinv, page_table,
           w_ada, b_ada, g_norm1, w_in, g_v_ln, b_v_ln, w_s, b_s, g_q_lat, w_uq, g_q_nope, g_q_rope,
           g_kv_lat, g_k_rope, w_uk, w_uv, g_k_nope, w_proj_a, w_proj_b, w_out, g_norm2,
           w_router, b_router, w_gate_up, b_gate_up, w_down, b_down):
    depth = w_ada.shape[0]
    bp, sp, d = x_prompt.shape
    bs, ss, _ = x_sample.shape
    past_len = page_table.shape[1] * PAGE_SIZE
    pos_p = jnp.arange(sp, dtype=jnp.int32)
    pos_s = past_len + jnp.arange(ss, dtype=jnp.int32)
    half = ROPE_DIM // 2
    y_p, y_s = x_prompt, x_sample
    outs = [[] for _ in range(7)]
    for l in range(depth):
        wi = w_in[l]
        kr = wi[:, 2816:2880]
        w_in_perm = jnp.concatenate(
            [wi[:, 0:2048], wi[:, 2880:6976], wi[:, 2048:2816], kr, kr[:, half:], kr[:, :half]], axis=1).astype(BF16)
        wq = w_uq[l]
        wq_rope = wq[:, :, NOPE_DIM:]
        p = {
            'g_norm1': g_norm1[l], 'w_in': w_in_perm, 'g_v_ln': g_v_ln[l], 'b_v_ln': b_v_ln[l],
            'w_s': w_s[l], 'b_s': b_s[l], 'g_q_lat': g_q_lat[l], 'g_q_nope': g_q_nope[l], 'g_q_rope': g_q_rope[l],
            'g_kv_lat': g_kv_lat[l], 'g_k_rope': g_k_rope[l], 'g_k_nope': g_k_nope[l],
            'wq_n': wq[:, :, :NOPE_DIM].reshape(Q_LORA, -1).astype(BF16),
            'wq_r': jnp.concatenate([wq_rope, wq_rope[:, :, half:], wq_rope[:, :, :half]],
                                    axis=-1).reshape(Q_LORA, -1).astype(BF16),
            'w_uk': w_uk[l].reshape(KV_LORA, -1).astype(BF16), 'w_uv': w_uv[l].reshape(KV_LORA, -1).astype(BF16),
            'w_proj_a': w_proj_a[l].astype(BF16), 'w_proj_b': w_proj_b[l].astype(BF16),
            'w_out': w_out[l].astype(BF16), 'g_norm2': g_norm2[l],
            'wr_pad': jnp.pad(w_router[l], ((0, 0), (0, LANES - N_EXPERTS))).astype(BF16),
            'br_pad': jnp.pad(b_router[l], (0, LANES - N_EXPERTS), constant_values=NEG_INF).reshape(1, LANES),
        }
        n_c = bp + bs
        c_all = jnp.pad(jnp.concatenate([c_prompt, c_sample], axis=0), ((0, -n_c % 8), (0, 0)))
        mod = _ada(c_all, w_ada[l], b_ada[l])
        caches = (cache_latent[l], jnp.swapaxes(cache_k_rope[l], 1, 2), jnp.swapaxes(cache_k_rinv[l], 1, 2))
        (x1_p, h2_p, idx_p, gate_p, mod3_p), aux_p = _mixers(y_p, mod[:bp], pos_p, p, prompt=True)
        (x1_s, h2_s, idx_s, gate_s, mod3_s), aux_s = _mixers(y_s, mod[bp:n_c], pos_s, p, prompt=False,
                                                            caches=caches, page_table=page_table)
        h2 = jnp.concatenate([h2_p, h2_s], axis=0)
        row_tok, block_e, dest = _route(jnp.concatenate([idx_p, idx_s], axis=0))
        x_sorted = _dispatch(row_tok, h2)
        out_rows = _moe_experts(block_e, x_sorted, w_gate_up[l], b_gate_up[l], w_down[l], b_down[l])
        n_p = bp * sp * TOP_K
        y_p = _combine(dest[:n_p], x1_p, mod3_p, gate_p, out_rows, 1, COMBINE_TOKENS)
        y_s = _combine(dest[n_p:], x1_s, mod3_s, gate_s, out_rows, COMBINE_TOKENS // ss, ss)
        for o, a in zip(outs, (aux_p[0].reshape(bp, sp, KV_LORA), aux_p[1].reshape(bp, sp, ROPE_DIM),
                               aux_p[2].reshape(bp, sp, N_HEADS), aux_s[0].reshape(bs, ss, KV_LORA),
                               aux_s[1].reshape(bs, ss, ROPE_DIM), aux_s[2].reshape(bs, ss, N_HEADS),
                               aux_s[3].reshape(bs, ss, D_A))):
            o.append(a)
    return (y_p, y_s) + tuple(jnp.stack(o) for o in outs)
```

```python
import functools
import math

import jax
import jax.numpy as jnp
from jax import lax
from jax.experimental import pallas as pl
from jax.experimental.pallas import tpu as pltpu

F32 = jnp.float32
BF16 = jnp.bfloat16

D_MODEL = 2048
D_A = D_MODEL // 2
A_GROUP = 128
A_GROUPS = D_A // A_GROUP
CHUNK = 128
N_HEADS = 16
Q_LORA = D_MODEL // 4
KV_LORA = D_MODEL // 8
NOPE_DIM = 128
ROPE_DIM = 64
V_DIM = 128
QK_DIM = NOPE_DIM + ROPE_DIM
ROPE_THETA = 10000.0
SCALE = 1.0 / math.sqrt(QK_DIM)
N_EXPERTS = 32
TOP_K = 4
D_FF = D_MODEL
SWIGLU_LIMIT = 7.0
SWIGLU_ALPHA = 1.702
EPS = 1e-6
NEG_INF = -1e30
PAGE_SIZE = 128

LANES = 128
MOE_ROWS = 128
GATHER_ROWS = 512
COMBINE_TOKENS = 128
PAGES_PER_STEP = 32
DECODE_CHAINS = 4
VMEM_LIMIT = 56 * 1024 * 1024

ZO_U = 0
ZO_V = ZO_U + D_A
ZO_GA = ZO_V + D_A
ZO_GB = ZO_GA + D_MODEL
ZO_Q = ZO_GB + D_MODEL
ZO_KV = ZO_Q + Q_LORA
ZO_KR = ZO_KV + KV_LORA
Z_COLS = ZO_KR + 2 * ROPE_DIM


def _params(sem):
    return pltpu.CompilerParams(dimension_semantics=sem, vmem_limit_bytes=VMEM_LIMIT)


def _dot(a, b):
    return jnp.dot(a, b, preferred_element_type=F32)


def _dot_nt(a, b):
    return lax.dot_general(a, b, (((1,), (1,)), ((), ())), preferred_element_type=F32)


def _sigmoid(x):
    return 1.0 / (1.0 + jnp.exp(-x))


def _gelu(x):
    c = math.sqrt(2.0 / math.pi)
    return 0.5 * x * (1.0 + jnp.tanh(c * (x + 0.044715 * (x * x * x))))


def _rms(x):
    return x * lax.rsqrt(jnp.mean(x * x, axis=-1, keepdims=True) + EPS)


def _ada_kernel(c_ref, w_ref, b_ref, o_ref):
    c = c_ref[...]
    a = (c * _sigmoid(c)).astype(BF16)
    o_ref[...] = _dot(a, w_ref[...].astype(BF16)) + b_ref[...]


def _ada(c_all, w_ada, b_ada):
    rows, d = c_all.shape
    n = w_ada.shape[1]
    tn = 1024
    return pl.pallas_call(
        _ada_kernel,
        out_shape=jax.ShapeDtypeStruct((rows, n), F32),
        grid=(n // tn,),
        in_specs=[pl.BlockSpec((rows, d), lambda j: (0, 0)),
                  pl.BlockSpec((d, tn), lambda j: (0, j)),
                  pl.BlockSpec((1, tn), lambda j: (0, j))],
        out_specs=pl.BlockSpec((rows, tn), lambda j: (0, j)),
        compiler_params=_params(("arbitrary",)),
    )(c_all, w_ada, b_ada.reshape(1, n))


def _normmod_kernel(x_ref, g_ref, sc_ref, sh_ref, o_ref):
    x = x_ref[...]
    y = _rms(x) * g_ref[...]
    y = y * (1.0 + sc_ref[...]) + sh_ref[...]
    o_ref[...] = y.reshape(o_ref.shape).astype(o_ref.dtype)


def _normmod(x, g, mod3, sc_idx, sh_idx, bb, tt):
    b, t, d = x.shape
    return pl.pallas_call(
        _normmod_kernel,
        out_shape=jax.ShapeDtypeStruct((b * t, d), BF16),
        grid=(b // bb, t // tt),
        in_specs=[pl.BlockSpec((bb, tt, d), lambda i, j: (i, j, 0)),
                  pl.BlockSpec((1, 1, d), lambda i, j: (0, 0, 0)),
                  pl.BlockSpec((bb, 1, d), lambda i, j: (i, 0, sc_idx)),
                  pl.BlockSpec((bb, 1, d), lambda i, j: (i, 0, sh_idx))],
        out_specs=pl.BlockSpec((bb * tt, d), lambda i, j: (i * (t // tt) + j, 0)),
        compiler_params=_params(("arbitrary", "arbitrary")),
    )(x, g.reshape(1, 1, d), mod3, mod3)


def _mm_kernel(x_ref, w_ref, o_ref):
    o_ref[...] = _dot(x_ref[...], w_ref[...]).astype(o_ref.dtype)


def _mm(x, w, tm, tn, out_dtype=F32):
    m, k = x.shape
    n = w.shape[1]
    return pl.pallas_call(
        _mm_kernel,
        out_shape=jax.ShapeDtypeStruct((m, n), out_dtype),
        grid=(m // tm, n // tn),
        in_specs=[pl.BlockSpec((tm, k), lambda i, j: (i, 0)),
                  pl.BlockSpec((k, tn), lambda i, j: (0, j))],
        out_specs=pl.BlockSpec((tm, tn), lambda i, j: (i, j)),
        compiler_params=_params(("arbitrary", "arbitrary")),
    )(x, w)


def _gmlp_kernel(zu_ref, zv_ref, gln_ref, bln_ref, ws_ref, bs_ref, a_ref, *maybe_vn_ref, lc):
    u = _gelu(zu_ref[...])
    gv = _gelu(zv_ref[...])
    xc = gv - jnp.mean(gv, axis=-1, keepdims=True)
    vn = xc * lax.rsqrt(jnp.mean(xc * xc, axis=-1, keepdims=True) + EPS)
    vn = vn * gln_ref[...] + bln_ref[...]
    if maybe_vn_ref:
        maybe_vn_ref[0][...] = vn
    row = lax.broadcasted_iota(jnp.int32, (CHUNK, CHUNK), 0)
    col = lax.broadcasted_iota(jnp.int32, (CHUNK, CHUNK), 1)
    mask = (col <= row) & ((row // lc) == (col // lc))
    vb = vn.astype(BF16)
    for g in range(A_GROUPS):
        w = jnp.where(mask, ws_ref[g], 0.0).astype(BF16)
        sl = slice(g * A_GROUP, (g + 1) * A_GROUP)
        s = _dot(w, vb[:, sl]) + bs_ref[g]
        a_ref[:, sl] = (u[:, sl] * s).astype(a_ref.dtype)


def _gmlp(z, g_ln, b_ln, ws_t, bs_t, lc, want_vn):
    t = z.shape[0]
    out_shape = [jax.ShapeDtypeStruct((t, D_A), BF16)]
    out_specs = [pl.BlockSpec((CHUNK, D_A), lambda i: (i, 0))]
    if want_vn:
        out_shape.append(jax.ShapeDtypeStruct((t, D_A), F32))
        out_specs.append(pl.BlockSpec((CHUNK, D_A), lambda i: (i, 0)))
    return pl.pallas_call(
        functools.partial(_gmlp_kernel, lc=lc),
        out_shape=out_shape,
        grid=(t // CHUNK,),
        in_specs=[pl.BlockSpec((CHUNK, D_A), lambda i: (i, ZO_U // D_A)),
                  pl.BlockSpec((CHUNK, D_A), lambda i: (i, ZO_V // D_A)),
                  pl.BlockSpec((1, D_A), lambda i: (0, 0)),
                  pl.BlockSpec((1, D_A), lambda i: (0, 0)),
                  pl.BlockSpec((A_GROUPS, CHUNK, CHUNK), lambda i: (0, 0, 0)),
                  pl.BlockSpec((A_GROUPS, CHUNK, CHUNK), lambda i: (0, 0, 0))],
        out_specs=out_specs,
        compiler_params=_params(("arbitrary",)),
    )(z, z, g_ln.reshape(1, D_A), b_ln.reshape(1, D_A), ws_t, bs_t)


def _rope_pair(y2, gain2, tab):
    p = y2 * gain2 * tab
    return p + pltpu.roll(p, ROPE_DIM, 1)


def _mla_kernel(zq_ref, zkv_ref, zkr_ref, tab_ref, gql_ref, gkv_ref, gqn_ref, gq2_ref, gk2_ref, gkn_ref,
                wqn_ref, wqr_ref, wuk_ref, wuv_ref, ckv_ref, kpe_ref, rinv_ref, *qkv_refs, prompt):
    tab = tab_ref[...]
    q_lat = (_rms(zq_ref[...]) * gql_ref[...]).astype(BF16)
    c_kv = _rms(zkv_ref[...]) * gkv_ref[...]
    ckv_ref[...] = c_kv
    kr2 = zkr_ref[...]
    kr2 = kr2 * lax.rsqrt(jnp.sum(kr2 * kr2, axis=-1, keepdims=True) / (2 * ROPE_DIM) + EPS)
    kpe2 = _rope_pair(kr2, gk2_ref[...], tab)
    kpe_ref[...] = kpe2[:, :ROPE_DIM]
    cb = c_kv.astype(BF16)
    tm = cb.shape[0]
    lane = lax.broadcasted_iota(jnp.int32, (tm, N_HEADS), 1)
    rinv_all = jnp.zeros((tm, N_HEADS), F32)
    if prompt:
        col = lax.broadcasted_iota(jnp.int32, kpe2.shape, 1)
        kp_b = jnp.where(col < ROPE_DIM, kpe2, 0.0).astype(BF16)
    for h in range(N_HEADS):
        sl = slice(h * NOPE_DIM, (h + 1) * NOPE_DIM)
        qn = _rms(_dot(q_lat, wqn_ref[:, sl])) * gqn_ref[...]
        qr2 = _dot(q_lat, wqr_ref[:, sl])
        qr2 = qr2 * lax.rsqrt(jnp.sum(qr2 * qr2, axis=-1, keepdims=True) / (2 * ROPE_DIM) + EPS)
        qp = (_rope_pair(qr2, gq2_ref[...], tab) * SCALE).astype(BF16)
        k_raw = _dot(cb, wuk_ref[:, sl])
        rinv = lax.rsqrt(jnp.mean(k_raw * k_raw, axis=-1, keepdims=True) + EPS)
        rinv_all = jnp.where(lane == h, rinv, rinv_all)
        if prompt:
            q_ref, k_ref, v_ref = qkv_refs
            lo = slice(2 * h * NOPE_DIM, (2 * h + 1) * NOPE_DIM)
            hi = slice((2 * h + 1) * NOPE_DIM, (2 * h + 2) * NOPE_DIM)
            q_ref[:, lo] = (qn * SCALE).astype(BF16)
            q_ref[:, hi] = qp
            k_ref[:, lo] = (k_raw * rinv * gkn_ref[...]).astype(BF16)
            k_ref[:, hi] = kp_b
            v_ref[:, sl] = _dot(cb, wuv_ref[:, sl]).astype(BF16)
        else:
            qn_ref, qp_ref = qkv_refs
            qn_ref[:, sl] = (qn * gkn_ref[...]).astype(BF16)
            qp_ref[:, sl] = qp
    rinv_ref[...] = rinv_all


def _mla(z, tab, gains, weights, tm, prompt):
    t = z.shape[0]
    hd = N_HEADS * NOPE_DIM
    row = lambda w: pl.BlockSpec((tm, w), lambda i: (i, 0))
    full = lambda a: pl.BlockSpec(a.shape, lambda i: (0,) * a.ndim)
    out_shape = [jax.ShapeDtypeStruct((t, KV_LORA), F32), jax.ShapeDtypeStruct((t, ROPE_DIM), F32),
                 jax.ShapeDtypeStruct((t, N_HEADS), F32)]
    out_specs = [row(KV_LORA), row(ROPE_DIM), row(N_HEADS)]
    widths = (2 * hd, 2 * hd, hd) if prompt else (hd, hd)
    out_shape += [jax.ShapeDtypeStruct((t, w), BF16) for w in widths]
    out_specs += [row(w) for w in widths]
    return pl.pallas_call(
        functools.partial(_mla_kernel, prompt=prompt),
        out_shape=out_shape,
        grid=(t // tm,),
        in_specs=[pl.BlockSpec((tm, Q_LORA), lambda i: (i, ZO_Q // Q_LORA)),
                  pl.BlockSpec((tm, KV_LORA), lambda i: (i, ZO_KV // KV_LORA)),
                  pl.BlockSpec((tm, LANES), lambda i: (i, ZO_KR // LANES)),
                  row(LANES)] + [full(a) for a in gains] + [full(a) for a in weights],
        out_specs=out_specs,
        compiler_params=_params(("arbitrary",)),
    )(z, z, z, tab, *gains, *weights)


def _fa_kernel(q_ref, k_ref, v_ref, o_ref, *, tq):
    i = pl.program_id(1)
    q = q_ref[...]

    def step(j, carry, masked):
        m, l, acc = carry
        ks = pl.ds(pl.multiple_of(j * tq, tq), tq)
        s = _dot_nt(q, k_ref[ks, :])
        if masked:
            row = lax.broadcasted_iota(jnp.int32, (tq, tq), 0)
            col = lax.broadcasted_iota(jnp.int32, (tq, tq), 1)
            s = jnp.where(col <= row, s, NEG_INF)
        m_new = jnp.maximum(m, jnp.max(s, axis=-1, keepdims=True))
        corr = jnp.exp(m - m_new)
        p = jnp.exp(s - m_new)
        l = l * corr + jnp.sum(p, axis=-1, keepdims=True)
        acc = acc * corr + _dot(p.astype(BF16), v_ref[ks, :])
        return m_new, l, acc

    init = (jnp.full((tq, 1), NEG_INF, F32), jnp.zeros((tq, 1), F32), jnp.zeros((tq, V_DIM), F32))
    carry = lax.fori_loop(0, i, functools.partial(step, masked=False), init)
    m, l, acc = step(i, carry, True)
    o_ref[...] = (acc / l).astype(o_ref.dtype)


def _attend_prompt(q, k, v, tq):
    t = q.shape[0]
    return pl.pallas_call(
        functools.partial(_fa_kernel, tq=tq),
        out_shape=jax.ShapeDtypeStruct((t, N_HEADS * V_DIM), BF16),
        grid=(N_HEADS, t // tq),
        in_specs=[pl.BlockSpec((tq, 2 * NOPE_DIM), lambda h, i: (i, h)),
                  pl.BlockSpec((t, 2 * NOPE_DIM), lambda h, i: (0, h)),
                  pl.BlockSpec((t, V_DIM), lambda h, i: (0, h))],
        out_specs=pl.BlockSpec((tq, V_DIM), lambda h, i: (i, h)),
        compiler_params=_params(("arbitrary", "arbitrary")),
    )(q, k, v)


def _absorb_kernel(qn_ref, wuk_ref, qa_ref):
    qa_ref[...] = (_dot_nt(qn_ref[...], wuk_ref[...]) * SCALE).astype(qa_ref.dtype)


def _absorb(qn, w_uk_b):
    tokens = qn.shape[0]
    return pl.pallas_call(
        _absorb_kernel,
        out_shape=jax.ShapeDtypeStruct((tokens, N_HEADS * KV_LORA), BF16),
        grid=(N_HEADS,),
        in_specs=[pl.BlockSpec((tokens, NOPE_DIM), lambda h: (0, h)),
                  pl.BlockSpec((KV_LORA, NOPE_DIM), lambda h: (0, h))],
        out_specs=pl.BlockSpec((tokens, KV_LORA), lambda h: (0, h)),
        compiler_params=_params(("arbitrary",)),
    )(qn, w_uk_b)


def _decode_kernel(pt_ref, qa_ref, qr_ref, nlat_ref, nrope_ref, nrinv_ref, lat_hbm, rope_hbm, rinv_hbm, o_ref,
                   lat_buf, rope_buf, rinv_buf, sems, ck_s, kr_s, ri_s, m_ref, l_ref, acc_ref, *, t_new):
    pc = PAGES_PER_STEP
    b, c = pl.program_id(0), pl.program_id(1)
    nb, nc = pl.num_programs(0), pl.num_programs(1)
    n = b * nc + c
    slot = n % 2
    rows = N_HEADS * t_new

    def page_copies(bi, ci, sl):
        copies = []
        for j in range(pc):
            page = pt_ref[bi, ci * pc + j]
            copies.append(pltpu.make_async_copy(lat_hbm.at[page], lat_buf.at[sl, j], sems.at[sl, 0]))
            copies.append(pltpu.make_async_copy(rope_hbm.at[page], rope_buf.at[sl, j], sems.at[sl, 1]))
            copies.append(pltpu.make_async_copy(rinv_hbm.at[page], rinv_buf.at[sl, j], sems.at[sl, 2]))
        return copies

    @pl.when(n == 0)
    def _():
        for cp in page_copies(b, c, slot):
            cp.start()

    @pl.when(n + 1 < nb * nc)
    def _():
        wrap = c + 1 == nc
        for cp in page_copies(jnp.where(wrap, b + 1, b), jnp.where(wrap, 0, c + 1), 1 - slot):
            cp.start()

    for cp in page_copies(b, c, slot):
        cp.wait()
    qa = qa_ref[...]
    qr = qr_ref[...][:, :ROPE_DIM]

    def scores(ck, krt, rit):
        n = ck.shape[0]
        return (_dot_nt(qa, ck).reshape(t_new, N_HEADS, n) * rit[None]).reshape(rows, n) + _dot(qr, krt)

    def fold(g, s, ck):
        m_old = m_ref[g]
        m_new = jnp.maximum(m_old, jnp.max(s, axis=-1, keepdims=True))
        corr = jnp.exp(m_old - m_new)
        p = jnp.exp(s - m_new)
        l_ref[g] = l_ref[g] * corr + jnp.sum(p, axis=-1, keepdims=True)
        m_ref[g] = m_new
        acc_ref[g] = acc_ref[g] * corr + _dot(p.astype(BF16), ck)

    @pl.when(c == 0)
    def _():
        m_ref[...] = jnp.full(m_ref.shape, NEG_INF, F32)
        l_ref[...] = jnp.zeros(l_ref.shape, F32)
        acc_ref[...] = jnp.zeros(acc_ref.shape, F32)
        ck = nlat_ref[...].astype(BF16)
        s = scores(ck, nrope_ref[...].astype(BF16), nrinv_ref[...])
        row = lax.broadcasted_iota(jnp.int32, s.shape, 0)
        col = lax.broadcasted_iota(jnp.int32, s.shape, 1)
        fold(0, jnp.where(col <= row // N_HEADS, s, NEG_INF), ck)

    for j in range(pc):
        ck_s[j * PAGE_SIZE:(j + 1) * PAGE_SIZE, :] = lat_buf[slot, j].astype(BF16)
        kr_s[:, j * PAGE_SIZE:(j + 1) * PAGE_SIZE] = rope_buf[slot, j].astype(BF16)
        ri_s[:, j * PAGE_SIZE:(j + 1) * PAGE_SIZE] = rinv_buf[slot, j]
    keys = pc * PAGE_SIZE // DECODE_CHAINS
    cks = [ck_s[g * keys:(g + 1) * keys, :] for g in range(DECODE_CHAINS)]
    ss = [scores(cks[g], kr_s[:, g * keys:(g + 1) * keys], ri_s[:, g * keys:(g + 1) * keys])
          for g in range(DECODE_CHAINS)]
    for g in range(DECODE_CHAINS):
        fold(g, ss[g], cks[g])

    @pl.when(c == nc - 1)
    def _():
        m = m_ref[0]
        for g in range(1, DECODE_CHAINS):
            m = jnp.maximum(m, m_ref[g])
        l = jnp.zeros(m.shape, F32)
        acc = jnp.zeros(acc_ref.shape[1:], F32)
        for g in range(DECODE_CHAINS):
            w = jnp.exp(m_ref[g] - m)
            l = l + l_ref[g] * w
            acc = acc + acc_ref[g] * w
        o_ref[...] = (acc / l).astype(o_ref.dtype)


def _decode(page_table, qa, qr, nlat, nrope_t, nrinv_t, cache_latent, cache_rope_t, cache_rinv_t, t_new):
    b, n_pages = page_table.shape
    pc = PAGES_PER_STEP
    rows = N_HEADS * t_new
    seq = lambda r, w: pl.BlockSpec((None, r, w), lambda i, c, pt: (i, 0, 0))
    hbm = pl.BlockSpec(memory_space=pl.ANY)
    keys = pc * PAGE_SIZE
    grid_spec = pltpu.PrefetchScalarGridSpec(
        num_scalar_prefetch=1,
        grid=(b, n_pages // pc),
        in_specs=[seq(rows, KV_LORA), seq(rows, LANES),
                  seq(PAGE_SIZE, KV_LORA), seq(ROPE_DIM, PAGE_SIZE), seq(N_HEADS, PAGE_SIZE), hbm, hbm, hbm],
        out_specs=seq(rows, KV_LORA),
        scratch_shapes=[pltpu.VMEM((2, pc, PAGE_SIZE, KV_LORA), F32), pltpu.VMEM((2, pc, ROPE_DIM, PAGE_SIZE), F32),
                        pltpu.VMEM((2, pc, N_HEADS, PAGE_SIZE), F32), pltpu.SemaphoreType.DMA((2, 3)),
                        pltpu.VMEM((keys, KV_LORA), BF16), pltpu.VMEM((ROPE_DIM, keys), BF16),
                        pltpu.VMEM((N_HEADS, keys), F32),
                        pltpu.VMEM((DECODE_CHAINS, rows, 1), F32), pltpu.VMEM((DECODE_CHAINS, rows, 1), F32),
                        pltpu.VMEM((DECODE_CHAINS, rows, KV_LORA), F32)],
    )
    return pl.pallas_call(
        functools.partial(_decode_kernel, t_new=t_new),
        out_shape=jax.ShapeDtypeStruct((b, rows, KV_LORA), F32),
        grid_spec=grid_spec,
        compiler_params=_params(("arbitrary", "arbitrary")),
    )(page_table, qa, qr, nlat, nrope_t, nrinv_t, cache_latent, cache_rope_t, cache_rinv_t)


def _upv_kernel(o_ref, wuv_ref, out_ref):
    out_ref[...] = _dot(o_ref[...].astype(BF16), wuv_ref[...]).astype(out_ref.dtype)


def _upv(o_lat, w_uv_b):
    tokens = o_lat.shape[0]
    return pl.pallas_call(
        _upv_kernel,
        out_shape=jax.ShapeDtypeStruct((tokens, N_HEADS * V_DIM), BF16),
        grid=(N_HEADS,),
        in_specs=[pl.BlockSpec((tokens, KV_LORA), lambda h: (0, h)),
                  pl.BlockSpec((KV_LORA, V_DIM), lambda h: (0, h))],
        out_specs=pl.BlockSpec((tokens, V_DIM), lambda h: (0, h)),
        compiler_params=_params(("arbitrary",)),
    )(o_lat, w_uv_b)


def _merge_kernel(a_ref, b_ref, wa_ref, wb_ref, ga_ref, gb_ref, o_ref):
    pa = _dot(a_ref[...], wa_ref[...])
    pb = _dot(b_ref[...], wb_ref[...])
    o_ref[...] = (_sigmoid(ga_ref[...]) * pa + _sigmoid(gb_ref[...]) * pb).astype(o_ref.dtype)


def _merge(a_out, b_out, wpa, wpb, z, tm, tn):
    t = a_out.shape[0]
    return pl.pallas_call(
        _merge_kernel,
        out_shape=jax.ShapeDtypeStruct((t, D_MODEL), BF16),
        grid=(t // tm, D_MODEL // tn),
        in_specs=[pl.BlockSpec((tm, D_A), lambda i, j: (i, 0)),
                  pl.BlockSpec((tm, D_MODEL), lambda i, j: (i, 0)),
                  pl.BlockSpec((D_A, tn), lambda i, j: (0, j)),
                  pl.BlockSpec((D_MODEL, tn), lambda i, j: (0, j)),
                  pl.BlockSpec((tm, tn), lambda i, j: (i, ZO_GA // tn + j)),
                  pl.BlockSpec((tm, tn), lambda i, j: (i, ZO_GB // tn + j))],
        out_specs=pl.BlockSpec((tm, tn), lambda i, j: (i, j)),
        compiler_params=_params(("arbitrary", "arbitrary")),
    )(a_out, b_out, wpa, wpb, z, z)


def _outproj_kernel(m_ref, wo_ref, x_ref, g1_ref, gn_ref, sc_ref, sh_ref, wr_ref, br_ref,
                    x1_ref, h2_ref, idx_ref, gate_ref):
    shape3 = x_ref.shape
    y = _dot(m_ref[...], wo_ref[...])
    x1 = x_ref[...] + g1_ref[...] * y.reshape(shape3)
    x1_ref[...] = x1
    h2 = _rms(x1) * gn_ref[...]
    h2 = (h2 * (1.0 + sc_ref[...]) + sh_ref[...]).reshape(y.shape)
    h2_ref[...] = h2
    logits = _dot(h2.astype(BF16), wr_ref[...]) + br_ref[...]
    lane = lax.broadcasted_iota(jnp.int32, logits.shape, 1).astype(F32)
    vals, idxs = [], []
    for _ in range(TOP_K):
        mx = jnp.max(logits, axis=-1, keepdims=True)
        am = jnp.min(jnp.where(logits == mx, lane, float(LANES)), axis=-1, keepdims=True)
        vals.append(mx)
        idxs.append(am)
        logits = jnp.where(lane == am, -3.0e38, logits)
    es = [jnp.exp(v - vals[0]) for v in vals]
    den = es[0] + es[1] + es[2] + es[3]
    idx_out = jnp.zeros(lane.shape, F32)
    gate_out = jnp.zeros(lane.shape, F32)
    for k in range(TOP_K):
        idx_out = jnp.where(lane == float(k), idxs[k], idx_out)
        gate_out = jnp.where(lane == float(k), es[k] / den, gate_out)
    idx_ref[...] = idx_out.astype(jnp.int32)
    gate_ref[...] = gate_out


def _outproj(merged, w_out_b, x, mod3, g_norm2, wr_pad, br_pad, bb, tt):
    b, t, d = x.shape
    tm = bb * tt
    nt = t // tt
    tok = lambda w: pl.BlockSpec((tm, w), lambda i, j: (i * nt + j, 0))
    modspec = lambda k: pl.BlockSpec((bb, 1, d), lambda i, j: (i, 0, k))
    return pl.pallas_call(
        _outproj_kernel,
        out_shape=[jax.ShapeDtypeStruct((b, t, d), F32), jax.ShapeDtypeStruct((b * t, d), F32),
                   jax.ShapeDtypeStruct((b * t, LANES), jnp.int32), jax.ShapeDtypeStruct((b * t, LANES), F32)],
        grid=(b // bb, nt),
        in_specs=[tok(d),
                  pl.BlockSpec((d, d), lambda i, j: (0, 0)),
                  pl.BlockSpec((bb, tt, d), lambda i, j: (i, j, 0)),
                  modspec(2),
                  pl.BlockSpec((1, 1, d), lambda i, j: (0, 0, 0)),
                  modspec(4), modspec(3),
                  pl.BlockSpec((d, LANES), lambda i, j: (0, 0)),
                  pl.BlockSpec((1, LANES), lambda i, j: (0, 0))],
        out_specs=[pl.BlockSpec((bb, tt, d), lambda i, j: (i, j, 0)), tok(d), tok(LANES), tok(LANES)],
        compiler_params=_params(("arbitrary", "arbitrary")),
    )(merged, w_out_b, x, mod3, g_norm2.reshape(1, 1, d), mod3, mod3, wr_pad, br_pad)


def _row_copy(src_ref, dst_ref, src_row, dst_row, sem):
    return pltpu.make_async_copy(src_ref.at[pl.ds(src_row, 1)], dst_ref.at[pl.ds(dst_row, 1)], sem)


def _dispatch_kernel(tok_ref, h_ref, o_ref, buf, sem):
    n = buf.shape[0]
    base = pl.program_id(0) * n

    def start(r, carry):
        _row_copy(h_ref, buf, tok_ref[base + r], r, sem).start()
        return carry

    def wait(r, carry):
        _row_copy(h_ref, buf, 0, r, sem).wait()
        return carry

    lax.fori_loop(0, n, start, 0, unroll=8)
    lax.fori_loop(0, n, wait, 0, unroll=8)
    o_ref[...] = buf[...].astype(o_ref.dtype)


def _dispatch(row_tok, h2):
    n_rows = row_tok.shape[0]
    d = h2.shape[1]
    return pl.pallas_call(
        _dispatch_kernel,
        out_shape=jax.ShapeDtypeStruct((n_rows, d), BF16),
        grid_spec=pltpu.PrefetchScalarGridSpec(
            num_scalar_prefetch=1,
            grid=(n_rows // GATHER_ROWS,),
            in_specs=[pl.BlockSpec(memory_space=pl.ANY)],
            out_specs=pl.BlockSpec((GATHER_ROWS, d), lambda i, tok: (i, 0)),
            scratch_shapes=[pltpu.VMEM((GATHER_ROWS, d), F32), pltpu.SemaphoreType.DMA(())],
        ),
        compiler_params=_params(("arbitrary",)),
    )(row_tok, h2)


def _moe_up_kernel(be_ref, nxt_ref, x_ref, bg_ref, bl_ref, w_hbm, o_ref, wbuf, sems, slot_ref, wg_s, wl_s, *, tf, nf):
    j, i = pl.program_id(0), pl.program_id(1)
    e = be_ref[i]

    def copies(ee, jj, sl):
        return [pltpu.make_async_copy(w_hbm.at[ee, :, pl.ds(pl.multiple_of((g * nf + jj) * tf, tf), tf)],
                                      wbuf.at[sl, g], sems.at[sl, g]) for g in range(2)]

    @pl.when((j == 0) & (i == 0))
    def _():
        slot_ref[0] = 0
        for cp in copies(e, j, 0):
            cp.start()

    @pl.when((i == 0) | (e != be_ref[jnp.maximum(i - 1, 0)]))
    def _():
        sl = slot_ref[0]
        for cp in copies(e, j, sl):
            cp.wait()
        nx = nxt_ref[i]
        more = nx >= 0

        @pl.when(more | (j + 1 < nf))
        def _():
            for cp in copies(jnp.where(more, nx, be_ref[0]), jnp.where(more, j, j + 1), 1 - sl):
                cp.start()

        wg_s[...] = wbuf[sl, 0].astype(BF16)
        wl_s[...] = wbuf[sl, 1].astype(BF16)
        slot_ref[0] = 1 - sl

    x = x_ref[...]
    glu = jnp.minimum(_dot(x, wg_s[...]) + bg_ref[...], SWIGLU_LIMIT)
    lin = jnp.clip(_dot(x, wl_s[...]) + bl_ref[...], -SWIGLU_LIMIT, SWIGLU_LIMIT)
    o_ref[...] = (glu * _sigmoid(SWIGLU_ALPHA * glu) * (lin + 1.0)).astype(o_ref.dtype)


def _moe_down_kernel(be_ref, nxt_ref, a_ref, b_ref, w_hbm, o_ref, wbuf, sems, slot_ref, w_s):
    i = pl.program_id(0)
    e = be_ref[i]

    def copy(ee, sl):
        return pltpu.make_async_copy(w_hbm.at[ee], wbuf.at[sl], sems.at[sl])

    @pl.when(i == 0)
    def _():
        slot_ref[0] = 0
        copy(e, 0).start()

    @pl.when((i == 0) | (e != be_ref[jnp.maximum(i - 1, 0)]))
    def _():
        sl = slot_ref[0]
        copy(e, sl).wait()
        nx = nxt_ref[i]

        @pl.when(nx >= 0)
        def _():
            copy(nx, 1 - sl).start()

        w_s[...] = wbuf[sl].astype(BF16)
        slot_ref[0] = 1 - sl

    o_ref[...] = _dot(a_ref[...], w_s[...]) + b_ref[...]


def _moe_experts(block_e, x_sorted, w_gate_up, b_gate_up, w_down, b_down):
    n_rows, d = x_sorted.shape
    n_blocks = n_rows // MOE_ROWS
    tf = 1024
    nf = D_FF // tf
    ids = jnp.arange(N_EXPERTS, dtype=jnp.int32)
    present = (block_e[None, :] == ids[:, None]).any(axis=1)
    later = jnp.where(present[None, :] & (ids[None, :] > ids[:, None]), ids[None, :], N_EXPERTS).min(axis=1)
    nxt = jnp.where(later == N_EXPERTS, -1, later).astype(jnp.int32)[block_e]
    bgu3 = b_gate_up.reshape(N_EXPERTS, 1, 2 * D_FF)
    hbm = pl.BlockSpec(memory_space=pl.ANY)
    act = pl.pallas_call(
        functools.partial(_moe_up_kernel, tf=tf, nf=nf),
        out_shape=jax.ShapeDtypeStruct((n_rows, D_FF), BF16),
        grid_spec=pltpu.PrefetchScalarGridSpec(
            num_scalar_prefetch=2,
            grid=(nf, n_blocks),
            in_specs=[pl.BlockSpec((MOE_ROWS, d), lambda j, i, be, nx: (i, 0)),
                      pl.BlockSpec((None, 1, tf), lambda j, i, be, nx: (be[i], 0, j)),
                      pl.BlockSpec((None, 1, tf), lambda j, i, be, nx: (be[i], 0, nf + j)),
                      hbm],
            out_specs=pl.BlockSpec((MOE_ROWS, tf), lambda j, i, be, nx: (i, j)),
            scratch_shapes=[pltpu.VMEM((2, 2, d, tf), F32), pltpu.SemaphoreType.DMA((2, 2)),
                            pltpu.SMEM((1,), jnp.int32), pltpu.VMEM((d, tf), BF16), pltpu.VMEM((d, tf), BF16)],
        ),
        compiler_params=_params(("arbitrary", "arbitrary")),
    )(block_e, nxt, x_sorted, bgu3, bgu3, w_gate_up)
    return pl.pallas_call(
        _moe_down_kernel,
        out_shape=jax.ShapeDtypeStruct((n_rows, d), F32),
        grid_spec=pltpu.PrefetchScalarGridSpec(
            num_scalar_prefetch=2,
            grid=(n_blocks,),
            in_specs=[pl.BlockSpec((MOE_ROWS, D_FF), lambda i, be, nx: (i, 0)),
                      pl.BlockSpec((None, 1, d), lambda i, be, nx: (be[i], 0, 0)),
                      hbm],
            out_specs=pl.BlockSpec((MOE_ROWS, d), lambda i, be, nx: (i, 0)),
            scratch_shapes=[pltpu.VMEM((2, D_FF, d), F32), pltpu.SemaphoreType.DMA((2,)),
                            pltpu.SMEM((1,), jnp.int32), pltpu.VMEM((D_FF, d), BF16)],
        ),
        compiler_params=_params(("arbitrary",)),
    )(block_e, nxt, act, b_down.reshape(N_EXPERTS, 1, d), w_down)


def _combine_kernel(dest_ref, x1_ref, g2_ref, gate_ref, rows_ref, o_ref, buf, sem):
    tm = buf.shape[1]
    nt = pl.num_programs(1)
    base = (pl.program_id(0) * nt + pl.program_id(1)) * (tm * TOP_K)

    def start(t, carry):
        for k in range(TOP_K):
            _row_copy(rows_ref, buf.at[k], dest_ref[base + t * TOP_K + k], t, sem).start()
        return carry

    def wait(t, carry):
        for k in range(TOP_K):
            _row_copy(rows_ref, buf.at[k], 0, t, sem).wait()
        return carry

    lax.fori_loop(0, tm, start, 0)
    lax.fori_loop(0, tm, wait, 0)
    gates = gate_ref[...]
    ff = buf[0] * gates[:, 0:1]
    for k in range(1, TOP_K):
        ff = ff + buf[k] * gates[:, k:k + 1]
    o_ref[...] = x1_ref[...] + g2_ref[...] * ff.reshape(x1_ref.shape)


def _combine(dest, x1, mod3, gates, out_rows, bb, tt):
    b, t, d = x1.shape
    nt = t // tt
    tm = bb * tt
    return pl.pallas_call(
        _combine_kernel,
        out_shape=jax.ShapeDtypeStruct((b, t, d), F32),
        grid_spec=pltpu.PrefetchScalarGridSpec(
            num_scalar_prefetch=1,
            grid=(b // bb, nt),
            in_specs=[pl.BlockSpec((bb, tt, d), lambda i, j, ds: (i, j, 0)),
                      pl.BlockSpec((bb, 1, d), lambda i, j, ds: (i, 0, 5)),
                      pl.BlockSpec((tm, LANES), lambda i, j, ds: (i * nt + j, 0)),
                      pl.BlockSpec(memory_space=pl.ANY)],
            out_specs=pl.BlockSpec((bb, tt, d), lambda i, j, ds: (i, j, 0)),
            scratch_shapes=[pltpu.VMEM((TOP_K, tm, d), F32), pltpu.SemaphoreType.DMA(())],
        ),
        compiler_params=_params(("arbitrary", "arbitrary")),
    )(dest, x1, mod3, gates, out_rows)


def _route(top_idx):
    t = top_idx.shape[0]
    n_assign = t * TOP_K
    flat_e = top_idx[:, :TOP_K].reshape(-1)
    flat_tok = jnp.arange(n_assign, dtype=jnp.int32) // TOP_K
    onehot = (flat_e[:, None] == jnp.arange(N_EXPERTS, dtype=jnp.int32)[None, :]).astype(jnp.int32)
    running = jnp.cumsum(onehot, axis=0)
    counts = running[-1]
    padded = (counts + MOE_ROWS - 1) // MOE_ROWS * MOE_ROWS
    pad_end = jnp.cumsum(padded)
    pad_start = pad_end - padded
    dest = jnp.sum(onehot * (running - 1 + pad_start[None, :]), axis=1)
    n_blocks = -(-(n_assign + N_EXPERTS * (MOE_ROWS - 1)) // MOE_ROWS)
    n_blocks = -(-n_blocks * MOE_ROWS // GATHER_ROWS) * GATHER_ROWS // MOE_ROWS
    row_tok = jnp.zeros((n_blocks * MOE_ROWS,), jnp.int32).at[dest].set(flat_tok, unique_indices=True)
    block_start = jnp.arange(n_blocks, dtype=jnp.int32) * MOE_ROWS
    block_e = jnp.minimum(jnp.sum((pad_end[None, :] <= block_start[:, None]).astype(jnp.int32), axis=1),
                          N_EXPERTS - 1)
    return row_tok, block_e, dest


def _mixers(x, mod, pos, p, *, prompt, caches=None, page_table=None):
    b, t, d = x.shape
    tokens = b * t
    mod3 = mod.reshape(b, 1, 6 * d)
    bb, tt = (1, 512) if prompt else (32, t)
    h = _normmod(x, p['g_norm1'], mod3, 1, 0, bb, tt)
    z = _mm(h, p['w_in'], min(tokens, 1024), Z_COLS // 5)
    lc = min(t, CHUNK)
    reps = CHUNK // lc
    ws_t = jnp.tile(p['w_s'][:, :lc, :lc], (1, reps, reps))
    bs_t = jnp.broadcast_to(jnp.tile(p['b_s'][:, :lc], (1, reps))[:, :, None], (A_GROUPS, CHUNK, CHUNK))
    gm = _gmlp(z, p['g_v_ln'], p['b_v_ln'], ws_t, bs_t, lc, want_vn=not prompt)
    a_out = gm[0]

    half = ROPE_DIM // 2
    inv_freq = jnp.exp(-math.log(ROPE_THETA) * jnp.arange(half, dtype=F32) / half)
    ang = pos.astype(F32)[:, None] * inv_freq[None, :]
    cos, sin = jnp.cos(ang), jnp.sin(ang)
    tab = jnp.concatenate([cos, cos, -sin, sin], axis=-1)
    tab = jnp.broadcast_to(tab[None], (b, t, LANES)).reshape(tokens, LANES)
    swap = lambda g: jnp.concatenate([g[half:], g[:half]])
    gains = [p['g_q_lat'].reshape(1, -1), p['g_kv_lat'].reshape(1, -1), p['g_q_nope'].reshape(1, -1),
             jnp.concatenate([p['g_q_rope'], swap(p['g_q_rope'])]).reshape(1, -1),
             jnp.concatenate([p['g_k_rope'], swap(p['g_k_rope'])]).reshape(1, -1),
             p['g_k_nope'].reshape(1, -1)]
    weights = [p['wq_n'], p['wq_r'], p['w_uk'], p['w_uv']]
    mla = _mla(z, tab, gains, weights, 256, prompt)
    c_kv, k_pe, rinv = mla[:3]
    if prompt:
        b_out = _attend_prompt(mla[3], mla[4], mla[5], 512)
    else:
        cache_latent, cache_rope_t, cache_rinv_t = caches
        rows = N_HEADS * t
        qa = _absorb(mla[3], p['w_uk']).reshape(b, rows, KV_LORA)
        qr = mla[4].reshape(b, rows, LANES)
        pad = PAGE_SIZE - t
        nlat = jnp.pad(c_kv.reshape(b, t, KV_LORA), ((0, 0), (0, pad), (0, 0)))
        nrope_t = jnp.pad(jnp.swapaxes(k_pe.reshape(b, t, ROPE_DIM), 1, 2), ((0, 0), (0, 0), (0, pad)))
        nrinv_t = jnp.pad(jnp.swapaxes(rinv.reshape(b, t, N_HEADS), 1, 2), ((0, 0), (0, 0), (0, pad)))
        o_lat = _decode(page_table, qa, qr, nlat, nrope_t, nrinv_t, cache_latent, cache_rope_t, cache_rinv_t, t)
        b_out = _upv(o_lat.reshape(tokens, N_HEADS * KV_LORA), p['w_uv'])
    merged = _merge(a_out, b_out, p['w_proj_a'], p['w_proj_b'], z, min(tokens, 512), 1024)
    bb2, tt2 = (1, 256) if prompt else (32, t)
    x1, h2, top_idx, gates = _outproj(merged, p['w_out'], x, mod3, p['g_norm2'], p['wr_pad'], p['br_pad'], bb2, tt2)
    vn = None if prompt else gm[1]
    return (x1, h2, top_idx, gates, mod3), (c_kv, k_pe, rinv, vn)


def kernel(x_prompt, x_sample, c_prompt, c_sample, cache_latent, cache_k_rope, cache_k_rinv, page_table,
           w_ada, b_ada, g_norm1, w_in, g_v_ln, b_v_ln, w_s, b_s, g_q_lat, w_uq, g_q_nope, g_q_rope,
           g_kv_lat, g_k_rope, w_uk, w_uv, g_k_nope, w_proj_a, w_proj_b, w_out, g_norm2,
           w_router, b_router, w_gate_up, b_gate_up, w_down, b_down):
    depth = w_ada.shape[0]
    bp, sp, d = x_prompt.shape
    bs, ss, _ = x_sample.shape
    past_len = page_table.shape[1] * PAGE_SIZE
    pos_p = jnp.arange(sp, dtype=jnp.int32)
    pos_s = past_len + jnp.arange(ss, dtype=jnp.int32)
    half = ROPE_DIM // 2
    y_p, y_s = x_prompt, x_sample
    outs = [[] for _ in range(7)]
    for l in range(depth):
        wi = w_in[l]
        kr = wi[:, 2816:2880]
        w_in_perm = jnp.concatenate(
            [wi[:, 0:2048], wi[:, 2880:6976], wi[:, 2048:2816], kr, kr[:, half:], kr[:, :half]], axis=1).astype(BF16)
        wq = w_uq[l]
        wq_rope = wq[:, :, NOPE_DIM:]
        p = {
            'g_norm1': g_norm1[l], 'w_in': w_in_perm, 'g_v_ln': g_v_ln[l], 'b_v_ln': b_v_ln[l],
            'w_s': w_s[l], 'b_s': b_s[l], 'g_q_lat': g_q_lat[l], 'g_q_nope': g_q_nope[l], 'g_q_rope': g_q_rope[l],
            'g_kv_lat': g_kv_lat[l], 'g_k_rope': g_k_rope[l], 'g_k_nope': g_k_nope[l],
            'wq_n': wq[:, :, :NOPE_DIM].reshape(Q_LORA, -1).astype(BF16),
            'wq_r': jnp.concatenate([wq_rope, wq_rope[:, :, half:], wq_rope[:, :, :half]],
                                    axis=-1).reshape(Q_LORA, -1).astype(BF16),
            'w_uk': w_uk[l].reshape(KV_LORA, -1).astype(BF16), 'w_uv': w_uv[l].reshape(KV_LORA, -1).astype(BF16),
            'w_proj_a': w_proj_a[l].astype(BF16), 'w_proj_b': w_proj_b[l].astype(BF16),
            'w_out': w_out[l].astype(BF16), 'g_norm2': g_norm2[l],
            'wr_pad': jnp.pad(w_router[l], ((0, 0), (0, LANES - N_EXPERTS))).astype(BF16),
            'br_pad': jnp.pad(b_router[l], (0, LANES - N_EXPERTS), constant_values=NEG_INF).reshape(1, LANES),
        }
        n_c = bp + bs
        c_all = jnp.pad(jnp.concatenate([c_prompt, c_sample], axis=0), ((0, -n_c % 8), (0, 0)))
        mod = _ada(c_all, w_ada[l], b_ada[l])
        caches = (cache_latent[l], jnp.swapaxes(cache_k_rope[l], 1, 2), jnp.swapaxes(cache_k_rinv[l], 1, 2))
        (x1_p, h2_p, idx_p, gate_p, mod3_p), aux_p = _mixers(y_p, mod[:bp], pos_p, p, prompt=True)
        (x1_s, h2_s, idx_s, gate_s, mod3_s), aux_s = _mixers(y_s, mod[bp:n_c], pos_s, p, prompt=False,
                                                            caches=caches, page_table=page_table)
        h2 = jnp.concatenate([h2_p, h2_s], axis=0)
        row_tok, block_e, dest = _route(jnp.concatenate([idx_p, idx_s], axis=0))
        x_sorted = _dispatch(row_tok, h2)
        out_rows = _moe_experts(block_e, x_sorted, w_gate_up[l], b_gate_up[l], w_down[l], b_down[l])
        n_p = bp * sp * TOP_K
        y_p = _combine(dest[:n_p], x1_p, mod3_p, gate_p, out_rows, 1, COMBINE_TOKENS)
        y_s = _combine(dest[n_p:], x1_s, mod3_s, gate_s, out_rows, COMBINE_TOKENS // ss, ss)
        for o, a in zip(outs, (aux_p[0].reshape(bp, sp, KV_LORA), aux_p[1].reshape(bp, sp, ROPE_DIM),
                               aux_p[2].reshape(bp, sp, N_HEADS), aux_s[0].reshape(bs, ss, KV_LORA),
                               aux_s[1].reshape(bs, ss, ROPE_DIM), aux_s[2].reshape(bs, ss, N_HEADS),
                               aux_s[3].reshape(bs, ss, D_A))):
            o.append(a)
    return (y_p, y_s) + tuple(jnp.stack(o) for o in outs)
```

```python
import functools
import math

import jax
import jax.numpy as jnp
from jax import lax
from jax.experimental import pallas as pl
from jax.experimental.pallas import tpu as pltpu

F32 = jnp.float32
BF16 = jnp.bfloat16

D_MODEL = 2048
D_A = D_MODEL // 2
A_GROUP = 128
A_GROUPS = D_A // A_GROUP
CHUNK = 128
N_HEADS = 16
Q_LORA = D_MODEL // 4
KV_LORA = D_MODEL // 8
NOPE_DIM = 128
ROPE_DIM = 64
V_DIM = 128
QK_DIM = NOPE_DIM + ROPE_DIM
ROPE_THETA = 10000.0
SCALE = 1.0 / math.sqrt(QK_DIM)
N_EXPERTS = 32
TOP_K = 4
D_FF = D_MODEL
SWIGLU_LIMIT = 7.0
SWIGLU_ALPHA = 1.702
EPS = 1e-6
NEG_INF = -1e30
PAGE_SIZE = 128

LANES = 128
MOE_ROWS = 256
GATHER_ROWS = 512
COMBINE_TOKENS = 128
PAGES_PER_STEP = 32
DECODE_CHAINS = 4
VMEM_LIMIT = 56 * 1024 * 1024

ZO_U = 0
ZO_V = ZO_U + D_A
ZO_GA = ZO_V + D_A
ZO_GB = ZO_GA + D_MODEL
ZO_Q = ZO_GB + D_MODEL
ZO_KV = ZO_Q + Q_LORA
ZO_KR = ZO_KV + KV_LORA
Z_COLS = ZO_KR + 2 * ROPE_DIM


def _params(sem):
    return pltpu.CompilerParams(dimension_semantics=sem, vmem_limit_bytes=VMEM_LIMIT)


def _dot(a, b):
    return jnp.dot(a, b, preferred_element_type=F32)


def _dot_nt(a, b):
    return lax.dot_general(a, b, (((1,), (1,)), ((), ())), preferred_element_type=F32)


def _sigmoid(x):
    return 1.0 / (1.0 + jnp.exp(-x))


def _gelu(x):
    c = math.sqrt(2.0 / math.pi)
    return 0.5 * x * (1.0 + jnp.tanh(c * (x + 0.044715 * (x * x * x))))


def _rms(x):
    return x * lax.rsqrt(jnp.mean(x * x, axis=-1, keepdims=True) + EPS)


def _ada_kernel(c_ref, w_ref, b_ref, o_ref):
    c = c_ref[...]
    a = (c * _sigmoid(c)).astype(BF16)
    o_ref[...] = _dot(a, w_ref[...].astype(BF16)) + b_ref[...]


def _ada(c_all, w_ada, b_ada):
    rows, d = c_all.shape
    n = w_ada.shape[1]
    tn = 1024
    return pl.pallas_call(
        _ada_kernel,
        out_shape=jax.ShapeDtypeStruct((rows, n), F32),
        grid=(n // tn,),
        in_specs=[pl.BlockSpec((rows, d), lambda j: (0, 0)),
                  pl.BlockSpec((d, tn), lambda j: (0, j)),
                  pl.BlockSpec((1, tn), lambda j: (0, j))],
        out_specs=pl.BlockSpec((rows, tn), lambda j: (0, j)),
        compiler_params=_params(("arbitrary",)),
    )(c_all, w_ada, b_ada.reshape(1, n))


def _normmod_kernel(x_ref, g_ref, sc_ref, sh_ref, o_ref):
    x = x_ref[...]
    y = _rms(x) * g_ref[...]
    y = y * (1.0 + sc_ref[...]) + sh_ref[...]
    o_ref[...] = y.reshape(o_ref.shape).astype(o_ref.dtype)


def _normmod(x, g, mod3, sc_idx, sh_idx, bb, tt):
    b, t, d = x.shape
    return pl.pallas_call(
        _normmod_kernel,
        out_shape=jax.ShapeDtypeStruct((b * t, d), BF16),
        grid=(b // bb, t // tt),
        in_specs=[pl.BlockSpec((bb, tt, d), lambda i, j: (i, j, 0)),
                  pl.BlockSpec((1, 1, d), lambda i, j: (0, 0, 0)),
                  pl.BlockSpec((bb, 1, d), lambda i, j: (i, 0, sc_idx)),
                  pl.BlockSpec((bb, 1, d), lambda i, j: (i, 0, sh_idx))],
        out_specs=pl.BlockSpec((bb * tt, d), lambda i, j: (i * (t // tt) + j, 0)),
        compiler_params=_params(("arbitrary", "arbitrary")),
    )(x, g.reshape(1, 1, d), mod3, mod3)


def _mm_kernel(x_ref, w_ref, o_ref):
    o_ref[...] = _dot(x_ref[...], w_ref[...]).astype(o_ref.dtype)


def _mm(x, w, tm, tn, out_dtype=F32):
    m, k = x.shape
    n = w.shape[1]
    return pl.pallas_call(
        _mm_kernel,
        out_shape=jax.ShapeDtypeStruct((m, n), out_dtype),
        grid=(m // tm, n // tn),
        in_specs=[pl.BlockSpec((tm, k), lambda i, j: (i, 0)),
                  pl.BlockSpec((k, tn), lambda i, j: (0, j))],
        out_specs=pl.BlockSpec((tm, tn), lambda i, j: (i, j)),
        compiler_params=_params(("arbitrary", "arbitrary")),
    )(x, w)


def _gmlp_kernel(zu_ref, zv_ref, gln_ref, bln_ref, ws_ref, bs_ref, a_ref, *maybe_vn_ref, lc):
    u = _gelu(zu_ref[...])
    gv = _gelu(zv_ref[...])
    xc = gv - jnp.mean(gv, axis=-1, keepdims=True)
    vn = xc * lax.rsqrt(jnp.mean(xc * xc, axis=-1, keepdims=True) + EPS)
    vn = vn * gln_ref[...] + bln_ref[...]
    if maybe_vn_ref:
        maybe_vn_ref[0][...] = vn
    row = lax.broadcasted_iota(jnp.int32, (CHUNK, CHUNK), 0)
    col = lax.broadcasted_iota(jnp.int32, (CHUNK, CHUNK), 1)
    mask = (col <= row) & ((row // lc) == (col // lc))
    vb = vn.astype(BF16)
    for g in range(A_GROUPS):
        w = jnp.where(mask, ws_ref[g], 0.0).astype(BF16)
        sl = slice(g * A_GROUP, (g + 1) * A_GROUP)
        s = _dot(w, vb[:, sl]) + bs_ref[g]
        a_ref[:, sl] = (u[:, sl] * s).astype(a_ref.dtype)


def _gmlp(z, g_ln, b_ln, ws_t, bs_t, lc, want_vn):
    t = z.shape[0]
    out_shape = [jax.ShapeDtypeStruct((t, D_A), BF16)]
    out_specs = [pl.BlockSpec((CHUNK, D_A), lambda i: (i, 0))]
    if want_vn:
        out_shape.append(jax.ShapeDtypeStruct((t, D_A), F32))
        out_specs.append(pl.BlockSpec((CHUNK, D_A), lambda i: (i, 0)))
    return pl.pallas_call(
        functools.partial(_gmlp_kernel, lc=lc),
        out_shape=out_shape,
        grid=(t // CHUNK,),
        in_specs=[pl.BlockSpec((CHUNK, D_A), lambda i: (i, ZO_U // D_A)),
                  pl.BlockSpec((CHUNK, D_A), lambda i: (i, ZO_V // D_A)),
                  pl.BlockSpec((1, D_A), lambda i: (0, 0)),
                  pl.BlockSpec((1, D_A), lambda i: (0, 0)),
                  pl.BlockSpec((A_GROUPS, CHUNK, CHUNK), lambda i: (0, 0, 0)),
                  pl.BlockSpec((A_GROUPS, CHUNK, CHUNK), lambda i: (0, 0, 0))],
        out_specs=out_specs,
        compiler_params=_params(("arbitrary",)),
    )(z, z, g_ln.reshape(1, D_A), b_ln.reshape(1, D_A), ws_t, bs_t)


def _rope_pair(y2, gain2, tab):
    p = y2 * gain2 * tab
    return p + pltpu.roll(p, ROPE_DIM, 1)


def _mla_kernel(zq_ref, zkv_ref, zkr_ref, tab_ref, gql_ref, gkv_ref, gqn_ref, gq2_ref, gk2_ref, gkn_ref,
                wqn_ref, wqr_ref, wuk_ref, wuv_ref, ckv_ref, kpe_ref, rinv_ref, *qkv_refs, prompt):
    tab = tab_ref[...]
    q_lat = (_rms(zq_ref[...]) * gql_ref[...]).astype(BF16)
    c_kv = _rms(zkv_ref[...]) * gkv_ref[...]
    ckv_ref[...] = c_kv
    kr2 = zkr_ref[...]
    kr2 = kr2 * lax.rsqrt(jnp.sum(kr2 * kr2, axis=-1, keepdims=True) / (2 * ROPE_DIM) + EPS)
    kpe2 = _rope_pair(kr2, gk2_ref[...], tab)
    kpe_ref[...] = kpe2[:, :ROPE_DIM]
    cb = c_kv.astype(BF16)
    tm = cb.shape[0]
    lane = lax.broadcasted_iota(jnp.int32, (tm, N_HEADS), 1)
    rinv_all = jnp.zeros((tm, N_HEADS), F32)
    if prompt:
        col = lax.broadcasted_iota(jnp.int32, kpe2.shape, 1)
        kp_b = jnp.where(col < ROPE_DIM, kpe2, 0.0).astype(BF16)
    for h in range(N_HEADS):
        sl = slice(h * NOPE_DIM, (h + 1) * NOPE_DIM)
        qn = _rms(_dot(q_lat, wqn_ref[:, sl])) * gqn_ref[...]
        qr2 = _dot(q_lat, wqr_ref[:, sl])
        qr2 = qr2 * lax.rsqrt(jnp.sum(qr2 * qr2, axis=-1, keepdims=True) / (2 * ROPE_DIM) + EPS)
        qp = (_rope_pair(qr2, gq2_ref[...], tab) * SCALE).astype(BF16)
        k_raw = _dot(cb, wuk_ref[:, sl])
        rinv = lax.rsqrt(jnp.mean(k_raw * k_raw, axis=-1, keepdims=True) + EPS)
        rinv_all = jnp.where(lane == h, rinv, rinv_all)
        if prompt:
            q_ref, k_ref, v_ref = qkv_refs
            lo = slice(2 * h * NOPE_DIM, (2 * h + 1) * NOPE_DIM)
            hi = slice((2 * h + 1) * NOPE_DIM, (2 * h + 2) * NOPE_DIM)
            q_ref[:, lo] = (qn * SCALE).astype(BF16)
            q_ref[:, hi] = qp
            k_ref[:, lo] = (k_raw * rinv * gkn_ref[...]).astype(BF16)
            k_ref[:, hi] = kp_b
            v_ref[sl, :] = _dot_nt(wuv_ref[sl, :], cb).astype(BF16)
        else:
            qn_ref, qp_ref = qkv_refs
            qn_ref[:, sl] = (qn * gkn_ref[...]).astype(BF16)
            qp_ref[:, sl] = qp
    rinv_ref[...] = rinv_all


def _mla(z, tab, gains, weights, tm, prompt):
    t = z.shape[0]
    hd = N_HEADS * NOPE_DIM
    row = lambda w: pl.BlockSpec((tm, w), lambda i: (i, 0))
    full = lambda a: pl.BlockSpec(a.shape, lambda i: (0,) * a.ndim)
    out_shape = [jax.ShapeDtypeStruct((t, KV_LORA), F32), jax.ShapeDtypeStruct((t, ROPE_DIM), F32),
                 jax.ShapeDtypeStruct((t, N_HEADS), F32)]
    out_specs = [row(KV_LORA), row(ROPE_DIM), row(N_HEADS)]
    widths = (2 * hd, 2 * hd) if prompt else (hd, hd)
    out_shape += [jax.ShapeDtypeStruct((t, w), BF16) for w in widths]
    out_specs += [row(w) for w in widths]
    if prompt:
        out_shape.append(jax.ShapeDtypeStruct((hd, t), BF16))
        out_specs.append(pl.BlockSpec((hd, tm), lambda i: (0, i)))
    return pl.pallas_call(
        functools.partial(_mla_kernel, prompt=prompt),
        out_shape=out_shape,
        grid=(t // tm,),
        in_specs=[pl.BlockSpec((tm, Q_LORA), lambda i: (i, ZO_Q // Q_LORA)),
                  pl.BlockSpec((tm, KV_LORA), lambda i: (i, ZO_KV // KV_LORA)),
                  pl.BlockSpec((tm, LANES), lambda i: (i, ZO_KR // LANES)),
                  row(LANES)] + [full(a) for a in gains] + [full(a) for a in weights],
        out_specs=out_specs,
        compiler_params=_params(("arbitrary",)),
    )(z, z, z, tab, *gains, *weights)


def _fa_kernel(q_ref, k_ref, vt_ref, o_ref, sa_ref, sb_ref, m_ref, l_ref, acc_ref, *, tq, tk):
    i = pl.program_id(1)
    q = q_ref[...]
    m_ref[...] = jnp.full(m_ref.shape, NEG_INF, F32)
    l_ref[...] = jnp.zeros(l_ref.shape, F32)
    acc_ref[...] = jnp.zeros(acc_ref.shape, F32)

    def keys(j):
        return pl.ds(pl.multiple_of(j * tk, tk), tk)

    def qk(j):
        return _dot_nt(k_ref[keys(j), :], q)

    def update(s, j):
        m_old = m_ref[...]
        m_new = jnp.maximum(m_old, jnp.max(s, axis=0, keepdims=True))
        corr = jnp.exp(m_old - m_new)
        p = jnp.exp(s - m_new)
        l_ref[...] = l_ref[...] * corr + jnp.sum(p, axis=0, keepdims=True)
        m_ref[...] = m_new
        acc_ref[...] = acc_ref[...] * corr + _dot(vt_ref[:, keys(j)], p.astype(BF16))

    sa_ref[...] = qk(0)

    def pair(u, carry):
        sb_ref[...] = qk(2 * u + 1)
        update(sa_ref[...], 2 * u)
        sa_ref[...] = qk(2 * u + 2)
        update(sb_ref[...], 2 * u + 1)
        return carry

    lax.fori_loop(0, i, pair, 0)
    sb_ref[...] = qk(2 * i + 1)
    key = lax.broadcasted_iota(jnp.int32, (tk, tq), 0)
    qry = lax.broadcasted_iota(jnp.int32, (tk, tq), 1)
    update(jnp.where(key <= qry, sa_ref[...], NEG_INF), 2 * i)
    update(jnp.where(key + tk <= qry, sb_ref[...], NEG_INF), 2 * i + 1)
    o_ref[...] = (acc_ref[...] / l_ref[...]).T.astype(o_ref.dtype)


def _attend_prompt(q, k, vt, tq):
    t = q.shape[0]
    tk = tq // 2
    return pl.pallas_call(
        functools.partial(_fa_kernel, tq=tq, tk=tk),
        out_shape=jax.ShapeDtypeStruct((t, N_HEADS * V_DIM), BF16),
        grid=(N_HEADS, t // tq),
        in_specs=[pl.BlockSpec((tq, 2 * NOPE_DIM), lambda h, i: (i, h)),
                  pl.BlockSpec((t, 2 * NOPE_DIM), lambda h, i: (0, h)),
                  pl.BlockSpec((V_DIM, t), lambda h, i: (h, 0))],
        out_specs=pl.BlockSpec((tq, V_DIM), lambda h, i: (i, h)),
        scratch_shapes=[pltpu.VMEM((tk, tq), F32), pltpu.VMEM((tk, tq), F32),
                        pltpu.VMEM((1, tq), F32), pltpu.VMEM((1, tq), F32), pltpu.VMEM((V_DIM, tq), F32)],
        compiler_params=_params(("arbitrary", "arbitrary")),
    )(q, k, vt)


def _absorb_kernel(qn_ref, wuk_ref, qa_ref):
    qa_ref[...] = (_dot_nt(qn_ref[...], wuk_ref[...]) * SCALE).astype(qa_ref.dtype)


def _absorb(qn, w_uk_b):
    tokens = qn.shape[0]
    return pl.pallas_call(
        _absorb_kernel,
        out_shape=jax.ShapeDtypeStruct((tokens, N_HEADS * KV_LORA), BF16),
        grid=(N_HEADS,),
        in_specs=[pl.BlockSpec((tokens, NOPE_DIM), lambda h: (0, h)),
                  pl.BlockSpec((KV_LORA, NOPE_DIM), lambda h: (0, h))],
        out_specs=pl.BlockSpec((tokens, KV_LORA), lambda h: (0, h)),
        compiler_params=_params(("arbitrary",)),
    )(qn, w_uk_b)


def _decode_kernel(pt_ref, qa_ref, qr_ref, nlat_ref, nrope_ref, nrinv_ref, lat_hbm, rope_hbm, rinv_hbm, o_ref,
                   lat_buf, rope_buf, rinv_buf, sems, ck_s, kr_s, ri_s, m_ref, l_ref, acc_ref, *, t_new):
    pc = PAGES_PER_STEP
    b, c = pl.program_id(0), pl.program_id(1)
    nb, nc = pl.num_programs(0), pl.num_programs(1)
    n = b * nc + c
    slot = n % 2
    rows = N_HEADS * t_new

    def page_copies(bi, ci, sl):
        copies = []
        for j in range(pc):
            page = pt_ref[bi, ci * pc + j]
            copies.append(pltpu.make_async_copy(lat_hbm.at[page], lat_buf.at[sl, j], sems.at[sl, 0]))
            copies.append(pltpu.make_async_copy(rope_hbm.at[page], rope_buf.at[sl, j], sems.at[sl, 1]))
            copies.append(pltpu.make_async_copy(rinv_hbm.at[page], rinv_buf.at[sl, j], sems.at[sl, 2]))
        return copies

    @pl.when(n == 0)
    def _():
        for cp in page_copies(b, c, slot):
            cp.start()

    @pl.when(n + 1 < nb * nc)
    def _():
        wrap = c + 1 == nc
        for cp in page_copies(jnp.where(wrap, b + 1, b), jnp.where(wrap, 0, c + 1), 1 - slot):
            cp.start()

    for cp in page_copies(b, c, slot):
        cp.wait()
    qa = qa_ref[...]
    qr = qr_ref[...][:, :ROPE_DIM]

    def scores(ck, krt, rit):
        n = ck.shape[0]
        return (_dot_nt(qa, ck).reshape(t_new, N_HEADS, n) * rit[None]).reshape(rows, n) + _dot(qr, krt)

    def fold(g, s, ck):
        m_old = m_ref[g]
        m_new = jnp.maximum(m_old, jnp.max(s, axis=-1, keepdims=True))
        corr = jnp.exp(m_old - m_new)
        p = jnp.exp(s - m_new)
        l_ref[g] = l_ref[g] * corr + jnp.sum(p, axis=-1, keepdims=True)
        m_ref[g] = m_new
        acc_ref[g] = acc_ref[g] * corr + _dot(p.astype(BF16), ck)

    @pl.when(c == 0)
    def _():
        m_ref[...] = jnp.full(m_ref.shape, NEG_INF, F32)
        l_ref[...] = jnp.zeros(l_ref.shape, F32)
        acc_ref[...] = jnp.zeros(acc_ref.shape, F32)
        ck = nlat_ref[...].astype(BF16)
        s = scores(ck, nrope_ref[...].astype(BF16), nrinv_ref[...])
        row = lax.broadcasted_iota(jnp.int32, s.shape, 0)
        col = lax.broadcasted_iota(jnp.int32, s.shape, 1)
        fold(0, jnp.where(col <= row // N_HEADS, s, NEG_INF), ck)

    for j in range(pc):
        ck_s[j * PAGE_SIZE:(j + 1) * PAGE_SIZE, :] = lat_buf[slot, j].astype(BF16)
        kr_s[:, j * PAGE_SIZE:(j + 1) * PAGE_SIZE] = rope_buf[slot, j].astype(BF16)
        ri_s[:, j * PAGE_SIZE:(j + 1) * PAGE_SIZE] = rinv_buf[slot, j]
    keys = pc * PAGE_SIZE // DECODE_CHAINS
    cks = [ck_s[g * keys:(g + 1) * keys, :] for g in range(DECODE_CHAINS)]
    ss = [scores(cks[g], kr_s[:, g * keys:(g + 1) * keys], ri_s[:, g * keys:(g + 1) * keys])
          for g in range(DECODE_CHAINS)]
    for g in range(DECODE_CHAINS):
        fold(g, ss[g], cks[g])

    @pl.when(c == nc - 1)
    def _():
        m = m_ref[0]
        for g in range(1, DECODE_CHAINS):
            m = jnp.maximum(m, m_ref[g])
        l = jnp.zeros(m.shape, F32)
        acc = jnp.zeros(acc_ref.shape[1:], F32)
        for g in range(DECODE_CHAINS):
            w = jnp.exp(m_ref[g] - m)
            l = l + l_ref[g] * w
            acc = acc + acc_ref[g] * w
        o_ref[...] = (acc / l).astype(o_ref.dtype)


def _decode(page_table, qa, qr, nlat, nrope_t, nrinv_t, cache_latent, cache_rope_t, cache_rinv_t, t_new):
    b, n_pages = page_table.shape
    pc = PAGES_PER_STEP
    rows = N_HEADS * t_new
    seq = lambda r, w: pl.BlockSpec((None, r, w), lambda i, c, pt: (i, 0, 0))
    hbm = pl.BlockSpec(memory_space=pl.ANY)
    keys = pc * PAGE_SIZE
    grid_spec = pltpu.PrefetchScalarGridSpec(
        num_scalar_prefetch=1,
        grid=(b, n_pages // pc),
        in_specs=[seq(rows, KV_LORA), seq(rows, LANES),
                  seq(PAGE_SIZE, KV_LORA), seq(ROPE_DIM, PAGE_SIZE), seq(N_HEADS, PAGE_SIZE), hbm, hbm, hbm],
        out_specs=seq(rows, KV_LORA),
        scratch_shapes=[pltpu.VMEM((2, pc, PAGE_SIZE, KV_LORA), F32), pltpu.VMEM((2, pc, ROPE_DIM, PAGE_SIZE), F32),
                        pltpu.VMEM((2, pc, N_HEADS, PAGE_SIZE), F32), pltpu.SemaphoreType.DMA((2, 3)),
                        pltpu.VMEM((keys, KV_LORA), BF16), pltpu.VMEM((ROPE_DIM, keys), BF16),
                        pltpu.VMEM((N_HEADS, keys), F32),
                        pltpu.VMEM((DECODE_CHAINS, rows, 1), F32), pltpu.VMEM((DECODE_CHAINS, rows, 1), F32),
                        pltpu.VMEM((DECODE_CHAINS, rows, KV_LORA), F32)],
    )
    return pl.pallas_call(
        functools.partial(_decode_kernel, t_new=t_new),
        out_shape=jax.ShapeDtypeStruct((b, rows, KV_LORA), F32),
        grid_spec=grid_spec,
        compiler_params=_params(("arbitrary", "arbitrary")),
    )(page_table, qa, qr, nlat, nrope_t, nrinv_t, cache_latent, cache_rope_t, cache_rinv_t)


def _upv_kernel(o_ref, wuv_ref, out_ref):
    out_ref[...] = _dot(o_ref[...].astype(BF16), wuv_ref[...]).astype(out_ref.dtype)


def _upv(o_lat, w_uv_b):
    tokens = o_lat.shape[0]
    return pl.pallas_call(
        _upv_kernel,
        out_shape=jax.ShapeDtypeStruct((tokens, N_HEADS * V_DIM), BF16),
        grid=(N_HEADS,),
        in_specs=[pl.BlockSpec((tokens, KV_LORA), lambda h: (0, h)),
                  pl.BlockSpec((KV_LORA, V_DIM), lambda h: (0, h))],
        out_specs=pl.BlockSpec((tokens, V_DIM), lambda h: (0, h)),
        compiler_params=_params(("arbitrary",)),
    )(o_lat, w_uv_b)


def _merge_kernel(a_ref, b_ref, wa_ref, wb_ref, ga_ref, gb_ref, o_ref):
    pa = _dot(a_ref[...], wa_ref[...])
    pb = _dot(b_ref[...], wb_ref[...])
    o_ref[...] = (_sigmoid(ga_ref[...]) * pa + _sigmoid(gb_ref[...]) * pb).astype(o_ref.dtype)


def _merge(a_out, b_out, wpa, wpb, z, tm, tn):
    t = a_out.shape[0]
    return pl.pallas_call(
        _merge_kernel,
        out_shape=jax.ShapeDtypeStruct((t, D_MODEL), BF16),
        grid=(t // tm, D_MODEL // tn),
        in_specs=[pl.BlockSpec((tm, D_A), lambda i, j: (i, 0)),
                  pl.BlockSpec((tm, D_MODEL), lambda i, j: (i, 0)),
                  pl.BlockSpec((D_A, tn), lambda i, j: (0, j)),
                  pl.BlockSpec((D_MODEL, tn), lambda i, j: (0, j)),
                  pl.BlockSpec((tm, tn), lambda i, j: (i, ZO_GA // tn + j)),
                  pl.BlockSpec((tm, tn), lambda i, j: (i, ZO_GB // tn + j))],
        out_specs=pl.BlockSpec((tm, tn), lambda i, j: (i, j)),
        compiler_params=_params(("arbitrary", "arbitrary")),
    )(a_out, b_out, wpa, wpb, z, z)


def _outproj_kernel(m_ref, wo_ref, x_ref, g1_ref, gn_ref, sc_ref, sh_ref, wr_ref, br_ref,
                    x1_ref, h2_ref, idx_ref, gate_ref):
    shape3 = x_ref.shape
    y = _dot(m_ref[...], wo_ref[...])
    x1 = x_ref[...] + g1_ref[...] * y.reshape(shape3)
    x1_ref[...] = x1
    h2 = _rms(x1) * gn_ref[...]
    h2 = (h2 * (1.0 + sc_ref[...]) + sh_ref[...]).reshape(y.shape)
    h2_ref[...] = h2
    logits = _dot(h2.astype(BF16), wr_ref[...]) + br_ref[...]
    lane = lax.broadcasted_iota(jnp.int32, logits.shape, 1).astype(F32)
    vals, idxs = [], []
    for _ in range(TOP_K):
        mx = jnp.max(logits, axis=-1, keepdims=True)
        am = jnp.min(jnp.where(logits == mx, lane, float(LANES)), axis=-1, keepdims=True)
        vals.append(mx)
        idxs.append(am)
        logits = jnp.where(lane == am, -3.0e38, logits)
    es = [jnp.exp(v - vals[0]) for v in vals]
    den = es[0] + es[1] + es[2] + es[3]
    idx_out = jnp.zeros(lane.shape, F32)
    gate_out = jnp.zeros(lane.shape, F32)
    for k in range(TOP_K):
        idx_out = jnp.where(lane == float(k), idxs[k], idx_out)
        gate_out = jnp.where(lane == float(k), es[k] / den, gate_out)
    idx_ref[...] = idx_out.astype(jnp.int32)
    gate_ref[...] = gate_out


def _outproj(merged, w_out_b, x, mod3, g_norm2, wr_pad, br_pad, bb, tt):
    b, t, d = x.shape
    tm = bb * tt
    nt = t // tt
    tok = lambda w: pl.BlockSpec((tm, w), lambda i, j: (i * nt + j, 0))
    modspec = lambda k: pl.BlockSpec((bb, 1, d), lambda i, j: (i, 0, k))
    return pl.pallas_call(
        _outproj_kernel,
        out_shape=[jax.ShapeDtypeStruct((b, t, d), F32), jax.ShapeDtypeStruct((b * t, d), F32),
                   jax.ShapeDtypeStruct((b * t, LANES), jnp.int32), jax.ShapeDtypeStruct((b * t, LANES), F32)],
        grid=(b // bb, nt),
        in_specs=[tok(d),
                  pl.BlockSpec((d, d), lambda i, j: (0, 0)),
                  pl.BlockSpec((bb, tt, d), lambda i, j: (i, j, 0)),
                  modspec(2),
                  pl.BlockSpec((1, 1, d), lambda i, j: (0, 0, 0)),
                  modspec(4), modspec(3),
                  pl.BlockSpec((d, LANES), lambda i, j: (0, 0)),
                  pl.BlockSpec((1, LANES), lambda i, j: (0, 0))],
        out_specs=[pl.BlockSpec((bb, tt, d), lambda i, j: (i, j, 0)), tok(d), tok(LANES), tok(LANES)],
        compiler_params=_params(("arbitrary", "arbitrary")),
    )(merged, w_out_b, x, mod3, g_norm2.reshape(1, 1, d), mod3, mod3, wr_pad, br_pad)


def _row_copy(src_ref, dst_ref, src_row, dst_row, sem):
    return pltpu.make_async_copy(src_ref.at[pl.ds(src_row, 1)], dst_ref.at[pl.ds(dst_row, 1)], sem)


def _dispatch_kernel(tok_ref, h_ref, o_ref, buf, sem):
    n = buf.shape[0]
    base = pl.program_id(0) * n

    def start(r, carry):
        _row_copy(h_ref, buf, tok_ref[base + r], r, sem).start()
        return carry

    lax.fori_loop(0, n, start, 0, unroll=8)
    pltpu.make_async_copy(h_ref.at[pl.ds(0, n)], buf, sem).wait()
    o_ref[...] = buf[...].astype(o_ref.dtype)


def _dispatch(row_tok, h2):
    n_rows = row_tok.shape[0]
    d = h2.shape[1]
    return pl.pallas_call(
        _dispatch_kernel,
        out_shape=jax.ShapeDtypeStruct((n_rows, d), BF16),
        grid_spec=pltpu.PrefetchScalarGridSpec(
            num_scalar_prefetch=1,
            grid=(n_rows // GATHER_ROWS,),
            in_specs=[pl.BlockSpec(memory_space=pl.ANY)],
            out_specs=pl.BlockSpec((GATHER_ROWS, d), lambda i, tok: (i, 0)),
            scratch_shapes=[pltpu.VMEM((GATHER_ROWS, d), F32), pltpu.SemaphoreType.DMA(())],
        ),
        compiler_params=_params(("arbitrary",)),
    )(row_tok, h2)


def _moe_up_kernel(be_ref, nxt_ref, used_ref, x_ref, bg_ref, bl_ref, w_hbm, o_ref, wbuf, sems, slot_ref, wg_s, wl_s, *, tf, nf):
    j, i = pl.program_id(0), pl.program_id(1)
    e = be_ref[i]

    def copies(ee, jj, sl):
        return [pltpu.make_async_copy(w_hbm.at[ee, :, pl.ds(pl.multiple_of((g * nf + jj) * tf, tf), tf)],
                                      wbuf.at[sl, g], sems.at[sl, g]) for g in range(2)]

    @pl.when((j == 0) & (i == 0))
    def _():
        slot_ref[0] = 0
        for cp in copies(e, j, 0):
            cp.start()

    @pl.when((i == 0) | (e != be_ref[jnp.maximum(i - 1, 0)]))
    def _():
        sl = slot_ref[0]
        for cp in copies(e, j, sl):
            cp.wait()
        nx = nxt_ref[i]
        more = nx >= 0

        @pl.when(more | (j + 1 < nf))
        def _():
            for cp in copies(jnp.where(more, nx, be_ref[0]), jnp.where(more, j, j + 1), 1 - sl):
                cp.start()

        wg_s[...] = wbuf[sl, 0].astype(BF16)
        wl_s[...] = wbuf[sl, 1].astype(BF16)
        slot_ref[0] = 1 - sl

    @pl.when(i < used_ref[0])
    def _():
        x = x_ref[...]
        glu = jnp.minimum(_dot(x, wg_s[...]) + bg_ref[...], SWIGLU_LIMIT)
        lin = jnp.clip(_dot(x, wl_s[...]) + bl_ref[...], -SWIGLU_LIMIT, SWIGLU_LIMIT)
        o_ref[...] = (glu * _sigmoid(SWIGLU_ALPHA * glu) * (lin + 1.0)).astype(o_ref.dtype)

    @pl.when(i >= used_ref[0])
    def _():
        o_ref[...] = jnp.zeros(o_ref.shape, o_ref.dtype)


def _moe_down_kernel(be_ref, nxt_ref, used_ref, a_ref, b_ref, w_hbm, o_ref, wbuf, sems, slot_ref, w_s):
    i = pl.program_id(0)
    e = be_ref[i]

    def copy(ee, sl):
        return pltpu.make_async_copy(w_hbm.at[ee], wbuf.at[sl], sems.at[sl])

    @pl.when(i == 0)
    def _():
        slot_ref[0] = 0
        copy(e, 0).start()

    @pl.when((i == 0) | (e != be_ref[jnp.maximum(i - 1, 0)]))
    def _():
        sl = slot_ref[0]
        copy(e, sl).wait()
        nx = nxt_ref[i]

        @pl.when(nx >= 0)
        def _():
            copy(nx, 1 - sl).start()

        w_s[...] = wbuf[sl].astype(BF16)
        slot_ref[0] = 1 - sl

    @pl.when(i < used_ref[0])
    def _():
        o_ref[...] = _dot(a_ref[...], w_s[...]) + b_ref[...]

    @pl.when(i >= used_ref[0])
    def _():
        o_ref[...] = jnp.zeros(o_ref.shape, o_ref.dtype)


def _moe_experts(block_e, n_used, x_sorted, w_gate_up, b_gate_up, w_down, b_down):
    n_rows, d = x_sorted.shape
    n_blocks = n_rows // MOE_ROWS
    tf = 1024
    nf = D_FF // tf
    ids = jnp.arange(N_EXPERTS, dtype=jnp.int32)
    present = (block_e[None, :] == ids[:, None]).any(axis=1)
    later = jnp.where(present[None, :] & (ids[None, :] > ids[:, None]), ids[None, :], N_EXPERTS).min(axis=1)
    nxt = jnp.where(later == N_EXPERTS, -1, later).astype(jnp.int32)[block_e]
    bgu3 = b_gate_up.reshape(N_EXPERTS, 1, 2 * D_FF)
    hbm = pl.BlockSpec(memory_space=pl.ANY)
    act = pl.pallas_call(
        functools.partial(_moe_up_kernel, tf=tf, nf=nf),
        out_shape=jax.ShapeDtypeStruct((n_rows, D_FF), BF16),
        grid_spec=pltpu.PrefetchScalarGridSpec(
            num_scalar_prefetch=3,
            grid=(nf, n_blocks),
            in_specs=[pl.BlockSpec((MOE_ROWS, d), lambda j, i, be, nx, nu: (i, 0)),
                      pl.BlockSpec((None, 1, tf), lambda j, i, be, nx, nu: (be[i], 0, j)),
                      pl.BlockSpec((None, 1, tf), lambda j, i, be, nx, nu: (be[i], 0, nf + j)),
                      hbm],
            out_specs=pl.BlockSpec((MOE_ROWS, tf), lambda j, i, be, nx, nu: (i, j)),
            scratch_shapes=[pltpu.VMEM((2, 2, d, tf), F32), pltpu.SemaphoreType.DMA((2, 2)),
                            pltpu.SMEM((1,), jnp.int32), pltpu.VMEM((d, tf), BF16), pltpu.VMEM((d, tf), BF16)],
        ),
        compiler_params=_params(("arbitrary", "arbitrary")),
    )(block_e, nxt, n_used, x_sorted, bgu3, bgu3, w_gate_up)
    return pl.pallas_call(
        _moe_down_kernel,
        out_shape=jax.ShapeDtypeStruct((n_rows, d), F32),
        grid_spec=pltpu.PrefetchScalarGridSpec(
            num_scalar_prefetch=3,
            grid=(n_blocks,),
            in_specs=[pl.BlockSpec((MOE_ROWS, D_FF), lambda i, be, nx, nu: (i, 0)),
                      pl.BlockSpec((None, 1, d), lambda i, be, nx, nu: (be[i], 0, 0)),
                      hbm],
            out_specs=pl.BlockSpec((MOE_ROWS, d), lambda i, be, nx, nu: (i, 0)),
            scratch_shapes=[pltpu.VMEM((2, D_FF, d), F32), pltpu.SemaphoreType.DMA((2,)),
                            pltpu.SMEM((1,), jnp.int32), pltpu.VMEM((D_FF, d), BF16)],
        ),
        compiler_params=_params(("arbitrary",)),
    )(block_e, nxt, n_used, act, b_down.reshape(N_EXPERTS, 1, d), w_down)


def _combine_kernel(dest_ref, x1_ref, g2_ref, gate_ref, rows_ref, o_ref, buf, sem):
    tm = buf.shape[1]
    nt = pl.num_programs(1)
    base = (pl.program_id(0) * nt + pl.program_id(1)) * (tm * TOP_K)

    def start(t, carry):
        for k in range(TOP_K):
            _row_copy(rows_ref, buf.at[k], dest_ref[base + t * TOP_K + k], t, sem).start()
        return carry

    lax.fori_loop(0, tm, start, 0, unroll=2)
    for k in range(TOP_K):
        pltpu.make_async_copy(rows_ref.at[pl.ds(0, tm)], buf.at[k], sem).wait()
    gates = gate_ref[...]
    ff = buf[0] * gates[:, 0:1]
    for k in range(1, TOP_K):
        ff = ff + buf[k] * gates[:, k:k + 1]
    o_ref[...] = x1_ref[...] + g2_ref[...] * ff.reshape(x1_ref.shape)


def _combine(dest, x1, mod3, gates, out_rows, bb, tt):
    b, t, d = x1.shape
    nt = t // tt
    tm = bb * tt
    return pl.pallas_call(
        _combine_kernel,
        out_shape=jax.ShapeDtypeStruct((b, t, d), F32),
        grid_spec=pltpu.PrefetchScalarGridSpec(
            num_scalar_prefetch=1,
            grid=(b // bb, nt),
            in_specs=[pl.BlockSpec((bb, tt, d), lambda i, j, ds: (i, j, 0)),
                      pl.BlockSpec((bb, 1, d), lambda i, j, ds: (i, 0, 5)),
                      pl.BlockSpec((tm, LANES), lambda i, j, ds: (i * nt + j, 0)),
                      pl.BlockSpec(memory_space=pl.ANY)],
            out_specs=pl.BlockSpec((bb, tt, d), lambda i, j, ds: (i, j, 0)),
            scratch_shapes=[pltpu.VMEM((TOP_K, tm, d), F32), pltpu.SemaphoreType.DMA(())],
        ),
        compiler_params=_params(("arbitrary", "arbitrary")),
    )(dest, x1, mod3, gates, out_rows)


def _route(top_idx):
    t = top_idx.shape[0]
    n_assign = t * TOP_K
    flat_e = top_idx[:, :TOP_K].reshape(-1)
    flat_tok = jnp.arange(n_assign, dtype=jnp.int32) // TOP_K
    onehot = (flat_e[:, None] == jnp.arange(N_EXPERTS, dtype=jnp.int32)[None, :]).astype(jnp.int32)
    running = jnp.cumsum(onehot, axis=0)
    counts = running[-1]
    padded = (counts + MOE_ROWS - 1) // MOE_ROWS * MOE_ROWS
    pad_end = jnp.cumsum(padded)
    pad_start = pad_end - padded
    dest = jnp.sum(onehot * (running - 1 + pad_start[None, :]), axis=1)
    n_blocks = -(-(n_assign + N_EXPERTS * (MOE_ROWS - 1)) // MOE_ROWS)
    n_blocks = -(-n_blocks * MOE_ROWS // GATHER_ROWS) * GATHER_ROWS // MOE_ROWS
    row_tok = jnp.zeros((n_blocks * MOE_ROWS,), jnp.int32).at[dest].set(flat_tok, unique_indices=True)
    block_start = jnp.arange(n_blocks, dtype=jnp.int32) * MOE_ROWS
    block_e = jnp.minimum(jnp.sum((pad_end[None, :] <= block_start[:, None]).astype(jnp.int32), axis=1),
                          N_EXPERTS - 1)
    n_used = (pad_end[-1:] // MOE_ROWS).astype(jnp.int32)
    return row_tok, block_e, n_used, dest


def _mixers(x, mod, pos, p, *, prompt, caches=None, page_table=None):
    b, t, d = x.shape
    tokens = b * t
    mod3 = mod.reshape(b, 1, 6 * d)
    bb, tt = (1, 512) if prompt else (32, t)
    h = _normmod(x, p['g_norm1'], mod3, 1, 0, bb, tt)
    z = _mm(h, p['w_in'], min(tokens, 1024), Z_COLS // 5)
    lc = min(t, CHUNK)
    reps = CHUNK // lc
    ws_t = jnp.tile(p['w_s'][:, :lc, :lc], (1, reps, reps))
    bs_t = jnp.broadcast_to(jnp.tile(p['b_s'][:, :lc], (1, reps))[:, :, None], (A_GROUPS, CHUNK, CHUNK))
    gm = _gmlp(z, p['g_v_ln'], p['b_v_ln'], ws_t, bs_t, lc, want_vn=not prompt)
    a_out = gm[0]

    half = ROPE_DIM // 2
    inv_freq = jnp.exp(-math.log(ROPE_THETA) * jnp.arange(half, dtype=F32) / half)
    ang = pos.astype(F32)[:, None] * inv_freq[None, :]
    cos, sin = jnp.cos(ang), jnp.sin(ang)
    tab = jnp.concatenate([cos, cos, -sin, sin], axis=-1)
    tab = jnp.broadcast_to(tab[None], (b, t, LANES)).reshape(tokens, LANES)
    swap = lambda g: jnp.concatenate([g[half:], g[:half]])
    gains = [p['g_q_lat'].reshape(1, -1), p['g_kv_lat'].reshape(1, -1), p['g_q_nope'].reshape(1, -1),
             jnp.concatenate([p['g_q_rope'], swap(p['g_q_rope'])]).reshape(1, -1),
             jnp.concatenate([p['g_k_rope'], swap(p['g_k_rope'])]).reshape(1, -1),
             p['g_k_nope'].reshape(1, -1)]
    weights = [p['wq_n'], p['wq_r'], p['w_uk'], p['w_uv'].T if prompt else p['w_uv']]
    mla = _mla(z, tab, gains, weights, 256, prompt)
    c_kv, k_pe, rinv = mla[:3]
    if prompt:
        b_out = _attend_prompt(mla[3], mla[4], mla[5], min(t, 1024))
    else:
        cache_latent, cache_rope_t, cache_rinv_t = caches
        rows = N_HEADS * t
        qa = _absorb(mla[3], p['w_uk']).reshape(b, rows, KV_LORA)
        qr = mla[4].reshape(b, rows, LANES)
        pad = PAGE_SIZE - t
        nlat = jnp.pad(c_kv.reshape(b, t, KV_LORA), ((0, 0), (0, pad), (0, 0)))
        nrope_t = jnp.pad(jnp.swapaxes(k_pe.reshape(b, t, ROPE_DIM), 1, 2), ((0, 0), (0, 0), (0, pad)))
        nrinv_t = jnp.pad(jnp.swapaxes(rinv.reshape(b, t, N_HEADS), 1, 2), ((0, 0), (0, 0), (0, pad)))
        o_lat = _decode(page_table, qa, qr, nlat, nrope_t, nrinv_t, cache_latent, cache_rope_t, cache_rinv_t, t)
        b_out = _upv(o_lat.reshape(tokens, N_HEADS * KV_LORA), p['w_uv'])
    merged = _merge(a_out, b_out, p['w_proj_a'], p['w_proj_b'], z, min(tokens, 512), 1024)
    bb2, tt2 = (1, 256) if prompt else (32, t)
    x1, h2, top_idx, gates = _outproj(merged, p['w_out'], x, mod3, p['g_norm2'], p['wr_pad'], p['br_pad'], bb2, tt2)
    vn = None if prompt else gm[1]
    return (x1, h2, top_idx, gates, mod3), (c_kv, k_pe, rinv, vn)


def kernel(x_prompt, x_sample, c_prompt, c_sample, cache_latent, cache_k_rope, cache_k_rinv, page_table,
           w_ada, b_ada, g_norm1, w_in, g_v_ln, b_v_ln, w_s, b_s, g_q_lat, w_uq, g_q_nope, g_q_rope,
           g_kv_lat, g_k_rope, w_uk, w_uv, g_k_nope, w_proj_a, w_proj_b, w_out, g_norm2,
           w_router, b_router, w_gate_up, b_gate_up, w_down, b_down):
    depth = w_ada.shape[0]
    bp, sp, d = x_prompt.shape
    bs, ss, _ = x_sample.shape
    past_len = page_table.shape[1] * PAGE_SIZE
    pos_p = jnp.arange(sp, dtype=jnp.int32)
    pos_s = past_len + jnp.arange(ss, dtype=jnp.int32)
    half = ROPE_DIM // 2
    y_p, y_s = x_prompt, x_sample
    outs = [[] for _ in range(7)]
    for l in range(depth):
        wi = w_in[l]
        kr = wi[:, 2816:2880]
        w_in_perm = jnp.concatenate(
            [wi[:, 0:2048], wi[:, 2880:6976], wi[:, 2048:2816], kr, kr[:, half:], kr[:, :half]], axis=1).astype(BF16)
        wq = w_uq[l]
        wq_rope = wq[:, :, NOPE_DIM:]
        p = {
            'g_norm1': g_norm1[l], 'w_in': w_in_perm, 'g_v_ln': g_v_ln[l], 'b_v_ln': b_v_ln[l],
            'w_s': w_s[l], 'b_s': b_s[l], 'g_q_lat': g_q_lat[l], 'g_q_nope': g_q_nope[l], 'g_q_rope': g_q_rope[l],
            'g_kv_lat': g_kv_lat[l], 'g_k_rope': g_k_rope[l], 'g_k_nope': g_k_nope[l],
            'wq_n': wq[:, :, :NOPE_DIM].reshape(Q_LORA, -1).astype(BF16),
            'wq_r': jnp.concatenate([wq_rope, wq_rope[:, :, half:], wq_rope[:, :, :half]],
                                    axis=-1).reshape(Q_LORA, -1).astype(BF16),
            'w_uk': w_uk[l].reshape(KV_LORA, -1).astype(BF16), 'w_uv': w_uv[l].reshape(KV_LORA, -1).astype(BF16),
            'w_proj_a': w_proj_a[l].astype(BF16), 'w_proj_b': w_proj_b[l].astype(BF16),
            'w_out': w_out[l].astype(BF16), 'g_norm2': g_norm2[l],
            'wr_pad': jnp.pad(w_router[l], ((0, 0), (0, LANES - N_EXPERTS))).astype(BF16),
            'br_pad': jnp.pad(b_router[l], (0, LANES - N_EXPERTS), constant_values=NEG_INF).reshape(1, LANES),
        }
        n_c = bp + bs
        c_all = jnp.pad(jnp.concatenate([c_prompt, c_sample], axis=0), ((0, -n_c % 8), (0, 0)))
        mod = _ada(c_all, w_ada[l], b_ada[l])
        caches = (cache_latent[l], jnp.swapaxes(cache_k_rope[l], 1, 2), jnp.swapaxes(cache_k_rinv[l], 1, 2))
        (x1_p, h2_p, idx_p, gate_p, mod3_p), aux_p = _mixers(y_p, mod[:bp], pos_p, p, prompt=True)
        (x1_s, h2_s, idx_s, gate_s, mod3_s), aux_s = _mixers(y_s, mod[bp:n_c], pos_s, p, prompt=False,
                                                            caches=caches, page_table=page_table)
        h2 = jnp.concatenate([h2_p, h2_s], axis=0)
        row_tok, block_e, n_used, dest = _route(jnp.concatenate([idx_p, idx_s], axis=0))
        x_sorted = _dispatch(row_tok, h2)
        out_rows = _moe_experts(block_e, n_used, x_sorted, w_gate_up[l], b_gate_up[l], w_down[l], b_down[l])
        n_p = bp * sp * TOP_K
        y_p = _combine(dest[:n_p], x1_p, mod3_p, gate_p, out_rows, 1, COMBINE_TOKENS)
        y_s = _combine(dest[n_p:], x1_s, mod3_s, gate_s, out_rows, COMBINE_TOKENS // ss, ss)
        for o, a in zip(outs, (aux_p[0].reshape(bp, sp, KV_LORA), aux_p[1].reshape(bp, sp, ROPE_DIM),
                               aux_p[2].reshape(bp, sp, N_HEADS), aux_s[0].reshape(bs, ss, KV_LORA),
                               aux_s[1].reshape(bs, ss, ROPE_DIM), aux_s[2].reshape(bs, ss, N_HEADS),
                               aux_s[3].reshape(bs, ss, D_A))):
            o.append(a)
    return (y_p, y_s) + tuple(jnp.stack(o) for o in outs)
```

```python
import functools
import math

import jax
import jax.numpy as jnp
from jax import lax
from jax.experimental import pallas as pl
from jax.experimental.pallas import tpu as pltpu

F32 = jnp.float32
BF16 = jnp.bfloat16

D_MODEL = 2048
D_A = D_MODEL // 2
A_GROUP = 128
A_GROUPS = D_A // A_GROUP
CHUNK = 128
N_HEADS = 16
Q_LORA = D_MODEL // 4
KV_LORA = D_MODEL // 8
NOPE_DIM = 128
ROPE_DIM = 64
V_DIM = 128
QK_DIM = NOPE_DIM + ROPE_DIM
ROPE_THETA = 10000.0
SCALE = 1.0 / math.sqrt(QK_DIM)
LOG2_E = math.log2(math.e)
N_EXPERTS = 32
TOP_K = 4
D_FF = D_MODEL
SWIGLU_LIMIT = 7.0
SWIGLU_ALPHA = 1.702
EPS = 1e-6
NEG_INF = -1e30
PAGE_SIZE = 128

LANES = 128
MOE_ROWS = 512
GATHER_ROWS = 512
COMBINE_TOKENS = 128
PAGES_PER_STEP = 64
DECODE_CHAINS = 4
VMEM_LIMIT = 56 * 1024 * 1024

ZO_U = 0
ZO_V = ZO_U + D_A
ZO_GA = ZO_V + D_A
ZO_GB = ZO_GA + D_MODEL
ZO_Q = ZO_GB + D_MODEL
ZO_KV = ZO_Q + Q_LORA
ZO_KR = ZO_KV + KV_LORA
Z_COLS = ZO_KR + 2 * ROPE_DIM


def _params(sem):
    return pltpu.CompilerParams(dimension_semantics=sem, vmem_limit_bytes=VMEM_LIMIT)


def _dot(a, b):
    return jnp.dot(a, b, preferred_element_type=F32)


def _dot_nt(a, b):
    return lax.dot_general(a, b, (((1,), (1,)), ((), ())), preferred_element_type=F32)


def _sigmoid(x):
    return 1.0 / (1.0 + jnp.exp(-x))


def _gelu(x):
    c = math.sqrt(2.0 / math.pi)
    return 0.5 * x * (1.0 + jnp.tanh(c * (x + 0.044715 * (x * x * x))))


def _rms(x):
    return x * lax.rsqrt(jnp.mean(x * x, axis=-1, keepdims=True) + EPS)


def _ada_kernel(c_ref, w_ref, b_ref, o_ref):
    c = c_ref[...]
    a = (c * _sigmoid(c)).astype(BF16)
    o_ref[...] = _dot(a, w_ref[...].astype(BF16)) + b_ref[...]


def _ada(c_all, w_ada, b_ada):
    rows, d = c_all.shape
    n = w_ada.shape[1]
    tn = 1024
    return pl.pallas_call(
        _ada_kernel,
        out_shape=jax.ShapeDtypeStruct((rows, n), F32),
        grid=(n // tn,),
        in_specs=[pl.BlockSpec((rows, d), lambda j: (0, 0)),
                  pl.BlockSpec((d, tn), lambda j: (0, j)),
                  pl.BlockSpec((1, tn), lambda j: (0, j))],
        out_specs=pl.BlockSpec((rows, tn), lambda j: (0, j)),
        compiler_params=_params(("arbitrary",)),
    )(c_all, w_ada, b_ada.reshape(1, n))


def _normmod_kernel(x_ref, g_ref, sc_ref, sh_ref, o_ref):
    x = x_ref[...]
    y = _rms(x) * g_ref[...]
    y = y * (1.0 + sc_ref[...]) + sh_ref[...]
    o_ref[...] = y.reshape(o_ref.shape).astype(o_ref.dtype)


def _normmod(x, g, mod3, sc_idx, sh_idx, bb, tt):
    b, t, d = x.shape
    return pl.pallas_call(
        _normmod_kernel,
        out_shape=jax.ShapeDtypeStruct((b * t, d), BF16),
        grid=(b // bb, t // tt),
        in_specs=[pl.BlockSpec((bb, tt, d), lambda i, j: (i, j, 0)),
                  pl.BlockSpec((1, 1, d), lambda i, j: (0, 0, 0)),
                  pl.BlockSpec((bb, 1, d), lambda i, j: (i, 0, sc_idx)),
                  pl.BlockSpec((bb, 1, d), lambda i, j: (i, 0, sh_idx))],
        out_specs=pl.BlockSpec((bb * tt, d), lambda i, j: (i * (t // tt) + j, 0)),
        compiler_params=_params(("arbitrary", "arbitrary")),
    )(x, g.reshape(1, 1, d), mod3, mod3)


def _mm_kernel(x_ref, w_ref, o_ref):
    o_ref[...] = _dot(x_ref[...], w_ref[...]).astype(o_ref.dtype)


def _mm(x, w, tm, tn, out_dtype=F32):
    m, k = x.shape
    n = w.shape[1]
    return pl.pallas_call(
        _mm_kernel,
        out_shape=jax.ShapeDtypeStruct((m, n), out_dtype),
        grid=(m // tm, n // tn),
        in_specs=[pl.BlockSpec((tm, k), lambda i, j: (i, 0)),
                  pl.BlockSpec((k, tn), lambda i, j: (0, j))],
        out_specs=pl.BlockSpec((tm, tn), lambda i, j: (i, j)),
        compiler_params=_params(("arbitrary", "arbitrary")),
    )(x, w)


def _gmlp_kernel(zu_ref, zv_ref, gln_ref, bln_ref, ws_ref, bs_ref, a_ref, *maybe_vn_ref, lc):
    u = _gelu(zu_ref[...])
    gv = _gelu(zv_ref[...])
    xc = gv - jnp.mean(gv, axis=-1, keepdims=True)
    vn = xc * lax.rsqrt(jnp.mean(xc * xc, axis=-1, keepdims=True) + EPS)
    vn = vn * gln_ref[...] + bln_ref[...]
    if maybe_vn_ref:
        maybe_vn_ref[0][...] = vn
    row = lax.broadcasted_iota(jnp.int32, (CHUNK, CHUNK), 0)
    col = lax.broadcasted_iota(jnp.int32, (CHUNK, CHUNK), 1)
    mask = (col <= row) & ((row // lc) == (col // lc))
    vb = vn.astype(BF16)
    for g in range(A_GROUPS):
        w = jnp.where(mask, ws_ref[g], 0.0).astype(BF16)
        sl = slice(g * A_GROUP, (g + 1) * A_GROUP)
        s = _dot(w, vb[:, sl]) + bs_ref[g]
        a_ref[:, sl] = (u[:, sl] * s).astype(a_ref.dtype)


def _gmlp(z, g_ln, b_ln, ws_t, bs_t, lc, want_vn):
    t = z.shape[0]
    out_shape = [jax.ShapeDtypeStruct((t, D_A), BF16)]
    out_specs = [pl.BlockSpec((CHUNK, D_A), lambda i: (i, 0))]
    if want_vn:
        out_shape.append(jax.ShapeDtypeStruct((t, D_A), F32))
        out_specs.append(pl.BlockSpec((CHUNK, D_A), lambda i: (i, 0)))
    return pl.pallas_call(
        functools.partial(_gmlp_kernel, lc=lc),
        out_shape=out_shape,
        grid=(t // CHUNK,),
        in_specs=[pl.BlockSpec((CHUNK, D_A), lambda i: (i, ZO_U // D_A)),
                  pl.BlockSpec((CHUNK, D_A), lambda i: (i, ZO_V // D_A)),
                  pl.BlockSpec((1, D_A), lambda i: (0, 0)),
                  pl.BlockSpec((1, D_A), lambda i: (0, 0)),
                  pl.BlockSpec((A_GROUPS, CHUNK, CHUNK), lambda i: (0, 0, 0)),
                  pl.BlockSpec((A_GROUPS, CHUNK, CHUNK), lambda i: (0, 0, 0))],
        out_specs=out_specs,
        compiler_params=_params(("arbitrary",)),
    )(z, z, g_ln.reshape(1, D_A), b_ln.reshape(1, D_A), ws_t, bs_t)


def _rope_pair(y2, gain2, tab):
    p = y2 * gain2 * tab
    return p + pltpu.roll(p, ROPE_DIM, 1)


def _mla_kernel(zq_ref, zkv_ref, zkr_ref, tab_ref, gql_ref, gkv_ref, gqn_ref, gq2_ref, gk2_ref, gkn_ref,
                wqn_ref, wqr_ref, wuk_ref, wuv_ref, ckv_ref, kpe_ref, rinv_ref, *qkv_refs, prompt):
    tab = tab_ref[...]
    q_scale = SCALE * LOG2_E if prompt else SCALE
    q_lat = (_rms(zq_ref[...]) * gql_ref[...]).astype(BF16)
    c_kv = _rms(zkv_ref[...]) * gkv_ref[...]
    ckv_ref[...] = c_kv
    kr2 = zkr_ref[...]
    kr2 = kr2 * lax.rsqrt(jnp.sum(kr2 * kr2, axis=-1, keepdims=True) / (2 * ROPE_DIM) + EPS)
    kpe2 = _rope_pair(kr2, gk2_ref[...], tab)
    kpe_ref[...] = kpe2[:, :ROPE_DIM]
    cb = c_kv.astype(BF16)
    tm = cb.shape[0]
    lane = lax.broadcasted_iota(jnp.int32, (tm, N_HEADS), 1)
    rinv_all = jnp.zeros((tm, N_HEADS), F32)
    if prompt:
        col = lax.broadcasted_iota(jnp.int32, kpe2.shape, 1)
        kp_b = jnp.where(col < ROPE_DIM, kpe2, 0.0).astype(BF16)
    for h in range(N_HEADS):
        sl = slice(h * NOPE_DIM, (h + 1) * NOPE_DIM)
        qn = _rms(_dot(q_lat, wqn_ref[:, sl])) * gqn_ref[...]
        qr2 = _dot(q_lat, wqr_ref[:, sl])
        qr2 = qr2 * lax.rsqrt(jnp.sum(qr2 * qr2, axis=-1, keepdims=True) / (2 * ROPE_DIM) + EPS)
        qp = (_rope_pair(qr2, gq2_ref[...], tab) * q_scale).astype(BF16)
        k_raw = _dot(cb, wuk_ref[:, sl])
        rinv = lax.rsqrt(jnp.mean(k_raw * k_raw, axis=-1, keepdims=True) + EPS)
        rinv_all = jnp.where(lane == h, rinv, rinv_all)
        if prompt:
            q_ref, k_ref, v_ref = qkv_refs
            lo = slice(2 * h * NOPE_DIM, (2 * h + 1) * NOPE_DIM)
            hi = slice((2 * h + 1) * NOPE_DIM, (2 * h + 2) * NOPE_DIM)
            q_ref[:, lo] = (qn * q_scale).astype(BF16)
            q_ref[:, hi] = qp
            k_ref[:, lo] = (k_raw * rinv * gkn_ref[...]).astype(BF16)
            k_ref[:, hi] = kp_b
            v_ref[sl, :] = _dot_nt(wuv_ref[sl, :], cb).astype(BF16)
        else:
            qn_ref, qp_ref = qkv_refs
            qn_ref[:, sl] = (qn * gkn_ref[...]).astype(BF16)
            qp_ref[:, sl] = qp
    rinv_ref[...] = rinv_all


def _mla(z, tab, gains, weights, tm, prompt):
    t = z.shape[0]
    hd = N_HEADS * NOPE_DIM
    row = lambda w: pl.BlockSpec((tm, w), lambda i: (i, 0))
    full = lambda a: pl.BlockSpec(a.shape, lambda i: (0,) * a.ndim)
    out_shape = [jax.ShapeDtypeStruct((t, KV_LORA), F32), jax.ShapeDtypeStruct((t, ROPE_DIM), F32),
                 jax.ShapeDtypeStruct((t, N_HEADS), F32)]
    out_specs = [row(KV_LORA), row(ROPE_DIM), row(N_HEADS)]
    widths = (2 * hd, 2 * hd) if prompt else (hd, hd)
    out_shape += [jax.ShapeDtypeStruct((t, w), BF16) for w in widths]
    out_specs += [row(w) for w in widths]
    if prompt:
        out_shape.append(jax.ShapeDtypeStruct((hd, t), BF16))
        out_specs.append(pl.BlockSpec((hd, tm), lambda i: (0, i)))
    return pl.pallas_call(
        functools.partial(_mla_kernel, prompt=prompt),
        out_shape=out_shape,
        grid=(t // tm,),
        in_specs=[pl.BlockSpec((tm, Q_LORA), lambda i: (i, ZO_Q // Q_LORA)),
                  pl.BlockSpec((tm, KV_LORA), lambda i: (i, ZO_KV // KV_LORA)),
                  pl.BlockSpec((tm, LANES), lambda i: (i, ZO_KR // LANES)),
                  row(LANES)] + [full(a) for a in gains] + [full(a) for a in weights],
        out_specs=out_specs,
        compiler_params=_params(("arbitrary",)),
    )(z, z, z, tab, *gains, *weights)


def _fa_kernel(q_ref, k_ref, vt_ref, o_ref, sa_ref, sb_ref, m_ref, l_ref, acc_ref, *, tq, tk):
    i = pl.program_id(1)
    q = q_ref[...]
    m_ref[...] = jnp.full(m_ref.shape, NEG_INF, F32)
    l_ref[...] = jnp.zeros(l_ref.shape, F32)
    acc_ref[...] = jnp.zeros(acc_ref.shape, F32)

    def keys(j):
        return pl.ds(pl.multiple_of(j * tk, tk), tk)

    def qk(j):
        return _dot_nt(k_ref[keys(j), :], q)

    def update(s, j):
        m_old = m_ref[...]
        m_new = jnp.maximum(m_old, jnp.max(s, axis=0, keepdims=True))
        corr = jnp.exp2(m_old - m_new)
        p = jnp.exp2(s - m_new)
        l_ref[...] = l_ref[...] * corr + jnp.sum(p, axis=0, keepdims=True)
        m_ref[...] = m_new
        acc_ref[...] = acc_ref[...] * corr + _dot(vt_ref[:, keys(j)], p.astype(BF16))

    sa_ref[...] = qk(0)

    def pair(u, carry):
        sb_ref[...] = qk(2 * u + 1)
        update(sa_ref[...], 2 * u)
        sa_ref[...] = qk(2 * u + 2)
        update(sb_ref[...], 2 * u + 1)
        return carry

    lax.fori_loop(0, i, pair, 0)
    sb_ref[...] = qk(2 * i + 1)
    key = lax.broadcasted_iota(jnp.int32, (tk, tq), 0)
    qry = lax.broadcasted_iota(jnp.int32, (tk, tq), 1)
    update(jnp.where(key <= qry, sa_ref[...], NEG_INF), 2 * i)
    update(jnp.where(key + tk <= qry, sb_ref[...], NEG_INF), 2 * i + 1)
    o_ref[...] = (acc_ref[...] / l_ref[...]).T.astype(o_ref.dtype)


def _attend_prompt(q, k, vt, tq):
    t = q.shape[0]
    tk = tq // 2
    return pl.pallas_call(
        functools.partial(_fa_kernel, tq=tq, tk=tk),
        out_shape=jax.ShapeDtypeStruct((t, N_HEADS * V_DIM), BF16),
        grid=(N_HEADS, t // tq),
        in_specs=[pl.BlockSpec((tq, 2 * NOPE_DIM), lambda h, i: (i, h)),
                  pl.BlockSpec((t, 2 * NOPE_DIM), lambda h, i: (0, h)),
                  pl.BlockSpec((V_DIM, t), lambda h, i: (h, 0))],
        out_specs=pl.BlockSpec((tq, V_DIM), lambda h, i: (i, h)),
        scratch_shapes=[pltpu.VMEM((tk, tq), F32), pltpu.VMEM((tk, tq), F32),
                        pltpu.VMEM((1, tq), F32), pltpu.VMEM((1, tq), F32), pltpu.VMEM((V_DIM, tq), F32)],
        compiler_params=_params(("arbitrary", "arbitrary")),
    )(q, k, vt)


def _absorb_kernel(qn_ref, wuk_ref, qa_ref):
    qa_ref[...] = (_dot_nt(qn_ref[...], wuk_ref[...]) * SCALE).astype(qa_ref.dtype)


def _absorb(qn, w_uk_b):
    tokens = qn.shape[0]
    return pl.pallas_call(
        _absorb_kernel,
        out_shape=jax.ShapeDtypeStruct((tokens, N_HEADS * KV_LORA), BF16),
        grid=(N_HEADS,),
        in_specs=[pl.BlockSpec((tokens, NOPE_DIM), lambda h: (0, h)),
                  pl.BlockSpec((KV_LORA, NOPE_DIM), lambda h: (0, h))],
        out_specs=pl.BlockSpec((tokens, KV_LORA), lambda h: (0, h)),
        compiler_params=_params(("arbitrary",)),
    )(qn, w_uk_b)


def _decode_kernel(pt_ref, qa_ref, qr_ref, nlat_ref, nrope_ref, nrinv_ref, lat_hbm, rope_hbm, rinv_hbm, o_ref,
                   lat_buf, rope_buf, rinv_buf, sems, ck_s, kr_s, ri_s, m_ref, l_ref, acc_ref, *, t_new):
    pc = PAGES_PER_STEP
    b, c = pl.program_id(0), pl.program_id(1)
    nb, nc = pl.num_programs(0), pl.num_programs(1)
    n = b * nc + c
    slot = n % 2
    rows = N_HEADS * t_new

    def page_copies(bi, ci, sl):
        copies = []
        for j in range(pc):
            page = pt_ref[bi, ci * pc + j]
            copies.append(pltpu.make_async_copy(lat_hbm.at[page], lat_buf.at[sl, j], sems.at[sl, 0]))
            copies.append(pltpu.make_async_copy(rope_hbm.at[page], rope_buf.at[sl, j], sems.at[sl, 1]))
            copies.append(pltpu.make_async_copy(rinv_hbm.at[page], rinv_buf.at[sl, j], sems.at[sl, 2]))
        return copies

    @pl.when(n == 0)
    def _():
        for cp in page_copies(b, c, slot):
            cp.start()

    @pl.when(n + 1 < nb * nc)
    def _():
        wrap = c + 1 == nc
        for cp in page_copies(jnp.where(wrap, b + 1, b), jnp.where(wrap, 0, c + 1), 1 - slot):
            cp.start()

    for cp in page_copies(b, c, slot):
        cp.wait()
    qa = qa_ref[...]
    qr = qr_ref[...][:, :ROPE_DIM]

    def scores(ck, krt, rit):
        n = ck.shape[0]
        return (_dot_nt(qa, ck).reshape(t_new, N_HEADS, n) * rit[None]).reshape(rows, n) + _dot(qr, krt)

    def fold(g, s, ck):
        m_old = m_ref[g]
        m_new = jnp.maximum(m_old, jnp.max(s, axis=-1, keepdims=True))
        corr = jnp.exp(m_old - m_new)
        p = jnp.exp(s - m_new)
        l_ref[g] = l_ref[g] * corr + jnp.sum(p, axis=-1, keepdims=True)
        m_ref[g] = m_new
        acc_ref[g] = acc_ref[g] * corr + _dot(p.astype(BF16), ck)

    @pl.when(c == 0)
    def _():
        m_ref[...] = jnp.full(m_ref.shape, NEG_INF, F32)
        l_ref[...] = jnp.zeros(l_ref.shape, F32)
        acc_ref[...] = jnp.zeros(acc_ref.shape, F32)
        ck = nlat_ref[...].astype(BF16)
        s = scores(ck, nrope_ref[...].astype(BF16), nrinv_ref[...])
        row = lax.broadcasted_iota(jnp.int32, s.shape, 0)
        col = lax.broadcasted_iota(jnp.int32, s.shape, 1)
        fold(0, jnp.where(col <= row // N_HEADS, s, NEG_INF), ck)

    for j in range(pc):
        ck_s[j * PAGE_SIZE:(j + 1) * PAGE_SIZE, :] = lat_buf[slot, j].astype(BF16)
        kr_s[:, j * PAGE_SIZE:(j + 1) * PAGE_SIZE] = rope_buf[slot, j].astype(BF16)
        ri_s[:, j * PAGE_SIZE:(j + 1) * PAGE_SIZE] = rinv_buf[slot, j]
    keys = pc * PAGE_SIZE // DECODE_CHAINS
    cks = [ck_s[g * keys:(g + 1) * keys, :] for g in range(DECODE_CHAINS)]
    ss = [scores(cks[g], kr_s[:, g * keys:(g + 1) * keys], ri_s[:, g * keys:(g + 1) * keys])
          for g in range(DECODE_CHAINS)]
    for g in range(DECODE_CHAINS):
        fold(g, ss[g], cks[g])

    @pl.when(c == nc - 1)
    def _():
        m = m_ref[0]
        for g in range(1, DECODE_CHAINS):
            m = jnp.maximum(m, m_ref[g])
        l = jnp.zeros(m.shape, F32)
        acc = jnp.zeros(acc_ref.shape[1:], F32)
        for g in range(DECODE_CHAINS):
            w = jnp.exp(m_ref[g] - m)
            l = l + l_ref[g] * w
            acc = acc + acc_ref[g] * w
        o_ref[...] = (acc / l).astype(o_ref.dtype)


def _decode(page_table, qa, qr, nlat, nrope_t, nrinv_t, cache_latent, cache_rope_t, cache_rinv_t, t_new):
    b, n_pages = page_table.shape
    pc = PAGES_PER_STEP
    rows = N_HEADS * t_new
    seq = lambda r, w: pl.BlockSpec((None, r, w), lambda i, c, pt: (i, 0, 0))
    hbm = pl.BlockSpec(memory_space=pl.ANY)
    keys = pc * PAGE_SIZE
    grid_spec = pltpu.PrefetchScalarGridSpec(
        num_scalar_prefetch=1,
        grid=(b, n_pages // pc),
        in_specs=[seq(rows, KV_LORA), seq(rows, LANES),
                  seq(PAGE_SIZE, KV_LORA), seq(ROPE_DIM, PAGE_SIZE), seq(N_HEADS, PAGE_SIZE), hbm, hbm, hbm],
        out_specs=seq(rows, KV_LORA),
        scratch_shapes=[pltpu.VMEM((2, pc, PAGE_SIZE, KV_LORA), F32), pltpu.VMEM((2, pc, ROPE_DIM, PAGE_SIZE), F32),
                        pltpu.VMEM((2, pc, N_HEADS, PAGE_SIZE), F32), pltpu.SemaphoreType.DMA((2, 3)),
                        pltpu.VMEM((keys, KV_LORA), BF16), pltpu.VMEM((ROPE_DIM, keys), BF16),
                        pltpu.VMEM((N_HEADS, keys), F32),
                        pltpu.VMEM((DECODE_CHAINS, rows, 1), F32), pltpu.VMEM((DECODE_CHAINS, rows, 1), F32),
                        pltpu.VMEM((DECODE_CHAINS, rows, KV_LORA), F32)],
    )
    return pl.pallas_call(
        functools.partial(_decode_kernel, t_new=t_new),
        out_shape=jax.ShapeDtypeStruct((b, rows, KV_LORA), F32),
        grid_spec=grid_spec,
        compiler_params=_params(("arbitrary", "arbitrary")),
    )(page_table, qa, qr, nlat, nrope_t, nrinv_t, cache_latent, cache_rope_t, cache_rinv_t)


def _upv_kernel(o_ref, wuv_ref, out_ref):
    out_ref[...] = _dot(o_ref[...].astype(BF16), wuv_ref[...]).astype(out_ref.dtype)


def _upv(o_lat, w_uv_b):
    tokens = o_lat.shape[0]
    return pl.pallas_call(
        _upv_kernel,
        out_shape=jax.ShapeDtypeStruct((tokens, N_HEADS * V_DIM), BF16),
        grid=(N_HEADS,),
        in_specs=[pl.BlockSpec((tokens, KV_LORA), lambda h: (0, h)),
                  pl.BlockSpec((KV_LORA, V_DIM), lambda h: (0, h))],
        out_specs=pl.BlockSpec((tokens, V_DIM), lambda h: (0, h)),
        compiler_params=_params(("arbitrary",)),
    )(o_lat, w_uv_b)


def _merge_kernel(a_ref, b_ref, wa_ref, wb_ref, ga_ref, gb_ref, o_ref):
    pa = _dot(a_ref[...], wa_ref[...])
    pb = _dot(b_ref[...], wb_ref[...])
    o_ref[...] = (_sigmoid(ga_ref[...]) * pa + _sigmoid(gb_ref[...]) * pb).astype(o_ref.dtype)


def _merge(a_out, b_out, wpa, wpb, z, tm, tn):
    t = a_out.shape[0]
    return pl.pallas_call(
        _merge_kernel,
        out_shape=jax.ShapeDtypeStruct((t, D_MODEL), BF16),
        grid=(t // tm, D_MODEL // tn),
        in_specs=[pl.BlockSpec((tm, D_A), lambda i, j: (i, 0)),
                  pl.BlockSpec((tm, D_MODEL), lambda i, j: (i, 0)),
                  pl.BlockSpec((D_A, tn), lambda i, j: (0, j)),
                  pl.BlockSpec((D_MODEL, tn), lambda i, j: (0, j)),
                  pl.BlockSpec((tm, tn), lambda i, j: (i, ZO_GA // tn + j)),
                  pl.BlockSpec((tm, tn), lambda i, j: (i, ZO_GB // tn + j))],
        out_specs=pl.BlockSpec((tm, tn), lambda i, j: (i, j)),
        compiler_params=_params(("arbitrary", "arbitrary")),
    )(a_out, b_out, wpa, wpb, z, z)


def _outproj_kernel(m_ref, wo_ref, x_ref, g1_ref, gn_ref, sc_ref, sh_ref, wr_ref, br_ref,
                    x1_ref, h2_ref, idx_ref, gate_ref):
    shape3 = x_ref.shape
    y = _dot(m_ref[...], wo_ref[...])
    x1 = x_ref[...] + g1_ref[...] * y.reshape(shape3)
    x1_ref[...] = x1
    h2 = _rms(x1) * gn_ref[...]
    h2 = (h2 * (1.0 + sc_ref[...]) + sh_ref[...]).reshape(y.shape)
    h2_ref[...] = h2
    logits = _dot(h2.astype(BF16), wr_ref[...]) + br_ref[...]
    lane = lax.broadcasted_iota(jnp.int32, logits.shape, 1).astype(F32)
    vals, idxs = [], []
    for _ in range(TOP_K):
        mx = jnp.max(logits, axis=-1, keepdims=True)
        am = jnp.min(jnp.where(logits == mx, lane, float(LANES)), axis=-1, keepdims=True)
        vals.append(mx)
        idxs.append(am)
        logits = jnp.where(lane == am, -3.0e38, logits)
    es = [jnp.exp(v - vals[0]) for v in vals]
    den = es[0] + es[1] + es[2] + es[3]
    idx_out = jnp.zeros(lane.shape, F32)
    gate_out = jnp.zeros(lane.shape, F32)
    for k in range(TOP_K):
        idx_out = jnp.where(lane == float(k), idxs[k], idx_out)
        gate_out = jnp.where(lane == float(k), es[k] / den, gate_out)
    idx_ref[...] = idx_out.astype(jnp.int32)
    gate_ref[...] = gate_out


def _outproj(merged, w_out_b, x, mod3, g_norm2, wr_pad, br_pad, bb, tt):
    b, t, d = x.shape
    tm = bb * tt
    nt = t // tt
    tok = lambda w: pl.BlockSpec((tm, w), lambda i, j: (i * nt + j, 0))
    modspec = lambda k: pl.BlockSpec((bb, 1, d), lambda i, j: (i, 0, k))
    return pl.pallas_call(
        _outproj_kernel,
        out_shape=[jax.ShapeDtypeStruct((b, t, d), F32), jax.ShapeDtypeStruct((b * t, d), F32),
                   jax.ShapeDtypeStruct((b * t, LANES), jnp.int32), jax.ShapeDtypeStruct((b * t, LANES), F32)],
        grid=(b // bb, nt),
        in_specs=[tok(d),
                  pl.BlockSpec((d, d), lambda i, j: (0, 0)),
                  pl.BlockSpec((bb, tt, d), lambda i, j: (i, j, 0)),
                  modspec(2),
                  pl.BlockSpec((1, 1, d), lambda i, j: (0, 0, 0)),
                  modspec(4), modspec(3),
                  pl.BlockSpec((d, LANES), lambda i, j: (0, 0)),
                  pl.BlockSpec((1, LANES), lambda i, j: (0, 0))],
        out_specs=[pl.BlockSpec((bb, tt, d), lambda i, j: (i, j, 0)), tok(d), tok(LANES), tok(LANES)],
        compiler_params=_params(("arbitrary", "arbitrary")),
    )(merged, w_out_b, x, mod3, g_norm2.reshape(1, 1, d), mod3, mod3, wr_pad, br_pad)


def _row_copy(src_ref, dst_ref, src_row, dst_row, sem):
    return pltpu.make_async_copy(src_ref.at[pl.ds(src_row, 1)], dst_ref.at[pl.ds(dst_row, 1)], sem)


def _dispatch_kernel(tok_ref, used_ref, h_ref, o_ref, buf, sem):
    n = buf.shape[0]
    base = pl.program_id(0) * n
    live = base < used_ref[0] * MOE_ROWS

    @pl.when(live)
    def _():
        def start(r, carry):
            _row_copy(h_ref, buf, tok_ref[base + r], r, sem).start()
            return carry

        lax.fori_loop(0, n, start, 0, unroll=8)
        pltpu.make_async_copy(h_ref.at[pl.ds(0, n)], buf, sem).wait()
        o_ref[...] = buf[...].astype(o_ref.dtype)

    @pl.when(jnp.logical_not(live))
    def _():
        o_ref[...] = jnp.zeros(o_ref.shape, o_ref.dtype)


def _dispatch(row_tok, n_used, h2):
    n_rows = row_tok.shape[0]
    d = h2.shape[1]
    return pl.pallas_call(
        _dispatch_kernel,
        out_shape=jax.ShapeDtypeStruct((n_rows, d), BF16),
        grid_spec=pltpu.PrefetchScalarGridSpec(
            num_scalar_prefetch=2,
            grid=(n_rows // GATHER_ROWS,),
            in_specs=[pl.BlockSpec(memory_space=pl.ANY)],
            out_specs=pl.BlockSpec((GATHER_ROWS, d), lambda i, tok, nu: (i, 0)),
            scratch_shapes=[pltpu.VMEM((GATHER_ROWS, d), F32), pltpu.SemaphoreType.DMA(())],
        ),
        compiler_params=_params(("arbitrary",)),
    )(row_tok, n_used, h2)


def _moe_up_kernel(be_ref, nxt_ref, used_ref, x_ref, bg_ref, bl_ref, w_hbm, o_ref, wbuf, sems, slot_ref, wg_s, wl_s, *, tf, nf):
    j, i = pl.program_id(0), pl.program_id(1)
    e = be_ref[i]

    def copies(ee, jj, sl):
        return [pltpu.make_async_copy(w_hbm.at[ee, :, pl.ds(pl.multiple_of((g * nf + jj) * tf, tf), tf)],
                                      wbuf.at[sl, g], sems.at[sl, g]) for g in range(2)]

    @pl.when((j == 0) & (i == 0))
    def _():
        slot_ref[0] = 0
        for cp in copies(e, j, 0):
            cp.start()

    @pl.when((i == 0) | (e != be_ref[jnp.maximum(i - 1, 0)]))
    def _():
        sl = slot_ref[0]
        for cp in copies(e, j, sl):
            cp.wait()
        nx = nxt_ref[i]
        more = nx >= 0

        @pl.when(more | (j + 1 < nf))
        def _():
            for cp in copies(jnp.where(more, nx, be_ref[0]), jnp.where(more, j, j + 1), 1 - sl):
                cp.start()

        wg_s[...] = wbuf[sl, 0].astype(BF16)
        wl_s[...] = wbuf[sl, 1].astype(BF16)
        slot_ref[0] = 1 - sl

    @pl.when(i < used_ref[0])
    def _():
        x = x_ref[...]
        glu = jnp.minimum(_dot(x, wg_s[...]) + bg_ref[...], SWIGLU_LIMIT)
        lin = jnp.clip(_dot(x, wl_s[...]) + bl_ref[...], -SWIGLU_LIMIT, SWIGLU_LIMIT)
        o_ref[...] = (glu * _sigmoid(SWIGLU_ALPHA * glu) * (lin + 1.0)).astype(o_ref.dtype)

    @pl.when(i >= used_ref[0])
    def _():
        o_ref[...] = jnp.zeros(o_ref.shape, o_ref.dtype)


def _moe_down_kernel(be_ref, nxt_ref, used_ref, a_ref, b_ref, w_hbm, o_ref, wbuf, sems, slot_ref, w_s):
    i = pl.program_id(0)
    e = be_ref[i]

    def copy(ee, sl):
        return pltpu.make_async_copy(w_hbm.at[ee], wbuf.at[sl], sems.at[sl])

    @pl.when(i == 0)
    def _():
        slot_ref[0] = 0
        copy(e, 0).start()

    @pl.when((i == 0) | (e != be_ref[jnp.maximum(i - 1, 0)]))
    def _():
        sl = slot_ref[0]
        copy(e, sl).wait()
        nx = nxt_ref[i]

        @pl.when(nx >= 0)
        def _():
            copy(nx, 1 - sl).start()

        w_s[...] = wbuf[sl].astype(BF16)
        slot_ref[0] = 1 - sl

    @pl.when(i < used_ref[0])
    def _():
        o_ref[...] = _dot(a_ref[...], w_s[...]) + b_ref[...]

    @pl.when(i >= used_ref[0])
    def _():
        o_ref[...] = jnp.zeros(o_ref.shape, o_ref.dtype)


def _moe_experts(block_e, n_used, x_sorted, w_gate_up, b_gate_up, w_down, b_down):
    n_rows, d = x_sorted.shape
    n_blocks = n_rows // MOE_ROWS
    tf = 1024
    nf = D_FF // tf
    ids = jnp.arange(N_EXPERTS, dtype=jnp.int32)
    present = (block_e[None, :] == ids[:, None]).any(axis=1)
    later = jnp.where(present[None, :] & (ids[None, :] > ids[:, None]), ids[None, :], N_EXPERTS).min(axis=1)
    nxt = jnp.where(later == N_EXPERTS, -1, later).astype(jnp.int32)[block_e]
    bgu3 = b_gate_up.reshape(N_EXPERTS, 1, 2 * D_FF)
    hbm = pl.BlockSpec(memory_space=pl.ANY)
    act = pl.pallas_call(
        functools.partial(_moe_up_kernel, tf=tf, nf=nf),
        out_shape=jax.ShapeDtypeStruct((n_rows, D_FF), BF16),
        grid_spec=pltpu.PrefetchScalarGridSpec(
            num_scalar_prefetch=3,
            grid=(nf, n_blocks),
            in_specs=[pl.BlockSpec((MOE_ROWS, d), lambda j, i, be, nx, nu: (i, 0)),
                      pl.BlockSpec((None, 1, tf), lambda j, i, be, nx, nu: (be[i], 0, j)),
                      pl.BlockSpec((None, 1, tf), lambda j, i, be, nx, nu: (be[i], 0, nf + j)),
                      hbm],
            out_specs=pl.BlockSpec((MOE_ROWS, tf), lambda j, i, be, nx, nu: (i, j)),
            scratch_shapes=[pltpu.VMEM((2, 2, d, tf), F32), pltpu.SemaphoreType.DMA((2, 2)),
                            pltpu.SMEM((1,), jnp.int32), pltpu.VMEM((d, tf), BF16), pltpu.VMEM((d, tf), BF16)],
        ),
        compiler_params=_params(("arbitrary", "arbitrary")),
    )(block_e, nxt, n_used, x_sorted, bgu3, bgu3, w_gate_up)
    return pl.pallas_call(
        _moe_down_kernel,
        out_shape=jax.ShapeDtypeStruct((n_rows, d), F32),
        grid_spec=pltpu.PrefetchScalarGridSpec(
            num_scalar_prefetch=3,
            grid=(n_blocks,),
            in_specs=[pl.BlockSpec((MOE_ROWS, D_FF), lambda i, be, nx, nu: (i, 0)),
                      pl.BlockSpec((None, 1, d), lambda i, be, nx, nu: (be[i], 0, 0)),
                      hbm],
            out_specs=pl.BlockSpec((MOE_ROWS, d), lambda i, be, nx, nu: (i, 0)),
            scratch_shapes=[pltpu.VMEM((2, D_FF, d), F32), pltpu.SemaphoreType.DMA((2,)),
                            pltpu.SMEM((1,), jnp.int32), pltpu.VMEM((D_FF, d), BF16)],
        ),
        compiler_params=_params(("arbitrary",)),
    )(block_e, nxt, n_used, act, b_down.reshape(N_EXPERTS, 1, d), w_down)


def _combine_kernel(dest_ref, x1_ref, g2_ref, gate_ref, rows_ref, o_ref, buf, sem):
    tm = buf.shape[1]
    nt = pl.num_programs(1)
    base = (pl.program_id(0) * nt + pl.program_id(1)) * (tm * TOP_K)

    def start(t, carry):
        for k in range(TOP_K):
            _row_copy(rows_ref, buf.at[k], dest_ref[base + t * TOP_K + k], t, sem).start()
        return carry

    lax.fori_loop(0, tm, start, 0, unroll=2)
    for k in range(TOP_K):
        pltpu.make_async_copy(rows_ref.at[pl.ds(0, tm)], buf.at[k], sem).wait()
    gates = gate_ref[...]
    ff = buf[0] * gates[:, 0:1]
    for k in range(1, TOP_K):
        ff = ff + buf[k] * gates[:, k:k + 1]
    o_ref[...] = x1_ref[...] + g2_ref[...] * ff.reshape(x1_ref.shape)


def _combine(dest, x1, mod3, gates, out_rows, bb, tt):
    b, t, d = x1.shape
    nt = t // tt
    tm = bb * tt
    return pl.pallas_call(
        _combine_kernel,
        out_shape=jax.ShapeDtypeStruct((b, t, d), F32),
        grid_spec=pltpu.PrefetchScalarGridSpec(
            num_scalar_prefetch=1,
            grid=(b // bb, nt),
            in_specs=[pl.BlockSpec((bb, tt, d), lambda i, j, ds: (i, j, 0)),
                      pl.BlockSpec((bb, 1, d), lambda i, j, ds: (i, 0, 5)),
                      pl.BlockSpec((tm, LANES), lambda i, j, ds: (i * nt + j, 0)),
                      pl.BlockSpec(memory_space=pl.ANY)],
            out_specs=pl.BlockSpec((bb, tt, d), lambda i, j, ds: (i, j, 0)),
            scratch_shapes=[pltpu.VMEM((TOP_K, tm, d), F32), pltpu.SemaphoreType.DMA(())],
        ),
        compiler_params=_params(("arbitrary", "arbitrary")),
    )(dest, x1, mod3, gates, out_rows)


def _route(top_idx):
    t = top_idx.shape[0]
    n_assign = t * TOP_K
    flat_e = top_idx[:, :TOP_K].reshape(-1)
    flat_tok = jnp.arange(n_assign, dtype=jnp.int32) // TOP_K
    onehot = (flat_e[:, None] == jnp.arange(N_EXPERTS, dtype=jnp.int32)[None, :]).astype(jnp.int32)
    running = jnp.cumsum(onehot, axis=0)
    counts = running[-1]
    padded = (counts + MOE_ROWS - 1) // MOE_ROWS * MOE_ROWS
    pad_end = jnp.cumsum(padded)
    pad_start = pad_end - padded
    dest = jnp.sum(onehot * (running - 1 + pad_start[None, :]), axis=1)
    n_blocks = -(-(n_assign + N_EXPERTS * (MOE_ROWS - 1)) // MOE_ROWS)
    n_blocks = -(-n_blocks * MOE_ROWS // GATHER_ROWS) * GATHER_ROWS // MOE_ROWS
    row_tok = jnp.zeros((n_blocks * MOE_ROWS,), jnp.int32).at[dest].set(flat_tok, unique_indices=True)
    block_start = jnp.arange(n_blocks, dtype=jnp.int32) * MOE_ROWS
    block_e = jnp.minimum(jnp.sum((pad_end[None, :] <= block_start[:, None]).astype(jnp.int32), axis=1),
                          N_EXPERTS - 1)
    n_used = (pad_end[-1:] // MOE_ROWS).astype(jnp.int32)
    return row_tok, block_e, n_used, dest


def _mixers(x, mod, pos, p, *, prompt, caches=None, page_table=None):
    b, t, d = x.shape
    tokens = b * t
    mod3 = mod.reshape(b, 1, 6 * d)
    bb, tt = (1, 512) if prompt else (32, t)
    h = _normmod(x, p['g_norm1'], mod3, 1, 0, bb, tt)
    z = _mm(h, p['w_in'], min(tokens, 1024), Z_COLS // 5)
    lc = min(t, CHUNK)
    reps = CHUNK // lc
    ws_t = jnp.tile(p['w_s'][:, :lc, :lc], (1, reps, reps))
    bs_t = jnp.broadcast_to(jnp.tile(p['b_s'][:, :lc], (1, reps))[:, :, None], (A_GROUPS, CHUNK, CHUNK))
    gm = _gmlp(z, p['g_v_ln'], p['b_v_ln'], ws_t, bs_t, lc, want_vn=not prompt)
    a_out = gm[0]

    half = ROPE_DIM // 2
    inv_freq = jnp.exp(-math.log(ROPE_THETA) * jnp.arange(half, dtype=F32) / half)
    ang = pos.astype(F32)[:, None] * inv_freq[None, :]
    cos, sin = jnp.cos(ang), jnp.sin(ang)
    tab = jnp.concatenate([cos, cos, -sin, sin], axis=-1)
    tab = jnp.broadcast_to(tab[None], (b, t, LANES)).reshape(tokens, LANES)
    swap = lambda g: jnp.concatenate([g[half:], g[:half]])
    gains = [p['g_q_lat'].reshape(1, -1), p['g_kv_lat'].reshape(1, -1), p['g_q_nope'].reshape(1, -1),
             jnp.concatenate([p['g_q_rope'], swap(p['g_q_rope'])]).reshape(1, -1),
             jnp.concatenate([p['g_k_rope'], swap(p['g_k_rope'])]).reshape(1, -1),
             p['g_k_nope'].reshape(1, -1)]
    weights = [p['wq_n'], p['wq_r'], p['w_uk'], p['w_uv'].T if prompt else p['w_uv']]
    mla = _mla(z, tab, gains, weights, 256, prompt)
    c_kv, k_pe, rinv = mla[:3]
    if prompt:
        b_out = _attend_prompt(mla[3], mla[4], mla[5], min(t, 1024))
    else:
        cache_latent, cache_rope_t, cache_rinv_t = caches
        rows = N_HEADS * t
        qa = _absorb(mla[3], p['w_uk']).reshape(b, rows, KV_LORA)
        qr = mla[4].reshape(b, rows, LANES)
        pad = PAGE_SIZE - t
        nlat = jnp.pad(c_kv.reshape(b, t, KV_LORA), ((0, 0), (0, pad), (0, 0)))
        nrope_t = jnp.pad(jnp.swapaxes(k_pe.reshape(b, t, ROPE_DIM), 1, 2), ((0, 0), (0, 0), (0, pad)))
        nrinv_t = jnp.pad(jnp.swapaxes(rinv.reshape(b, t, N_HEADS), 1, 2), ((0, 0), (0, 0), (0, pad)))
        o_lat = _decode(page_table, qa, qr, nlat, nrope_t, nrinv_t, cache_latent, cache_rope_t, cache_rinv_t, t)
        b_out = _upv(o_lat.reshape(tokens, N_HEADS * KV_LORA), p['w_uv'])
    merged = _merge(a_out, b_out, p['w_proj_a'], p['w_proj_b'], z, min(tokens, 512), 1024)
    bb2, tt2 = (1, 256) if prompt else (32, t)
    x1, h2, top_idx, gates = _outproj(merged, p['w_out'], x, mod3, p['g_norm2'], p['wr_pad'], p['br_pad'], bb2, tt2)
    vn = None if prompt else gm[1]
    return (x1, h2, top_idx, gates, mod3), (c_kv, k_pe, rinv, vn)


def kernel(x_prompt, x_sample, c_prompt, c_sample, cache_latent, cache_k_rope, cache_k_rinv, page_table,
           w_ada, b_ada, g_norm1, w_in, g_v_ln, b_v_ln, w_s, b_s, g_q_lat, w_uq, g_q_nope, g_q_rope,
           g_kv_lat, g_k_rope, w_uk, w_uv, g_k_nope, w_proj_a, w_proj_b, w_out, g_norm2,
           w_router, b_router, w_gate_up, b_gate_up, w_down, b_down):
    depth = w_ada.shape[0]
    bp, sp, d = x_prompt.shape
    bs, ss, _ = x_sample.shape
    past_len = page_table.shape[1] * PAGE_SIZE
    pos_p = jnp.arange(sp, dtype=jnp.int32)
    pos_s = past_len + jnp.arange(ss, dtype=jnp.int32)
    half = ROPE_DIM // 2
    y_p, y_s = x_prompt, x_sample
    outs = [[] for _ in range(7)]
    for l in range(depth):
        wi = w_in[l]
        kr = wi[:, 2816:2880]
        w_in_perm = jnp.concatenate(
            [wi[:, 0:2048], wi[:, 2880:6976], wi[:, 2048:2816], kr, kr[:, half:], kr[:, :half]], axis=1).astype(BF16)
        wq = w_uq[l]
        wq_rope = wq[:, :, NOPE_DIM:]
        p = {
            'g_norm1': g_norm1[l], 'w_in': w_in_perm, 'g_v_ln': g_v_ln[l], 'b_v_ln': b_v_ln[l],
            'w_s': w_s[l], 'b_s': b_s[l], 'g_q_lat': g_q_lat[l], 'g_q_nope': g_q_nope[l], 'g_q_rope': g_q_rope[l],
            'g_kv_lat': g_kv_lat[l], 'g_k_rope': g_k_rope[l], 'g_k_nope': g_k_nope[l],
            'wq_n': wq[:, :, :NOPE_DIM].reshape(Q_LORA, -1).astype(BF16),
            'wq_r': jnp.concatenate([wq_rope, wq_rope[:, :, half:], wq_rope[:, :, :half]],
                                    axis=-1).reshape(Q_LORA, -1).astype(BF16),
            'w_uk': w_uk[l].reshape(KV_LORA, -1).astype(BF16), 'w_uv': w_uv[l].reshape(KV_LORA, -1).astype(BF16),
            'w_proj_a': w_proj_a[l].astype(BF16), 'w_proj_b': w_proj_b[l].astype(BF16),
            'w_out': w_out[l].astype(BF16), 'g_norm2': g_norm2[l],
            'wr_pad': jnp.pad(w_router[l], ((0, 0), (0, LANES - N_EXPERTS))).astype(BF16),
            'br_pad': jnp.pad(b_router[l], (0, LANES - N_EXPERTS), constant_values=NEG_INF).reshape(1, LANES),
        }
        n_c = bp + bs
        c_all = jnp.pad(jnp.concatenate([c_prompt, c_sample], axis=0), ((0, -n_c % 8), (0, 0)))
        mod = _ada(c_all, w_ada[l], b_ada[l])
        caches = (cache_latent[l], jnp.swapaxes(cache_k_rope[l], 1, 2), jnp.swapaxes(cache_k_rinv[l], 1, 2))
        (x1_p, h2_p, idx_p, gate_p, mod3_p), aux_p = _mixers(y_p, mod[:bp], pos_p, p, prompt=True)
        (x1_s, h2_s, idx_s, gate_s, mod3_s), aux_s = _mixers(y_s, mod[bp:n_c], pos_s, p, prompt=False,
                                                            caches=caches, page_table=page_table)
        h2 = jnp.concatenate([h2_p, h2_s], axis=0)
        row_tok, block_e, n_used, dest = _route(jnp.concatenate([idx_p, idx_s], axis=0))
        x_sorted = _dispatch(row_tok, n_used, h2)
        out_rows = _moe_experts(block_e, n_used, x_sorted, w_gate_up[l], b_gate_up[l], w_down[l], b_down[l])
        n_p = bp * sp * TOP_K
        y_p = _combine(dest[:n_p], x1_p, mod3_p, gate_p, out_rows, 1, COMBINE_TOKENS)
        y_s = _combine(dest[n_p:], x1_s, mod3_s, gate_s, out_rows, COMBINE_TOKENS // ss, ss)
        for o, a in zip(outs, (aux_p[0].reshape(bp, sp, KV_LORA), aux_p[1].reshape(bp, sp, ROPE_DIM),
                               aux_p[2].reshape(bp, sp, N_HEADS), aux_s[0].reshape(bs, ss, KV_LORA),
                               aux_s[1].reshape(bs, ss, ROPE_DIM), aux_s[2].reshape(bs, ss, N_HEADS),
                               aux_s[3].reshape(bs, ss, D_A))):
            o.append(a)
    return (y_p, y_s) + tuple(jnp.stack(o) for o in outs)
```

```python
import functools
import math

import jax
import jax.numpy as jnp
from jax import lax
from jax.experimental import pallas as pl
from jax.experimental.pallas import tpu as pltpu

F32 = jnp.float32
BF16 = jnp.bfloat16

D_MODEL = 2048
D_A = D_MODEL // 2
A_GROUP = 128
A_GROUPS = D_A // A_GROUP
CHUNK = 128
N_HEADS = 16
Q_LORA = D_MODEL // 4
KV_LORA = D_MODEL // 8
NOPE_DIM = 128
ROPE_DIM = 64
V_DIM = 128
QK_DIM = NOPE_DIM + ROPE_DIM
ROPE_THETA = 10000.0
SCALE = 1.0 / math.sqrt(QK_DIM)
LOG2_E = math.log2(math.e)
N_EXPERTS = 32
TOP_K = 4
D_FF = D_MODEL
SWIGLU_LIMIT = 7.0
SWIGLU_ALPHA = 1.702
EPS = 1e-6
NEG_INF = -1e30
PAGE_SIZE = 128

LANES = 128
MOE_ROWS = 512
GATHER_ROWS = 512
ROW_DMA_UNROLL = 8
COMBINE_TOKENS = 128
PAGES_PER_STEP = 64
DECODE_CHAINS = 4
PAGE_DMA_PRIORITY = (1, 0, 0)
VMEM_LIMIT = 56 * 1024 * 1024

ZO_U = 0
ZO_V = ZO_U + D_A
ZO_GA = ZO_V + D_A
ZO_GB = ZO_GA + D_MODEL
ZO_Q = ZO_GB + D_MODEL
ZO_KV = ZO_Q + Q_LORA
ZO_KR = ZO_KV + KV_LORA
Z_COLS = ZO_KR + 2 * ROPE_DIM


def _params(sem):
    return pltpu.CompilerParams(dimension_semantics=sem, vmem_limit_bytes=VMEM_LIMIT)


def _dot(a, b):
    return jnp.dot(a, b, preferred_element_type=F32)


def _dot_nt(a, b):
    return lax.dot_general(a, b, (((1,), (1,)), ((), ())), preferred_element_type=F32)


def _sigmoid(x):
    return 1.0 / (1.0 + jnp.exp(-x))


def _gelu(x):
    c = math.sqrt(2.0 / math.pi)
    return 0.5 * x * (1.0 + jnp.tanh(c * (x + 0.044715 * (x * x * x))))


def _rms(x):
    return x * lax.rsqrt(jnp.mean(x * x, axis=-1, keepdims=True) + EPS)


def _ada_kernel(c_ref, w_ref, b_ref, o_ref):
    c = c_ref[...]
    a = (c * _sigmoid(c)).astype(BF16)
    o_ref[...] = _dot(a, w_ref[...].astype(BF16)) + b_ref[...]


def _ada(c_all, w_ada, b_ada):
    rows, d = c_all.shape
    n = w_ada.shape[1]
    tn = 1024
    return pl.pallas_call(
        _ada_kernel,
        out_shape=jax.ShapeDtypeStruct((rows, n), F32),
        grid=(n // tn,),
        in_specs=[pl.BlockSpec((rows, d), lambda j: (0, 0)),
                  pl.BlockSpec((d, tn), lambda j: (0, j)),
                  pl.BlockSpec((1, tn), lambda j: (0, j))],
        out_specs=pl.BlockSpec((rows, tn), lambda j: (0, j)),
        compiler_params=_params(("arbitrary",)),
    )(c_all, w_ada, b_ada.reshape(1, n))


def _normmod_kernel(x_ref, g_ref, sc_ref, sh_ref, o_ref):
    x = x_ref[...]
    y = _rms(x) * g_ref[...]
    y = y * (1.0 + sc_ref[...]) + sh_ref[...]
    o_ref[...] = y.reshape(o_ref.shape).astype(o_ref.dtype)


def _normmod(x, g, mod3, sc_idx, sh_idx, bb, tt):
    b, t, d = x.shape
    return pl.pallas_call(
        _normmod_kernel,
        out_shape=jax.ShapeDtypeStruct((b * t, d), BF16),
        grid=(b // bb, t // tt),
        in_specs=[pl.BlockSpec((bb, tt, d), lambda i, j: (i, j, 0)),
                  pl.BlockSpec((1, 1, d), lambda i, j: (0, 0, 0)),
                  pl.BlockSpec((bb, 1, d), lambda i, j: (i, 0, sc_idx)),
                  pl.BlockSpec((bb, 1, d), lambda i, j: (i, 0, sh_idx))],
        out_specs=pl.BlockSpec((bb * tt, d), lambda i, j: (i * (t // tt) + j, 0)),
        compiler_params=_params(("arbitrary", "arbitrary")),
    )(x, g.reshape(1, 1, d), mod3, mod3)


def _mm_kernel(x_ref, w_ref, o_ref):
    o_ref[...] = _dot(x_ref[...], w_ref[...]).astype(o_ref.dtype)


def _mm(x, w, tm, tn, out_dtype=F32):
    m, k = x.shape
    n = w.shape[1]
    return pl.pallas_call(
        _mm_kernel,
        out_shape=jax.ShapeDtypeStruct((m, n), out_dtype),
        grid=(m // tm, n // tn),
        in_specs=[pl.BlockSpec((tm, k), lambda i, j: (i, 0)),
                  pl.BlockSpec((k, tn), lambda i, j: (0, j))],
        out_specs=pl.BlockSpec((tm, tn), lambda i, j: (i, j)),
        compiler_params=_params(("arbitrary", "arbitrary")),
    )(x, w)


def _gmlp_kernel(zu_ref, zv_ref, gln_ref, bln_ref, ws_ref, bs_ref, a_ref, *maybe_vn_ref, lc):
    u = _gelu(zu_ref[...])
    gv = _gelu(zv_ref[...])
    xc = gv - jnp.mean(gv, axis=-1, keepdims=True)
    vn = xc * lax.rsqrt(jnp.mean(xc * xc, axis=-1, keepdims=True) + EPS)
    vn = vn * gln_ref[...] + bln_ref[...]
    if maybe_vn_ref:
        maybe_vn_ref[0][...] = vn
    row = lax.broadcasted_iota(jnp.int32, (CHUNK, CHUNK), 0)
    col = lax.broadcasted_iota(jnp.int32, (CHUNK, CHUNK), 1)
    mask = (col <= row) & ((row // lc) == (col // lc))
    vb = vn.astype(BF16)
    for g in range(A_GROUPS):
        w = jnp.where(mask, ws_ref[g], 0.0).astype(BF16)
        sl = slice(g * A_GROUP, (g + 1) * A_GROUP)
        s = _dot(w, vb[:, sl]) + bs_ref[g]
        a_ref[:, sl] = (u[:, sl] * s).astype(a_ref.dtype)


def _gmlp(z, g_ln, b_ln, ws_t, bs_t, lc, want_vn):
    t = z.shape[0]
    out_shape = [jax.ShapeDtypeStruct((t, D_A), BF16)]
    out_specs = [pl.BlockSpec((CHUNK, D_A), lambda i: (i, 0))]
    if want_vn:
        out_shape.append(jax.ShapeDtypeStruct((t, D_A), F32))
        out_specs.append(pl.BlockSpec((CHUNK, D_A), lambda i: (i, 0)))
    return pl.pallas_call(
        functools.partial(_gmlp_kernel, lc=lc),
        out_shape=out_shape,
        grid=(t // CHUNK,),
        in_specs=[pl.BlockSpec((CHUNK, D_A), lambda i: (i, ZO_U // D_A)),
                  pl.BlockSpec((CHUNK, D_A), lambda i: (i, ZO_V // D_A)),
                  pl.BlockSpec((1, D_A), lambda i: (0, 0)),
                  pl.BlockSpec((1, D_A), lambda i: (0, 0)),
                  pl.BlockSpec((A_GROUPS, CHUNK, CHUNK), lambda i: (0, 0, 0)),
                  pl.BlockSpec((A_GROUPS, CHUNK, CHUNK), lambda i: (0, 0, 0))],
        out_specs=out_specs,
        compiler_params=_params(("arbitrary",)),
    )(z, z, g_ln.reshape(1, D_A), b_ln.reshape(1, D_A), ws_t, bs_t)


def _rope_pair(y2, gain2, tab):
    p = y2 * gain2 * tab
    return p + pltpu.roll(p, ROPE_DIM, 1)


def _mla_kernel(zq_ref, zkv_ref, zkr_ref, tab_ref, gql_ref, gkv_ref, gqn_ref, gq2_ref, gk2_ref, gkn_ref,
                wqn_ref, wqr_ref, wuk_ref, wuv_ref, ckv_ref, kpe_ref, rinv_ref, *qkv_refs, prompt):
    tab = tab_ref[...]
    q_scale = SCALE * LOG2_E if prompt else SCALE
    q_lat = (_rms(zq_ref[...]) * gql_ref[...]).astype(BF16)
    c_kv = _rms(zkv_ref[...]) * gkv_ref[...]
    ckv_ref[...] = c_kv
    kr2 = zkr_ref[...]
    kr2 = kr2 * lax.rsqrt(jnp.sum(kr2 * kr2, axis=-1, keepdims=True) / (2 * ROPE_DIM) + EPS)
    kpe2 = _rope_pair(kr2, gk2_ref[...], tab)
    kpe_ref[...] = kpe2[:, :ROPE_DIM]
    cb = c_kv.astype(BF16)
    tm = cb.shape[0]
    lane = lax.broadcasted_iota(jnp.int32, (tm, N_HEADS), 1)
    rinv_all = jnp.zeros((tm, N_HEADS), F32)
    if prompt:
        col = lax.broadcasted_iota(jnp.int32, kpe2.shape, 1)
        kp_b = jnp.where(col < ROPE_DIM, kpe2, 0.0).astype(BF16)
    for h in range(N_HEADS):
        sl = slice(h * NOPE_DIM, (h + 1) * NOPE_DIM)
        qn = _rms(_dot(q_lat, wqn_ref[:, sl])) * gqn_ref[...]
        qr2 = _dot(q_lat, wqr_ref[:, sl])
        qr2 = qr2 * lax.rsqrt(jnp.sum(qr2 * qr2, axis=-1, keepdims=True) / (2 * ROPE_DIM) + EPS)
        qp = (_rope_pair(qr2, gq2_ref[...], tab) * q_scale).astype(BF16)
        k_raw = _dot(cb, wuk_ref[:, sl])
        rinv = lax.rsqrt(jnp.mean(k_raw * k_raw, axis=-1, keepdims=True) + EPS)
        rinv_all = jnp.where(lane == h, rinv, rinv_all)
        if prompt:
            q_ref, k_ref, v_ref = qkv_refs
            lo = slice(2 * h * NOPE_DIM, (2 * h + 1) * NOPE_DIM)
            hi = slice((2 * h + 1) * NOPE_DIM, (2 * h + 2) * NOPE_DIM)
            q_ref[:, lo] = (qn * q_scale).astype(BF16)
            q_ref[:, hi] = qp
            k_ref[:, lo] = (k_raw * rinv * gkn_ref[...]).astype(BF16)
            k_ref[:, hi] = kp_b
            v_ref[sl, :] = _dot_nt(wuv_ref[sl, :], cb).astype(BF16)
        else:
            qn_ref, qp_ref = qkv_refs
            qn_ref[:, sl] = (qn * gkn_ref[...]).astype(BF16)
            qp_ref[:, sl] = qp
    rinv_ref[...] = rinv_all


def _mla(z, tab, gains, weights, tm, prompt):
    t = z.shape[0]
    hd = N_HEADS * NOPE_DIM
    row = lambda w: pl.BlockSpec((tm, w), lambda i: (i, 0))
    full = lambda a: pl.BlockSpec(a.shape, lambda i: (0,) * a.ndim)
    out_shape = [jax.ShapeDtypeStruct((t, KV_LORA), F32), jax.ShapeDtypeStruct((t, ROPE_DIM), F32),
                 jax.ShapeDtypeStruct((t, N_HEADS), F32)]
    out_specs = [row(KV_LORA), row(ROPE_DIM), row(N_HEADS)]
    widths = (2 * hd, 2 * hd) if prompt else (hd, hd)
    out_shape += [jax.ShapeDtypeStruct((t, w), BF16) for w in widths]
    out_specs += [row(w) for w in widths]
    if prompt:
        out_shape.append(jax.ShapeDtypeStruct((hd, t), BF16))
        out_specs.append(pl.BlockSpec((hd, tm), lambda i: (0, i)))
    return pl.pallas_call(
        functools.partial(_mla_kernel, prompt=prompt),
        out_shape=out_shape,
        grid=(t // tm,),
        in_specs=[pl.BlockSpec((tm, Q_LORA), lambda i: (i, ZO_Q // Q_LORA)),
                  pl.BlockSpec((tm, KV_LORA), lambda i: (i, ZO_KV // KV_LORA)),
                  pl.BlockSpec((tm, LANES), lambda i: (i, ZO_KR // LANES)),
                  row(LANES)] + [full(a) for a in gains] + [full(a) for a in weights],
        out_specs=out_specs,
        compiler_params=_params(("arbitrary",)),
    )(z, z, z, tab, *gains, *weights)


def _fa_kernel(q_ref, k_ref, vt_ref, o_ref, sa_ref, sb_ref, m_ref, l_ref, acc_ref, *, tq, tk):
    i = pl.program_id(1)
    q = q_ref[...]
    m_ref[...] = jnp.full(m_ref.shape, NEG_INF, F32)
    l_ref[...] = jnp.zeros(l_ref.shape, F32)
    acc_ref[...] = jnp.zeros(acc_ref.shape, F32)

    def keys(j):
        return pl.ds(pl.multiple_of(j * tk, tk), tk)

    def qk(j):
        return _dot_nt(k_ref[keys(j), :], q)

    def update(s, j):
        m_old = m_ref[...]
        m_new = jnp.maximum(m_old, jnp.max(s, axis=0, keepdims=True))
        corr = jnp.exp2(m_old - m_new)
        p = jnp.exp2(s - m_new)
        l_ref[...] = l_ref[...] * corr + jnp.sum(p, axis=0, keepdims=True)
        m_ref[...] = m_new
        acc_ref[...] = acc_ref[...] * corr + _dot(vt_ref[:, keys(j)], p.astype(BF16))

    sa_ref[...] = qk(0)

    def pair(u, carry):
        sb_ref[...] = qk(2 * u + 1)
        update(sa_ref[...], 2 * u)
        sa_ref[...] = qk(2 * u + 2)
        update(sb_ref[...], 2 * u + 1)
        return carry

    lax.fori_loop(0, i, pair, 0)
    sb_ref[...] = qk(2 * i + 1)
    key = lax.broadcasted_iota(jnp.int32, (tk, tq), 0)
    qry = lax.broadcasted_iota(jnp.int32, (tk, tq), 1)
    update(jnp.where(key <= qry, sa_ref[...], NEG_INF), 2 * i)
    update(jnp.where(key + tk <= qry, sb_ref[...], NEG_INF), 2 * i + 1)
    o_ref[...] = (acc_ref[...] / l_ref[...]).T.astype(o_ref.dtype)


def _attend_prompt(q, k, vt, tq):
    t = q.shape[0]
    tk = tq // 2
    return pl.pallas_call(
        functools.partial(_fa_kernel, tq=tq, tk=tk),
        out_shape=jax.ShapeDtypeStruct((t, N_HEADS * V_DIM), BF16),
        grid=(N_HEADS, t // tq),
        in_specs=[pl.BlockSpec((tq, 2 * NOPE_DIM), lambda h, i: (i, h)),
                  pl.BlockSpec((t, 2 * NOPE_DIM), lambda h, i: (0, h)),
                  pl.BlockSpec((V_DIM, t), lambda h, i: (h, 0))],
        out_specs=pl.BlockSpec((tq, V_DIM), lambda h, i: (i, h)),
        scratch_shapes=[pltpu.VMEM((tk, tq), F32), pltpu.VMEM((tk, tq), F32),
                        pltpu.VMEM((1, tq), F32), pltpu.VMEM((1, tq), F32), pltpu.VMEM((V_DIM, tq), F32)],
        compiler_params=_params(("arbitrary", "arbitrary")),
    )(q, k, vt)


def _absorb_kernel(qn_ref, wuk_ref, qa_ref):
    qa_ref[...] = (_dot_nt(qn_ref[...], wuk_ref[...]) * SCALE).astype(qa_ref.dtype)


def _absorb(qn, w_uk_b):
    tokens = qn.shape[0]
    return pl.pallas_call(
        _absorb_kernel,
        out_shape=jax.ShapeDtypeStruct((tokens, N_HEADS * KV_LORA), BF16),
        grid=(N_HEADS,),
        in_specs=[pl.BlockSpec((tokens, NOPE_DIM), lambda h: (0, h)),
                  pl.BlockSpec((KV_LORA, NOPE_DIM), lambda h: (0, h))],
        out_specs=pl.BlockSpec((tokens, KV_LORA), lambda h: (0, h)),
        compiler_params=_params(("arbitrary",)),
    )(qn, w_uk_b)


def _decode_kernel(pt_ref, qa_ref, qr_ref, nlat_ref, nrope_ref, nrinv_ref, lat_hbm, rope_hbm, rinv_hbm, o_ref,
                   lat_buf, rope_buf, rinv_buf, sems, ck_s, kr_s, ri_s, m_ref, l_ref, acc_ref, *, t_new):
    pc = PAGES_PER_STEP
    b, c = pl.program_id(0), pl.program_id(1)
    nb, nc = pl.num_programs(0), pl.num_programs(1)
    n = b * nc + c
    slot = n % 2
    rows = N_HEADS * t_new

    def page_copies(bi, ci, sl):
        copies = []
        for j in range(pc):
            page = pt_ref[bi, ci * pc + j]
            copies.append(pltpu.make_async_copy(lat_hbm.at[page], lat_buf.at[sl, j], sems.at[sl, 0]))
            copies.append(pltpu.make_async_copy(rope_hbm.at[page], rope_buf.at[sl, j], sems.at[sl, 1]))
            copies.append(pltpu.make_async_copy(rinv_hbm.at[page], rinv_buf.at[sl, j], sems.at[sl, 2]))
        return copies

    @pl.when(n == 0)
    def _():
        for q, cp in enumerate(page_copies(b, c, slot)):
            cp.start(priority=PAGE_DMA_PRIORITY[q % 3])

    @pl.when(n + 1 < nb * nc)
    def _():
        wrap = c + 1 == nc
        for q, cp in enumerate(page_copies(jnp.where(wrap, b + 1, b), jnp.where(wrap, 0, c + 1), 1 - slot)):
            cp.start(priority=PAGE_DMA_PRIORITY[q % 3])

    for cp in page_copies(b, c, slot):
        cp.wait()
    qa = qa_ref[...]
    qr = qr_ref[...][:, :ROPE_DIM]

    def scores(ck, krt, rit):
        n = ck.shape[0]
        return (_dot_nt(qa, ck).reshape(t_new, N_HEADS, n) * rit[None]).reshape(rows, n) + _dot(qr, krt)

    def fold(g, s, ck):
        m_old = m_ref[g]
        m_new = jnp.maximum(m_old, jnp.max(s, axis=-1, keepdims=True))
        corr = jnp.exp(m_old - m_new)
        p = jnp.exp(s - m_new)
        l_ref[g] = l_ref[g] * corr + jnp.sum(p, axis=-1, keepdims=True)
        m_ref[g] = m_new
        acc_ref[g] = acc_ref[g] * corr + _dot(p.astype(BF16), ck)

    @pl.when(c == 0)
    def _():
        m_ref[...] = jnp.full(m_ref.shape, NEG_INF, F32)
        l_ref[...] = jnp.zeros(l_ref.shape, F32)
        acc_ref[...] = jnp.zeros(acc_ref.shape, F32)
        ck = nlat_ref[...].astype(BF16)
        s = scores(ck, nrope_ref[...].astype(BF16), nrinv_ref[...])
        row = lax.broadcasted_iota(jnp.int32, s.shape, 0)
        col = lax.broadcasted_iota(jnp.int32, s.shape, 1)
        fold(0, jnp.where(col <= row // N_HEADS, s, NEG_INF), ck)

    for j in range(pc):
        ck_s[j * PAGE_SIZE:(j + 1) * PAGE_SIZE, :] = lat_buf[slot, j].astype(BF16)
        kr_s[:, j * PAGE_SIZE:(j + 1) * PAGE_SIZE] = rope_buf[slot, j].astype(BF16)
        ri_s[:, j * PAGE_SIZE:(j + 1) * PAGE_SIZE] = rinv_buf[slot, j]
    keys = pc * PAGE_SIZE // DECODE_CHAINS
    cks = [ck_s[g * keys:(g + 1) * keys, :] for g in range(DECODE_CHAINS)]
    ss = [scores(cks[g], kr_s[:, g * keys:(g + 1) * keys], ri_s[:, g * keys:(g + 1) * keys])
          for g in range(DECODE_CHAINS)]
    for g in range(DECODE_CHAINS):
        fold(g, ss[g], cks[g])

    @pl.when(c == nc - 1)
    def _():
        m = m_ref[0]
        for g in range(1, DECODE_CHAINS):
            m = jnp.maximum(m, m_ref[g])
        l = jnp.zeros(m.shape, F32)
        acc = jnp.zeros(acc_ref.shape[1:], F32)
        for g in range(DECODE_CHAINS):
            w = jnp.exp(m_ref[g] - m)
            l = l + l_ref[g] * w
            acc = acc + acc_ref[g] * w
        o_ref[...] = (acc / l).astype(o_ref.dtype)


def _decode(page_table, qa, qr, nlat, nrope_t, nrinv_t, cache_latent, cache_rope_t, cache_rinv_t, t_new):
    b, n_pages = page_table.shape
    pc = PAGES_PER_STEP
    rows = N_HEADS * t_new
    seq = lambda r, w: pl.BlockSpec((None, r, w), lambda i, c, pt: (i, 0, 0))
    hbm = pl.BlockSpec(memory_space=pl.ANY)
    keys = pc * PAGE_SIZE
    grid_spec = pltpu.PrefetchScalarGridSpec(
        num_scalar_prefetch=1,
        grid=(b, n_pages // pc),
        in_specs=[seq(rows, KV_LORA), seq(rows, LANES),
                  seq(PAGE_SIZE, KV_LORA), seq(ROPE_DIM, PAGE_SIZE), seq(N_HEADS, PAGE_SIZE), hbm, hbm, hbm],
        out_specs=seq(rows, KV_LORA),
        scratch_shapes=[pltpu.VMEM((2, pc, PAGE_SIZE, KV_LORA), F32), pltpu.VMEM((2, pc, ROPE_DIM, PAGE_SIZE), F32),
                        pltpu.VMEM((2, pc, N_HEADS, PAGE_SIZE), F32), pltpu.SemaphoreType.DMA((2, 3)),
                        pltpu.VMEM((keys, KV_LORA), BF16), pltpu.VMEM((ROPE_DIM, keys), BF16),
                        pltpu.VMEM((N_HEADS, keys), F32),
                        pltpu.VMEM((DECODE_CHAINS, rows, 1), F32), pltpu.VMEM((DECODE_CHAINS, rows, 1), F32),
                        pltpu.VMEM((DECODE_CHAINS, rows, KV_LORA), F32)],
    )
    return pl.pallas_call(
        functools.partial(_decode_kernel, t_new=t_new),
        out_shape=jax.ShapeDtypeStruct((b, rows, KV_LORA), F32),
        grid_spec=grid_spec,
        compiler_params=_params(("arbitrary", "arbitrary")),
    )(page_table, qa, qr, nlat, nrope_t, nrinv_t, cache_latent, cache_rope_t, cache_rinv_t)


def _upv_kernel(o_ref, wuv_ref, out_ref):
    out_ref[...] = _dot(o_ref[...].astype(BF16), wuv_ref[...]).astype(out_ref.dtype)


def _upv(o_lat, w_uv_b):
    tokens = o_lat.shape[0]
    return pl.pallas_call(
        _upv_kernel,
        out_shape=jax.ShapeDtypeStruct((tokens, N_HEADS * V_DIM), BF16),
        grid=(N_HEADS,),
        in_specs=[pl.BlockSpec((tokens, KV_LORA), lambda h: (0, h)),
                  pl.BlockSpec((KV_LORA, V_DIM), lambda h: (0, h))],
        out_specs=pl.BlockSpec((tokens, V_DIM), lambda h: (0, h)),
        compiler_params=_params(("arbitrary",)),
    )(o_lat, w_uv_b)


def _merge_kernel(a_ref, b_ref, wa_ref, wb_ref, ga_ref, gb_ref, o_ref):
    pa = _dot(a_ref[...], wa_ref[...])
    pb = _dot(b_ref[...], wb_ref[...])
    o_ref[...] = (_sigmoid(ga_ref[...]) * pa + _sigmoid(gb_ref[...]) * pb).astype(o_ref.dtype)


def _merge(a_out, b_out, wpa, wpb, z, tm, tn):
    t = a_out.shape[0]
    return pl.pallas_call(
        _merge_kernel,
        out_shape=jax.ShapeDtypeStruct((t, D_MODEL), BF16),
        grid=(t // tm, D_MODEL // tn),
        in_specs=[pl.BlockSpec((tm, D_A), lambda i, j: (i, 0)),
                  pl.BlockSpec((tm, D_MODEL), lambda i, j: (i, 0)),
                  pl.BlockSpec((D_A, tn), lambda i, j: (0, j)),
                  pl.BlockSpec((D_MODEL, tn), lambda i, j: (0, j)),
                  pl.BlockSpec((tm, tn), lambda i, j: (i, ZO_GA // tn + j)),
                  pl.BlockSpec((tm, tn), lambda i, j: (i, ZO_GB // tn + j))],
        out_specs=pl.BlockSpec((tm, tn), lambda i, j: (i, j)),
        compiler_params=_params(("arbitrary", "arbitrary")),
    )(a_out, b_out, wpa, wpb, z, z)


def _outproj_kernel(m_ref, wo_ref, x_ref, g1_ref, gn_ref, sc_ref, sh_ref, wr_ref, br_ref,
                    x1_ref, h2_ref, idx_ref, gate_ref):
    shape3 = x_ref.shape
    y = _dot(m_ref[...], wo_ref[...])
    x1 = x_ref[...] + g1_ref[...] * y.reshape(shape3)
    x1_ref[...] = x1
    h2 = _rms(x1) * gn_ref[...]
    h2 = (h2 * (1.0 + sc_ref[...]) + sh_ref[...]).reshape(y.shape)
    h2_ref[...] = h2
    logits = _dot(h2.astype(BF16), wr_ref[...]) + br_ref[...]
    lane = lax.broadcasted_iota(jnp.int32, logits.shape, 1).astype(F32)
    vals, idxs = [], []
    for _ in range(TOP_K):
        mx = jnp.max(logits, axis=-1, keepdims=True)
        am = jnp.min(jnp.where(logits == mx, lane, float(LANES)), axis=-1, keepdims=True)
        vals.append(mx)
        idxs.append(am)
        logits = jnp.where(lane == am, -3.0e38, logits)
    es = [jnp.exp(v - vals[0]) for v in vals]
    den = es[0] + es[1] + es[2] + es[3]
    idx_out = jnp.zeros(lane.shape, F32)
    gate_out = jnp.zeros(lane.shape, F32)
    for k in range(TOP_K):
        idx_out = jnp.where(lane == float(k), idxs[k], idx_out)
        gate_out = jnp.where(lane == float(k), es[k] / den, gate_out)
    idx_ref[...] = idx_out.astype(jnp.int32)
    gate_ref[...] = gate_out


def _outproj(merged, w_out_b, x, mod3, g_norm2, wr_pad, br_pad, bb, tt):
    b, t, d = x.shape
    tm = bb * tt
    nt = t // tt
    tok = lambda w: pl.BlockSpec((tm, w), lambda i, j: (i * nt + j, 0))
    modspec = lambda k: pl.BlockSpec((bb, 1, d), lambda i, j: (i, 0, k))
    return pl.pallas_call(
        _outproj_kernel,
        out_shape=[jax.ShapeDtypeStruct((b, t, d), F32), jax.ShapeDtypeStruct((b * t, d), F32),
                   jax.ShapeDtypeStruct((b * t, LANES), jnp.int32), jax.ShapeDtypeStruct((b * t, LANES), F32)],
        grid=(b // bb, nt),
        in_specs=[tok(d),
                  pl.BlockSpec((d, d), lambda i, j: (0, 0)),
                  pl.BlockSpec((bb, tt, d), lambda i, j: (i, j, 0)),
                  modspec(2),
                  pl.BlockSpec((1, 1, d), lambda i, j: (0, 0, 0)),
                  modspec(4), modspec(3),
                  pl.BlockSpec((d, LANES), lambda i, j: (0, 0)),
                  pl.BlockSpec((1, LANES), lambda i, j: (0, 0))],
        out_specs=[pl.BlockSpec((bb, tt, d), lambda i, j: (i, j, 0)), tok(d), tok(LANES), tok(LANES)],
        compiler_params=_params(("arbitrary", "arbitrary")),
    )(merged, w_out_b, x, mod3, g_norm2.reshape(1, 1, d), mod3, mod3, wr_pad, br_pad)


def _row_copy(src_ref, dst_ref, src_row, dst_row, sem):
    return pltpu.make_async_copy(src_ref.at[pl.ds(src_row, 1)], dst_ref.at[pl.ds(dst_row, 1)], sem)


def _dispatch_kernel(tok_ref, used_ref, h_ref, o_ref, buf, sem):
    n = buf.shape[0]
    base = pl.program_id(0) * n
    live = base < used_ref[0] * MOE_ROWS

    @pl.when(live)
    def _():
        def start(g, carry):
            for k in range(ROW_DMA_UNROLL):
                r = g * ROW_DMA_UNROLL + k
                _row_copy(h_ref, buf, tok_ref[base + r], r, sem).start(priority=k % 2)
            return carry

        lax.fori_loop(0, n // ROW_DMA_UNROLL, start, 0)
        pltpu.make_async_copy(h_ref.at[pl.ds(0, n)], buf, sem).wait()
        o_ref[...] = buf[...].astype(o_ref.dtype)

    @pl.when(jnp.logical_not(live))
    def _():
        o_ref[...] = jnp.zeros(o_ref.shape, o_ref.dtype)


def _dispatch(row_tok, n_used, h2):
    n_rows = row_tok.shape[0]
    d = h2.shape[1]
    return pl.pallas_call(
        _dispatch_kernel,
        out_shape=jax.ShapeDtypeStruct((n_rows, d), BF16),
        grid_spec=pltpu.PrefetchScalarGridSpec(
            num_scalar_prefetch=2,
            grid=(n_rows // GATHER_ROWS,),
            in_specs=[pl.BlockSpec(memory_space=pl.ANY)],
            out_specs=pl.BlockSpec((GATHER_ROWS, d), lambda i, tok, nu: (i, 0)),
            scratch_shapes=[pltpu.VMEM((GATHER_ROWS, d), F32), pltpu.SemaphoreType.DMA(())],
        ),
        compiler_params=_params(("arbitrary",)),
    )(row_tok, n_used, h2)


def _moe_up_kernel(be_ref, nxt_ref, used_ref, x_ref, bg_ref, bl_ref, w_hbm, o_ref, wbuf, sems, slot_ref, wg_s, wl_s, *, tf, nf):
    j, i = pl.program_id(0), pl.program_id(1)
    e = be_ref[i]

    def copies(ee, jj, sl):
        return [pltpu.make_async_copy(w_hbm.at[ee, :, pl.ds(pl.multiple_of((g * nf + jj) * tf, tf), tf)],
                                      wbuf.at[sl, g], sems.at[sl, g]) for g in range(2)]

    @pl.when((j == 0) & (i == 0))
    def _():
        slot_ref[0] = 0
        for cp in copies(e, j, 0):
            cp.start()

    @pl.when((i == 0) | (e != be_ref[jnp.maximum(i - 1, 0)]))
    def _():
        sl = slot_ref[0]
        for cp in copies(e, j, sl):
            cp.wait()
        nx = nxt_ref[i]
        more = nx >= 0

        @pl.when(more | (j + 1 < nf))
        def _():
            for cp in copies(jnp.where(more, nx, be_ref[0]), jnp.where(more, j, j + 1), 1 - sl):
                cp.start()

        wg_s[...] = wbuf[sl, 0].astype(BF16)
        wl_s[...] = wbuf[sl, 1].astype(BF16)
        slot_ref[0] = 1 - sl

    @pl.when(i < used_ref[0])
    def _():
        x = x_ref[...]
        glu = jnp.minimum(_dot(x, wg_s[...]) + bg_ref[...], SWIGLU_LIMIT)
        lin = jnp.clip(_dot(x, wl_s[...]) + bl_ref[...], -SWIGLU_LIMIT, SWIGLU_LIMIT)
        o_ref[...] = (glu * _sigmoid(SWIGLU_ALPHA * glu) * (lin + 1.0)).astype(o_ref.dtype)

    @pl.when(i >= used_ref[0])
    def _():
        o_ref[...] = jnp.zeros(o_ref.shape, o_ref.dtype)


def _moe_down_kernel(be_ref, nxt_ref, used_ref, a_ref, b_ref, w_hbm, o_ref, wbuf, sems, slot_ref, w_s):
    i = pl.program_id(0)
    e = be_ref[i]

    def copy(ee, sl):
        return pltpu.make_async_copy(w_hbm.at[ee], wbuf.at[sl], sems.at[sl])

    @pl.when(i == 0)
    def _():
        slot_ref[0] = 0
        copy(e, 0).start()

    @pl.when((i == 0) | (e != be_ref[jnp.maximum(i - 1, 0)]))
    def _():
        sl = slot_ref[0]
        copy(e, sl).wait()
        nx = nxt_ref[i]

        @pl.when(nx >= 0)
        def _():
            copy(nx, 1 - sl).start()

        w_s[...] = wbuf[sl].astype(BF16)
        slot_ref[0] = 1 - sl

    @pl.when(i < used_ref[0])
    def _():
        o_ref[...] = _dot(a_ref[...], w_s[...]) + b_ref[...]

    @pl.when(i >= used_ref[0])
    def _():
        o_ref[...] = jnp.zeros(o_ref.shape, o_ref.dtype)


def _moe_experts(block_e, n_used, x_sorted, w_gate_up, b_gate_up, w_down, b_down):
    n_rows, d = x_sorted.shape
    n_blocks = n_rows // MOE_ROWS
    tf = 1024
    nf = D_FF // tf
    ids = jnp.arange(N_EXPERTS, dtype=jnp.int32)
    present = (block_e[None, :] == ids[:, None]).any(axis=1)
    later = jnp.where(present[None, :] & (ids[None, :] > ids[:, None]), ids[None, :], N_EXPERTS).min(axis=1)
    nxt = jnp.where(later == N_EXPERTS, -1, later).astype(jnp.int32)[block_e]
    bgu3 = b_gate_up.reshape(N_EXPERTS, 1, 2 * D_FF)
    hbm = pl.BlockSpec(memory_space=pl.ANY)
    act = pl.pallas_call(
        functools.partial(_moe_up_kernel, tf=tf, nf=nf),
        out_shape=jax.ShapeDtypeStruct((n_rows, D_FF), BF16),
        grid_spec=pltpu.PrefetchScalarGridSpec(
            num_scalar_prefetch=3,
            grid=(nf, n_blocks),
            in_specs=[pl.BlockSpec((MOE_ROWS, d), lambda j, i, be, nx, nu: (i, 0)),
                      pl.BlockSpec((None, 1, tf), lambda j, i, be, nx, nu: (be[i], 0, j)),
                      pl.BlockSpec((None, 1, tf), lambda j, i, be, nx, nu: (be[i], 0, nf + j)),
                      hbm],
            out_specs=pl.BlockSpec((MOE_ROWS, tf), lambda j, i, be, nx, nu: (i, j)),
            scratch_shapes=[pltpu.VMEM((2, 2, d, tf), F32), pltpu.SemaphoreType.DMA((2, 2)),
                            pltpu.SMEM((1,), jnp.int32), pltpu.VMEM((d, tf), BF16), pltpu.VMEM((d, tf), BF16)],
        ),
        compiler_params=_params(("arbitrary", "arbitrary")),
    )(block_e, nxt, n_used, x_sorted, bgu3, bgu3, w_gate_up)
    return pl.pallas_call(
        _moe_down_kernel,
        out_shape=jax.ShapeDtypeStruct((n_rows, d), F32),
        grid_spec=pltpu.PrefetchScalarGridSpec(
            num_scalar_prefetch=3,
            grid=(n_blocks,),
            in_specs=[pl.BlockSpec((MOE_ROWS, D_FF), lambda i, be, nx, nu: (i, 0)),
                      pl.BlockSpec((None, 1, d), lambda i, be, nx, nu: (be[i], 0, 0)),
                      hbm],
            out_specs=pl.BlockSpec((MOE_ROWS, d), lambda i, be, nx, nu: (i, 0)),
            scratch_shapes=[pltpu.VMEM((2, D_FF, d), F32), pltpu.SemaphoreType.DMA((2,)),
                            pltpu.SMEM((1,), jnp.int32), pltpu.VMEM((D_FF, d), BF16)],
        ),
        compiler_params=_params(("arbitrary",)),
    )(block_e, nxt, n_used, act, b_down.reshape(N_EXPERTS, 1, d), w_down)


def _combine_kernel(dest_ref, x1_ref, g2_ref, gate_ref, rows_ref, o_ref, buf, sem):
    tm = buf.shape[1]
    nt = pl.num_programs(1)
    base = (pl.program_id(0) * nt + pl.program_id(1)) * (tm * TOP_K)

    def start(t, carry):
        for k in range(TOP_K):
            _row_copy(rows_ref, buf.at[k], dest_ref[base + t * TOP_K + k], t, sem).start(priority=k % 2)
        return carry

    lax.fori_loop(0, tm, start, 0, unroll=2)
    for k in range(TOP_K):
        pltpu.make_async_copy(rows_ref.at[pl.ds(0, tm)], buf.at[k], sem).wait()
    gates = gate_ref[...]
    ff = buf[0] * gates[:, 0:1]
    for k in range(1, TOP_K):
        ff = ff + buf[k] * gates[:, k:k + 1]
    o_ref[...] = x1_ref[...] + g2_ref[...] * ff.reshape(x1_ref.shape)


def _combine(dest, x1, mod3, gates, out_rows, bb, tt):
    b, t, d = x1.shape
    nt = t // tt
    tm = bb * tt
    return pl.pallas_call(
        _combine_kernel,
        out_shape=jax.ShapeDtypeStruct((b, t, d), F32),
        grid_spec=pltpu.PrefetchScalarGridSpec(
            num_scalar_prefetch=1,
            grid=(b // bb, nt),
            in_specs=[pl.BlockSpec((bb, tt, d), lambda i, j, ds: (i, j, 0)),
                      pl.BlockSpec((bb, 1, d), lambda i, j, ds: (i, 0, 5)),
                      pl.BlockSpec((tm, LANES), lambda i, j, ds: (i * nt + j, 0)),
                      pl.BlockSpec(memory_space=pl.ANY)],
            out_specs=pl.BlockSpec((bb, tt, d), lambda i, j, ds: (i, j, 0)),
            scratch_shapes=[pltpu.VMEM((TOP_K, tm, d), F32), pltpu.SemaphoreType.DMA(())],
        ),
        compiler_params=_params(("arbitrary", "arbitrary")),
    )(dest, x1, mod3, gates, out_rows)


def _route(top_idx):
    t = top_idx.shape[0]
    n_assign = t * TOP_K
    flat_e = top_idx[:, :TOP_K].reshape(-1)
    flat_tok = jnp.arange(n_assign, dtype=jnp.int32) // TOP_K
    onehot = (flat_e[:, None] == jnp.arange(N_EXPERTS, dtype=jnp.int32)[None, :]).astype(jnp.int32)
    running = jnp.cumsum(onehot, axis=0)
    counts = running[-1]
    padded = (counts + MOE_ROWS - 1) // MOE_ROWS * MOE_ROWS
    pad_end = jnp.cumsum(padded)
    pad_start = pad_end - padded
    dest = jnp.sum(onehot * (running - 1 + pad_start[None, :]), axis=1)
    n_blocks = -(-(n_assign + N_EXPERTS * (MOE_ROWS - 1)) // MOE_ROWS)
    n_blocks = -(-n_blocks * MOE_ROWS // GATHER_ROWS) * GATHER_ROWS // MOE_ROWS
    row_tok = jnp.zeros((n_blocks * MOE_ROWS,), jnp.int32).at[dest].set(flat_tok, unique_indices=True)
    block_start = jnp.arange(n_blocks, dtype=jnp.int32) * MOE_ROWS
    block_e = jnp.minimum(jnp.sum((pad_end[None, :] <= block_start[:, None]).astype(jnp.int32), axis=1),
                          N_EXPERTS - 1)
    n_used = (pad_end[-1:] // MOE_ROWS).astype(jnp.int32)
    return row_tok, block_e, n_used, dest


def _mixers(x, mod, pos, p, *, prompt, caches=None, page_table=None):
    b, t, d = x.shape
    tokens = b * t
    mod3 = mod.reshape(b, 1, 6 * d)
    bb, tt = (1, 512) if prompt else (32, t)
    h = _normmod(x, p['g_norm1'], mod3, 1, 0, bb, tt)
    z = _mm(h, p['w_in'], min(tokens, 1024), Z_COLS // 5)
    lc = min(t, CHUNK)
    reps = CHUNK // lc
    ws_t = jnp.tile(p['w_s'][:, :lc, :lc], (1, reps, reps))
    bs_t = jnp.broadcast_to(jnp.tile(p['b_s'][:, :lc], (1, reps))[:, :, None], (A_GROUPS, CHUNK, CHUNK))
    gm = _gmlp(z, p['g_v_ln'], p['b_v_ln'], ws_t, bs_t, lc, want_vn=not prompt)
    a_out = gm[0]

    half = ROPE_DIM // 2
    inv_freq = jnp.exp(-math.log(ROPE_THETA) * jnp.arange(half, dtype=F32) / half)
    ang = pos.astype(F32)[:, None] * inv_freq[None, :]
    cos, sin = jnp.cos(ang), jnp.sin(ang)
    tab = jnp.concatenate([cos, cos, -sin, sin], axis=-1)
    tab = jnp.broadcast_to(tab[None], (b, t, LANES)).reshape(tokens, LANES)
    swap = lambda g: jnp.concatenate([g[half:], g[:half]])
    gains = [p['g_q_lat'].reshape(1, -1), p['g_kv_lat'].reshape(1, -1), p['g_q_nope'].reshape(1, -1),
             jnp.concatenate([p['g_q_rope'], swap(p['g_q_rope'])]).reshape(1, -1),
             jnp.concatenate([p['g_k_rope'], swap(p['g_k_rope'])]).reshape(1, -1),
             p['g_k_nope'].reshape(1, -1)]
    weights = [p['wq_n'], p['wq_r'], p['w_uk'], p['w_uv'].T if prompt else p['w_uv']]
    mla = _mla(z, tab, gains, weights, 256, prompt)
    c_kv, k_pe, rinv = mla[:3]
    if prompt:
        b_out = _attend_prompt(mla[3], mla[4], mla[5], min(t, 1024))
    else:
        cache_latent, cache_rope_t, cache_rinv_t = caches
        rows = N_HEADS * t
        qa = _absorb(mla[3], p['w_uk']).reshape(b, rows, KV_LORA)
        qr = mla[4].reshape(b, rows, LANES)
        pad = PAGE_SIZE - t
        nlat = jnp.pad(c_kv.reshape(b, t, KV_LORA), ((0, 0), (0, pad), (0, 0)))
        nrope_t = jnp.pad(jnp.swapaxes(k_pe.reshape(b, t, ROPE_DIM), 1, 2), ((0, 0), (0, 0), (0, pad)))
        nrinv_t = jnp.pad(jnp.swapaxes(rinv.reshape(b, t, N_HEADS), 1, 2), ((0, 0), (0, 0), (0, pad)))
        o_lat = _decode(page_table, qa, qr, nlat, nrope_t, nrinv_t, cache_latent, cache_rope_t, cache_rinv_t, t)
        b_out = _upv(o_lat.reshape(tokens, N_HEADS * KV_LORA), p['w_uv'])
    merged = _merge(a_out, b_out, p['w_proj_a'], p['w_proj_b'], z, min(tokens, 512), 1024)
    bb2, tt2 = (1, 256) if prompt else (32, t)
    x1, h2, top_idx, gates = _outproj(merged, p['w_out'], x, mod3, p['g_norm2'], p['wr_pad'], p['br_pad'], bb2, tt2)
    vn = None if prompt else gm[1]
    return (x1, h2, top_idx, gates, mod3), (c_kv, k_pe, rinv, vn)


def kernel(x_prompt, x_sample, c_prompt, c_sample, cache_latent, cache_k_rope, cache_k_rinv, page_table,
           w_ada, b_ada, g_norm1, w_in, g_v_ln, b_v_ln, w_s, b_s, g_q_lat, w_uq, g_q_nope, g_q_rope,
           g_kv_lat, g_k_rope, w_uk, w_uv, g_k_nope, w_proj_a, w_proj_b, w_out, g_norm2,
           w_router, b_router, w_gate_up, b_gate_up, w_down, b_down):
    depth = w_ada.shape[0]
    bp, sp, d = x_prompt.shape
    bs, ss, _ = x_sample.shape
    past_len = page_table.shape[1] * PAGE_SIZE
    pos_p = jnp.arange(sp, dtype=jnp.int32)
    pos_s = past_len + jnp.arange(ss, dtype=jnp.int32)
    half = ROPE_DIM // 2
    y_p, y_s = x_prompt, x_sample
    outs = [[] for _ in range(7)]
    for l in range(depth):
        wi = w_in[l]
        kr = wi[:, 2816:2880]
        w_in_perm = jnp.concatenate(
            [wi[:, 0:2048], wi[:, 2880:6976], wi[:, 2048:2816], kr, kr[:, half:], kr[:, :half]], axis=1).astype(BF16)
        wq = w_uq[l]
        wq_rope = wq[:, :, NOPE_DIM:]
        p = {
            'g_norm1': g_norm1[l], 'w_in': w_in_perm, 'g_v_ln': g_v_ln[l], 'b_v_ln': b_v_ln[l],
            'w_s': w_s[l], 'b_s': b_s[l], 'g_q_lat': g_q_lat[l], 'g_q_nope': g_q_nope[l], 'g_q_rope': g_q_rope[l],
            'g_kv_lat': g_kv_lat[l], 'g_k_rope': g_k_rope[l], 'g_k_nope': g_k_nope[l],
            'wq_n': wq[:, :, :NOPE_DIM].reshape(Q_LORA, -1).astype(BF16),
            'wq_r': jnp.concatenate([wq_rope, wq_rope[:, :, half:], wq_rope[:, :, :half]],
                                    axis=-1).reshape(Q_LORA, -1).astype(BF16),
            'w_uk': w_uk[l].reshape(KV_LORA, -1).astype(BF16), 'w_uv': w_uv[l].reshape(KV_LORA, -1).astype(BF16),
            'w_proj_a': w_proj_a[l].astype(BF16), 'w_proj_b': w_proj_b[l].astype(BF16),
            'w_out': w_out[l].astype(BF16), 'g_norm2': g_norm2[l],
            'wr_pad': jnp.pad(w_router[l], ((0, 0), (0, LANES - N_EXPERTS))).astype(BF16),
            'br_pad': jnp.pad(b_router[l], (0, LANES - N_EXPERTS), constant_values=NEG_INF).reshape(1, LANES),
        }
        n_c = bp + bs
        c_all = jnp.pad(jnp.concatenate([c_prompt, c_sample], axis=0), ((0, -n_c % 8), (0, 0)))
        mod = _ada(c_all, w_ada[l], b_ada[l])
        caches = (cache_latent[l], jnp.swapaxes(cache_k_rope[l], 1, 2), jnp.swapaxes(cache_k_rinv[l], 1, 2))
        (x1_p, h2_p, idx_p, gate_p, mod3_p), aux_p = _mixers(y_p, mod[:bp], pos_p, p, prompt=True)
        (x1_s, h2_s, idx_s, gate_s, mod3_s), aux_s = _mixers(y_s, mod[bp:n_c], pos_s, p, prompt=False,
                                                            caches=caches, page_table=page_table)
        h2 = jnp.concatenate([h2_p, h2_s], axis=0)
        row_tok, block_e, n_used, dest = _route(jnp.concatenate([idx_p, idx_s], axis=0))
        x_sorted = _dispatch(row_tok, n_used, h2)
        out_rows = _moe_experts(block_e, n_used, x_sorted, w_gate_up[l], b_gate_up[l], w_down[l], b_down[l])
        n_p = bp * sp * TOP_K
        y_p = _combine(dest[:n_p], x1_p, mod3_p, gate_p, out_rows, 1, COMBINE_TOKENS)
        y_s = _combine(dest[n_p:], x1_s, mod3_s, gate_s, out_rows, COMBINE_TOKENS // ss, ss)
        for o, a in zip(outs, (aux_p[0].reshape(bp, sp, KV_LORA), aux_p[1].reshape(bp, sp, ROPE_DIM),
                               aux_p[2].reshape(bp, sp, N_HEADS), aux_s[0].reshape(bs, ss, KV_LORA),
                               aux_s[1].reshape(bs, ss, ROPE_DIM), aux_s[2].reshape(bs, ss, N_HEADS),
                               aux_s[3].reshape(bs, ss, D_A))):
            o.append(a)
    return (y_p, y_s) + tuple(jnp.stack(o) for o in outs)
```

```python
import functools
import math

import jax
import jax.numpy as jnp
from jax import lax
from jax.experimental import pallas as pl
from jax.experimental.pallas import tpu as pltpu

F32 = jnp.float32
BF16 = jnp.bfloat16

D_MODEL = 2048
D_A = D_MODEL // 2
A_GROUP = 128
A_GROUPS = D_A // A_GROUP
CHUNK = 128
N_HEADS = 16
Q_LORA = D_MODEL // 4
KV_LORA = D_MODEL // 8
NOPE_DIM = 128
ROPE_DIM = 64
V_DIM = 128
QK_DIM = NOPE_DIM + ROPE_DIM
ROPE_THETA = 10000.0
SCALE = 1.0 / math.sqrt(QK_DIM)
LOG2_E = math.log2(math.e)
N_EXPERTS = 32
TOP_K = 4
D_FF = D_MODEL
SWIGLU_LIMIT = 7.0
SWIGLU_ALPHA = 1.702
EPS = 1e-6
NEG_INF = -1e30
PAGE_SIZE = 128

LANES = 128
MOE_ROWS = 512
GATHER_ROWS = 512
ROW_DMA_UNROLL = 8
COMBINE_TOKENS = 128
PAGES_PER_STEP = 64
DECODE_CHAINS = 4
PAGE_DMA_PRIORITY = (1, 0, 0)
VMEM_LIMIT = 56 * 1024 * 1024

ZO_U = 0
ZO_V = ZO_U + D_A
ZO_GA = ZO_V + D_A
ZO_GB = ZO_GA + D_MODEL
ZO_Q = ZO_GB + D_MODEL
ZO_KV = ZO_Q + Q_LORA
ZO_KR = ZO_KV + KV_LORA
Z_COLS = ZO_KR + 2 * ROPE_DIM


def _params(sem):
    return pltpu.CompilerParams(dimension_semantics=sem, vmem_limit_bytes=VMEM_LIMIT)


def _dot(a, b):
    return jnp.dot(a, b, preferred_element_type=F32)


def _dot_nt(a, b):
    return lax.dot_general(a, b, (((1,), (1,)), ((), ())), preferred_element_type=F32)


def _sigmoid(x):
    return 1.0 / (1.0 + jnp.exp(-x))


def _gelu(x):
    c = math.sqrt(2.0 / math.pi)
    return 0.5 * x * (1.0 + jnp.tanh(c * (x + 0.044715 * (x * x * x))))


def _rms(x):
    return x * lax.rsqrt(jnp.mean(x * x, axis=-1, keepdims=True) + EPS)


def _ada_kernel(c_ref, w_ref, b_ref, o_ref):
    c = c_ref[...]
    a = (c * _sigmoid(c)).astype(BF16)
    o_ref[...] = _dot(a, w_ref[...].astype(BF16)) + b_ref[...]


def _ada(c_all, w_ada, b_ada):
    rows, d = c_all.shape
    n = w_ada.shape[1]
    tn = 1024
    return pl.pallas_call(
        _ada_kernel,
        out_shape=jax.ShapeDtypeStruct((rows, n), F32),
        grid=(n // tn,),
        in_specs=[pl.BlockSpec((rows, d), lambda j: (0, 0)),
                  pl.BlockSpec((d, tn), lambda j: (0, j)),
                  pl.BlockSpec((1, tn), lambda j: (0, j))],
        out_specs=pl.BlockSpec((rows, tn), lambda j: (0, j)),
        compiler_params=_params(("arbitrary",)),
    )(c_all, w_ada, b_ada.reshape(1, n))


def _normmod_kernel(x_ref, g_ref, sc_ref, sh_ref, o_ref):
    x = x_ref[...]
    y = _rms(x) * g_ref[...]
    y = y * (1.0 + sc_ref[...]) + sh_ref[...]
    o_ref[...] = y.reshape(o_ref.shape).astype(o_ref.dtype)


def _normmod(x, g, mod3, sc_idx, sh_idx, bb, tt):
    b, t, d = x.shape
    return pl.pallas_call(
        _normmod_kernel,
        out_shape=jax.ShapeDtypeStruct((b * t, d), BF16),
        grid=(b // bb, t // tt),
        in_specs=[pl.BlockSpec((bb, tt, d), lambda i, j: (i, j, 0)),
                  pl.BlockSpec((1, 1, d), lambda i, j: (0, 0, 0)),
                  pl.BlockSpec((bb, 1, d), lambda i, j: (i, 0, sc_idx)),
                  pl.BlockSpec((bb, 1, d), lambda i, j: (i, 0, sh_idx))],
        out_specs=pl.BlockSpec((bb * tt, d), lambda i, j: (i * (t // tt) + j, 0)),
        compiler_params=_params(("arbitrary", "arbitrary")),
    )(x, g.reshape(1, 1, d), mod3, mod3)


def _mm_kernel(x_ref, w_ref, o_ref):
    o_ref[...] = _dot(x_ref[...], w_ref[...]).astype(o_ref.dtype)


def _mm(x, w, tm, tn, out_dtype=F32):
    m, k = x.shape
    n = w.shape[1]
    return pl.pallas_call(
        _mm_kernel,
        out_shape=jax.ShapeDtypeStruct((m, n), out_dtype),
        grid=(m // tm, n // tn),
        in_specs=[pl.BlockSpec((tm, k), lambda i, j: (i, 0)),
                  pl.BlockSpec((k, tn), lambda i, j: (0, j))],
        out_specs=pl.BlockSpec((tm, tn), lambda i, j: (i, j)),
        compiler_params=_params(("arbitrary", "arbitrary")),
    )(x, w)


def _gmlp_kernel(zu_ref, zv_ref, gln_ref, bln_ref, ws_ref, bs_ref, a_ref, *maybe_vn_ref, lc):
    u = _gelu(zu_ref[...])
    gv = _gelu(zv_ref[...])
    xc = gv - jnp.mean(gv, axis=-1, keepdims=True)
    vn = xc * lax.rsqrt(jnp.mean(xc * xc, axis=-1, keepdims=True) + EPS)
    vn = vn * gln_ref[...] + bln_ref[...]
    if maybe_vn_ref:
        maybe_vn_ref[0][...] = vn
    row = lax.broadcasted_iota(jnp.int32, (CHUNK, CHUNK), 0)
    col = lax.broadcasted_iota(jnp.int32, (CHUNK, CHUNK), 1)
    mask = (col <= row) & ((row // lc) == (col // lc))
    vb = vn.astype(BF16)
    for g in range(A_GROUPS):
        w = jnp.where(mask, ws_ref[g], 0.0).astype(BF16)
        sl = slice(g * A_GROUP, (g + 1) * A_GROUP)
        s = _dot(w, vb[:, sl]) + bs_ref[g]
        a_ref[:, sl] = (u[:, sl] * s).astype(a_ref.dtype)


def _gmlp(z, g_ln, b_ln, ws_t, bs_t, lc, want_vn):
    t = z.shape[0]
    out_shape = [jax.ShapeDtypeStruct((t, D_A), BF16)]
    out_specs = [pl.BlockSpec((CHUNK, D_A), lambda i: (i, 0))]
    if want_vn:
        out_shape.append(jax.ShapeDtypeStruct((t, D_A), F32))
        out_specs.append(pl.BlockSpec((CHUNK, D_A), lambda i: (i, 0)))
    return pl.pallas_call(
        functools.partial(_gmlp_kernel, lc=lc),
        out_shape=out_shape,
        grid=(t // CHUNK,),
        in_specs=[pl.BlockSpec((CHUNK, D_A), lambda i: (i, ZO_U // D_A)),
                  pl.BlockSpec((CHUNK, D_A), lambda i: (i, ZO_V // D_A)),
                  pl.BlockSpec((1, D_A), lambda i: (0, 0)),
                  pl.BlockSpec((1, D_A), lambda i: (0, 0)),
                  pl.BlockSpec((A_GROUPS, CHUNK, CHUNK), lambda i: (0, 0, 0)),
                  pl.BlockSpec((A_GROUPS, CHUNK, CHUNK), lambda i: (0, 0, 0))],
        out_specs=out_specs,
        compiler_params=_params(("arbitrary",)),
    )(z, z, g_ln.reshape(1, D_A), b_ln.reshape(1, D_A), ws_t, bs_t)


def _rope_pair(y2, gain2, tab):
    p = y2 * gain2 * tab
    return p + pltpu.roll(p, ROPE_DIM, 1)


def _mla_kernel(zq_ref, zkv_ref, zkr_ref, tab_ref, gql_ref, gkv_ref, gqn_ref, gq2_ref, gk2_ref, gkn_ref,
                wqn_ref, wqr_ref, wuk_ref, wuv_ref, ckv_ref, kpe_ref, rinv_ref, *qkv_refs, prompt):
    tab = tab_ref[...]
    q_scale = SCALE * LOG2_E
    q_lat = (_rms(zq_ref[...]) * gql_ref[...]).astype(BF16)
    c_kv = _rms(zkv_ref[...]) * gkv_ref[...]
    ckv_ref[...] = c_kv
    kr2 = zkr_ref[...]
    kr2 = kr2 * lax.rsqrt(jnp.sum(kr2 * kr2, axis=-1, keepdims=True) / (2 * ROPE_DIM) + EPS)
    kpe2 = _rope_pair(kr2, gk2_ref[...], tab)
    kpe_ref[...] = kpe2[:, :ROPE_DIM]
    cb = c_kv.astype(BF16)
    tm = cb.shape[0]
    lane = lax.broadcasted_iota(jnp.int32, (tm, N_HEADS), 1)
    rinv_all = jnp.zeros((tm, N_HEADS), F32)
    if prompt:
        col = lax.broadcasted_iota(jnp.int32, kpe2.shape, 1)
        kp_b = jnp.where(col < ROPE_DIM, kpe2, 0.0).astype(BF16)
    for h in range(N_HEADS):
        sl = slice(h * NOPE_DIM, (h + 1) * NOPE_DIM)
        qn = _rms(_dot(q_lat, wqn_ref[:, sl])) * gqn_ref[...]
        qr2 = _dot(q_lat, wqr_ref[:, sl])
        qr2 = qr2 * lax.rsqrt(jnp.sum(qr2 * qr2, axis=-1, keepdims=True) / (2 * ROPE_DIM) + EPS)
        qp = (_rope_pair(qr2, gq2_ref[...], tab) * q_scale).astype(BF16)
        k_raw = _dot(cb, wuk_ref[:, sl])
        rinv = lax.rsqrt(jnp.mean(k_raw * k_raw, axis=-1, keepdims=True) + EPS)
        rinv_all = jnp.where(lane == h, rinv, rinv_all)
        if prompt:
            q_ref, k_ref, v_ref = qkv_refs
            lo = slice(2 * h * NOPE_DIM, (2 * h + 1) * NOPE_DIM)
            hi = slice((2 * h + 1) * NOPE_DIM, (2 * h + 2) * NOPE_DIM)
            q_ref[:, lo] = (qn * q_scale).astype(BF16)
            q_ref[:, hi] = qp
            k_ref[:, lo] = (k_raw * rinv * gkn_ref[...]).astype(BF16)
            k_ref[:, hi] = kp_b
            v_ref[sl, :] = _dot_nt(wuv_ref[sl, :], cb).astype(BF16)
        else:
            qn_ref, qp_ref = qkv_refs
            qn_ref[:, sl] = (qn * gkn_ref[...]).astype(BF16)
            qp_ref[:, sl] = qp
    rinv_ref[...] = rinv_all


def _mla(z, tab, gains, weights, tm, prompt):
    t = z.shape[0]
    hd = N_HEADS * NOPE_DIM
    row = lambda w: pl.BlockSpec((tm, w), lambda i: (i, 0))
    full = lambda a: pl.BlockSpec(a.shape, lambda i: (0,) * a.ndim)
    out_shape = [jax.ShapeDtypeStruct((t, KV_LORA), F32), jax.ShapeDtypeStruct((t, ROPE_DIM), F32),
                 jax.ShapeDtypeStruct((t, N_HEADS), F32)]
    out_specs = [row(KV_LORA), row(ROPE_DIM), row(N_HEADS)]
    widths = (2 * hd, 2 * hd) if prompt else (hd, hd)
    out_shape += [jax.ShapeDtypeStruct((t, w), BF16) for w in widths]
    out_specs += [row(w) for w in widths]
    if prompt:
        out_shape.append(jax.ShapeDtypeStruct((hd, t), BF16))
        out_specs.append(pl.BlockSpec((hd, tm), lambda i: (0, i)))
    return pl.pallas_call(
        functools.partial(_mla_kernel, prompt=prompt),
        out_shape=out_shape,
        grid=(t // tm,),
        in_specs=[pl.BlockSpec((tm, Q_LORA), lambda i: (i, ZO_Q // Q_LORA)),
                  pl.BlockSpec((tm, KV_LORA), lambda i: (i, ZO_KV // KV_LORA)),
                  pl.BlockSpec((tm, LANES), lambda i: (i, ZO_KR // LANES)),
                  row(LANES)] + [full(a) for a in gains] + [full(a) for a in weights],
        out_specs=out_specs,
        compiler_params=_params(("arbitrary",)),
    )(z, z, z, tab, *gains, *weights)


def _fa_kernel(q_ref, k_ref, vt_ref, o_ref, sa_ref, sb_ref, m_ref, l_ref, acc_ref, *, tq, tk):
    i = pl.program_id(1)
    q = q_ref[...]
    m_ref[...] = jnp.full(m_ref.shape, NEG_INF, F32)
    l_ref[...] = jnp.zeros(l_ref.shape, F32)
    acc_ref[...] = jnp.zeros(acc_ref.shape, F32)

    def keys(j):
        return pl.ds(pl.multiple_of(j * tk, tk), tk)

    def qk(j):
        return _dot_nt(k_ref[keys(j), :], q)

    def update(s, j):
        m_old = m_ref[...]
        m_new = jnp.maximum(m_old, jnp.max(s, axis=0, keepdims=True))
        corr = jnp.exp2(m_old - m_new)
        p = jnp.exp2(s - m_new)
        l_ref[...] = l_ref[...] * corr + jnp.sum(p, axis=0, keepdims=True)
        m_ref[...] = m_new
        acc_ref[...] = acc_ref[...] * corr + _dot(vt_ref[:, keys(j)], p.astype(BF16))

    sa_ref[...] = qk(0)

    def pair(u, carry):
        sb_ref[...] = qk(2 * u + 1)
        update(sa_ref[...], 2 * u)
        sa_ref[...] = qk(2 * u + 2)
        update(sb_ref[...], 2 * u + 1)
        return carry

    lax.fori_loop(0, i, pair, 0)
    sb_ref[...] = qk(2 * i + 1)
    key = lax.broadcasted_iota(jnp.int32, (tk, tq), 0)
    qry = lax.broadcasted_iota(jnp.int32, (tk, tq), 1)
    update(jnp.where(key <= qry, sa_ref[...], NEG_INF), 2 * i)
    update(jnp.where(key + tk <= qry, sb_ref[...], NEG_INF), 2 * i + 1)
    o_ref[...] = (acc_ref[...] / l_ref[...]).T.astype(o_ref.dtype)


def _attend_prompt(q, k, vt, tq):
    t = q.shape[0]
    tk = tq // 2
    return pl.pallas_call(
        functools.partial(_fa_kernel, tq=tq, tk=tk),
        out_shape=jax.ShapeDtypeStruct((t, N_HEADS * V_DIM), BF16),
        grid=(N_HEADS, t // tq),
        in_specs=[pl.BlockSpec((tq, 2 * NOPE_DIM), lambda h, i: (i, h)),
                  pl.BlockSpec((t, 2 * NOPE_DIM), lambda h, i: (0, h)),
                  pl.BlockSpec((V_DIM, t), lambda h, i: (h, 0))],
        out_specs=pl.BlockSpec((tq, V_DIM), lambda h, i: (i, h)),
        scratch_shapes=[pltpu.VMEM((tk, tq), F32), pltpu.VMEM((tk, tq), F32),
                        pltpu.VMEM((1, tq), F32), pltpu.VMEM((1, tq), F32), pltpu.VMEM((V_DIM, tq), F32)],
        compiler_params=_params(("arbitrary", "arbitrary")),
    )(q, k, vt)


def _absorb_kernel(qn_ref, wuk_ref, qa_ref):
    qa_ref[...] = (_dot_nt(qn_ref[...], wuk_ref[...]) * (SCALE * LOG2_E)).astype(qa_ref.dtype)


def _absorb(qn, w_uk_b):
    tokens = qn.shape[0]
    return pl.pallas_call(
        _absorb_kernel,
        out_shape=jax.ShapeDtypeStruct((tokens, N_HEADS * KV_LORA), BF16),
        grid=(N_HEADS,),
        in_specs=[pl.BlockSpec((tokens, NOPE_DIM), lambda h: (0, h)),
                  pl.BlockSpec((KV_LORA, NOPE_DIM), lambda h: (0, h))],
        out_specs=pl.BlockSpec((tokens, KV_LORA), lambda h: (0, h)),
        compiler_params=_params(("arbitrary",)),
    )(qn, w_uk_b)


def _decode_kernel(pt_ref, qa_ref, qr_ref, nlat_ref, nrope_ref, nrinv_ref, lat_hbm, rope_hbm, rinv_hbm, o_ref,
                   lat_buf, rope_buf, rinv_buf, sems, ck_s, kr_s, ri_s, m_ref, l_ref, acc_ref, *, t_new):
    pc = PAGES_PER_STEP
    b, c = pl.program_id(0), pl.program_id(1)
    nb, nc = pl.num_programs(0), pl.num_programs(1)
    n = b * nc + c
    slot = n % 2
    rows = N_HEADS * t_new

    def page_copies(bi, ci, sl):
        copies = []
        for j in range(pc):
            page = pt_ref[bi, ci * pc + j]
            copies.append(pltpu.make_async_copy(lat_hbm.at[page], lat_buf.at[sl, j], sems.at[sl, 0]))
            copies.append(pltpu.make_async_copy(rope_hbm.at[page], rope_buf.at[sl, j], sems.at[sl, 1]))
            copies.append(pltpu.make_async_copy(rinv_hbm.at[page], rinv_buf.at[sl, j], sems.at[sl, 2]))
        return copies

    @pl.when(n == 0)
    def _():
        for q, cp in enumerate(page_copies(b, c, slot)):
            cp.start(priority=PAGE_DMA_PRIORITY[q % 3])

    @pl.when(n + 1 < nb * nc)
    def _():
        wrap = c + 1 == nc
        for q, cp in enumerate(page_copies(jnp.where(wrap, b + 1, b), jnp.where(wrap, 0, c + 1), 1 - slot)):
            cp.start(priority=PAGE_DMA_PRIORITY[q % 3])

    for cp in page_copies(b, c, slot):
        cp.wait()
    qa = qa_ref[...]
    qr = qr_ref[...][:, :ROPE_DIM]

    def scores(ck, krt, rit):
        n = ck.shape[0]
        return (_dot_nt(qa, ck).reshape(t_new, N_HEADS, n) * rit[None]).reshape(rows, n) + _dot(qr, krt)

    def fold(g, s, ck):
        m_old = m_ref[g]
        m_new = jnp.maximum(m_old, jnp.max(s, axis=-1, keepdims=True))
        corr = jnp.exp2(m_old - m_new)
        p = jnp.exp2(s - m_new)
        l_ref[g] = l_ref[g] * corr + jnp.sum(p, axis=-1, keepdims=True)
        m_ref[g] = m_new
        acc_ref[g] = acc_ref[g] * corr + _dot(p.astype(BF16), ck)

    @pl.when(c == 0)
    def _():
        m_ref[...] = jnp.full(m_ref.shape, NEG_INF, F32)
        l_ref[...] = jnp.zeros(l_ref.shape, F32)
        acc_ref[...] = jnp.zeros(acc_ref.shape, F32)
        ck = nlat_ref[...].astype(BF16)
        s = scores(ck, nrope_ref[...].astype(BF16), nrinv_ref[...])
        row = lax.broadcasted_iota(jnp.int32, s.shape, 0)
        col = lax.broadcasted_iota(jnp.int32, s.shape, 1)
        fold(0, jnp.where(col <= row // N_HEADS, s, NEG_INF), ck)

    for j in range(pc):
        ck_s[j * PAGE_SIZE:(j + 1) * PAGE_SIZE, :] = lat_buf[slot, j].astype(BF16)
        kr_s[:, j * PAGE_SIZE:(j + 1) * PAGE_SIZE] = rope_buf[slot, j].astype(BF16)
        ri_s[:, j * PAGE_SIZE:(j + 1) * PAGE_SIZE] = rinv_buf[slot, j]
    keys = pc * PAGE_SIZE // DECODE_CHAINS
    cks = [ck_s[g * keys:(g + 1) * keys, :] for g in range(DECODE_CHAINS)]
    ss = [scores(cks[g], kr_s[:, g * keys:(g + 1) * keys], ri_s[:, g * keys:(g + 1) * keys])
          for g in range(DECODE_CHAINS)]
    for g in range(DECODE_CHAINS):
        fold(g, ss[g], cks[g])

    @pl.when(c == nc - 1)
    def _():
        m = m_ref[0]
        for g in range(1, DECODE_CHAINS):
            m = jnp.maximum(m, m_ref[g])
        l = jnp.zeros(m.shape, F32)
        acc = jnp.zeros(acc_ref.shape[1:], F32)
        for g in range(DECODE_CHAINS):
            w = jnp.exp2(m_ref[g] - m)
            l = l + l_ref[g] * w
            acc = acc + acc_ref[g] * w
        o_ref[...] = (acc / l).astype(o_ref.dtype)


def _decode(page_table, qa, qr, nlat, nrope_t, nrinv_t, cache_latent, cache_rope_t, cache_rinv_t, t_new):
    b, n_pages = page_table.shape
    pc = PAGES_PER_STEP
    rows = N_HEADS * t_new
    seq = lambda r, w: pl.BlockSpec((None, r, w), lambda i, c, pt: (i, 0, 0))
    hbm = pl.BlockSpec(memory_space=pl.ANY)
    keys = pc * PAGE_SIZE
    grid_spec = pltpu.PrefetchScalarGridSpec(
        num_scalar_prefetch=1,
        grid=(b, n_pages // pc),
        in_specs=[seq(rows, KV_LORA), seq(rows, LANES),
                  seq(PAGE_SIZE, KV_LORA), seq(ROPE_DIM, PAGE_SIZE), seq(N_HEADS, PAGE_SIZE), hbm, hbm, hbm],
        out_specs=seq(rows, KV_LORA),
        scratch_shapes=[pltpu.VMEM((2, pc, PAGE_SIZE, KV_LORA), F32), pltpu.VMEM((2, pc, ROPE_DIM, PAGE_SIZE), F32),
                        pltpu.VMEM((2, pc, N_HEADS, PAGE_SIZE), F32), pltpu.SemaphoreType.DMA((2, 3)),
                        pltpu.VMEM((keys, KV_LORA), BF16), pltpu.VMEM((ROPE_DIM, keys), BF16),
                        pltpu.VMEM((N_HEADS, keys), F32),
                        pltpu.VMEM((DECODE_CHAINS, rows, 1), F32), pltpu.VMEM((DECODE_CHAINS, rows, 1), F32),
                        pltpu.VMEM((DECODE_CHAINS, rows, KV_LORA), F32)],
    )
    return pl.pallas_call(
        functools.partial(_decode_kernel, t_new=t_new),
        out_shape=jax.ShapeDtypeStruct((b, rows, KV_LORA), F32),
        grid_spec=grid_spec,
        compiler_params=_params(("arbitrary", "arbitrary")),
    )(page_table, qa, qr, nlat, nrope_t, nrinv_t, cache_latent, cache_rope_t, cache_rinv_t)


def _upv_kernel(o_ref, wuv_ref, out_ref):
    out_ref[...] = _dot(o_ref[...].astype(BF16), wuv_ref[...]).astype(out_ref.dtype)


def _upv(o_lat, w_uv_b):
    tokens = o_lat.shape[0]
    return pl.pallas_call(
        _upv_kernel,
        out_shape=jax.ShapeDtypeStruct((tokens, N_HEADS * V_DIM), BF16),
        grid=(N_HEADS,),
        in_specs=[pl.BlockSpec((tokens, KV_LORA), lambda h: (0, h)),
                  pl.BlockSpec((KV_LORA, V_DIM), lambda h: (0, h))],
        out_specs=pl.BlockSpec((tokens, V_DIM), lambda h: (0, h)),
        compiler_params=_params(("arbitrary",)),
    )(o_lat, w_uv_b)


def _merge_kernel(a_ref, b_ref, wa_ref, wb_ref, ga_ref, gb_ref, o_ref):
    pa = _dot(a_ref[...], wa_ref[...])
    pb = _dot(b_ref[...], wb_ref[...])
    o_ref[...] = (_sigmoid(ga_ref[...]) * pa + _sigmoid(gb_ref[...]) * pb).astype(o_ref.dtype)


def _merge(a_out, b_out, wpa, wpb, z, tm, tn):
    t = a_out.shape[0]
    return pl.pallas_call(
        _merge_kernel,
        out_shape=jax.ShapeDtypeStruct((t, D_MODEL), BF16),
        grid=(t // tm, D_MODEL // tn),
        in_specs=[pl.BlockSpec((tm, D_A), lambda i, j: (i, 0)),
                  pl.BlockSpec((tm, D_MODEL), lambda i, j: (i, 0)),
                  pl.BlockSpec((D_A, tn), lambda i, j: (0, j)),
                  pl.BlockSpec((D_MODEL, tn), lambda i, j: (0, j)),
                  pl.BlockSpec((tm, tn), lambda i, j: (i, ZO_GA // tn + j)),
                  pl.BlockSpec((tm, tn), lambda i, j: (i, ZO_GB // tn + j))],
        out_specs=pl.BlockSpec((tm, tn), lambda i, j: (i, j)),
        compiler_params=_params(("arbitrary", "arbitrary")),
    )(a_out, b_out, wpa, wpb, z, z)


def _outproj_kernel(m_ref, wo_ref, x_ref, g1_ref, gn_ref, sc_ref, sh_ref, wr_ref, br_ref,
                    x1_ref, h2_ref, idx_ref, gate_ref):
    shape3 = x_ref.shape
    y = _dot(m_ref[...], wo_ref[...])
    x1 = x_ref[...] + g1_ref[...] * y.reshape(shape3)
    x1_ref[...] = x1
    h2 = _rms(x1) * gn_ref[...]
    h2 = (h2 * (1.0 + sc_ref[...]) + sh_ref[...]).reshape(y.shape)
    h2_ref[...] = h2
    logits = _dot(h2.astype(BF16), wr_ref[...]) + br_ref[...]
    lane = lax.broadcasted_iota(jnp.int32, logits.shape, 1).astype(F32)
    vals, idxs = [], []
    for _ in range(TOP_K):
        mx = jnp.max(logits, axis=-1, keepdims=True)
        am = jnp.min(jnp.where(logits == mx, lane, float(LANES)), axis=-1, keepdims=True)
        vals.append(mx)
        idxs.append(am)
        logits = jnp.where(lane == am, -3.0e38, logits)
    es = [jnp.exp(v - vals[0]) for v in vals]
    den = es[0] + es[1] + es[2] + es[3]
    idx_out = jnp.zeros(lane.shape, F32)
    gate_out = jnp.zeros(lane.shape, F32)
    for k in range(TOP_K):
        idx_out = jnp.where(lane == float(k), idxs[k], idx_out)
        gate_out = jnp.where(lane == float(k), es[k] / den, gate_out)
    idx_ref[...] = idx_out.astype(jnp.int32)
    gate_ref[...] = gate_out


def _outproj(merged, w_out_b, x, mod3, g_norm2, wr_pad, br_pad, bb, tt):
    b, t, d = x.shape
    tm = bb * tt
    nt = t // tt
    tok = lambda w: pl.BlockSpec((tm, w), lambda i, j: (i * nt + j, 0))
    modspec = lambda k: pl.BlockSpec((bb, 1, d), lambda i, j: (i, 0, k))
    return pl.pallas_call(
        _outproj_kernel,
        out_shape=[jax.ShapeDtypeStruct((b, t, d), F32), jax.ShapeDtypeStruct((b * t, d), F32),
                   jax.ShapeDtypeStruct((b * t, LANES), jnp.int32), jax.ShapeDtypeStruct((b * t, LANES), F32)],
        grid=(b // bb, nt),
        in_specs=[tok(d),
                  pl.BlockSpec((d, d), lambda i, j: (0, 0)),
                  pl.BlockSpec((bb, tt, d), lambda i, j: (i, j, 0)),
                  modspec(2),
                  pl.BlockSpec((1, 1, d), lambda i, j: (0, 0, 0)),
                  modspec(4), modspec(3),
                  pl.BlockSpec((d, LANES), lambda i, j: (0, 0)),
                  pl.BlockSpec((1, LANES), lambda i, j: (0, 0))],
        out_specs=[pl.BlockSpec((bb, tt, d), lambda i, j: (i, j, 0)), tok(d), tok(LANES), tok(LANES)],
        compiler_params=_params(("arbitrary", "arbitrary")),
    )(merged, w_out_b, x, mod3, g_norm2.reshape(1, 1, d), mod3, mod3, wr_pad, br_pad)


def _row_copy(src_ref, dst_ref, src_row, dst_row, sem):
    return pltpu.make_async_copy(src_ref.at[pl.ds(src_row, 1)], dst_ref.at[pl.ds(dst_row, 1)], sem)


def _dispatch_kernel(tok_ref, used_ref, h_ref, o_ref, buf, sem):
    n = buf.shape[0]
    base = pl.program_id(0) * n
    live = base < used_ref[0] * MOE_ROWS

    @pl.when(live)
    def _():
        def start(g, carry):
            for k in range(ROW_DMA_UNROLL):
                r = g * ROW_DMA_UNROLL + k
                _row_copy(h_ref, buf, tok_ref[base + r], r, sem).start(priority=k % 2)
            return carry

        lax.fori_loop(0, n // ROW_DMA_UNROLL, start, 0)
        pltpu.make_async_copy(h_ref.at[pl.ds(0, n)], buf, sem).wait()
        o_ref[...] = buf[...].astype(o_ref.dtype)

    @pl.when(jnp.logical_not(live))
    def _():
        o_ref[...] = jnp.zeros(o_ref.shape, o_ref.dtype)


def _dispatch(row_tok, n_used, h2):
    n_rows = row_tok.shape[0]
    d = h2.shape[1]
    return pl.pallas_call(
        _dispatch_kernel,
        out_shape=jax.ShapeDtypeStruct((n_rows, d), BF16),
        grid_spec=pltpu.PrefetchScalarGridSpec(
            num_scalar_prefetch=2,
            grid=(n_rows // GATHER_ROWS,),
            in_specs=[pl.BlockSpec(memory_space=pl.ANY)],
            out_specs=pl.BlockSpec((GATHER_ROWS, d), lambda i, tok, nu: (i, 0)),
            scratch_shapes=[pltpu.VMEM((GATHER_ROWS, d), F32), pltpu.SemaphoreType.DMA(())],
        ),
        compiler_params=_params(("arbitrary",)),
    )(row_tok, n_used, h2)


def _by_live_rows(rows, x_ref, o_ref, fn):
    half = MOE_ROWS // 2

    @pl.when(rows > half)
    def _():
        o_ref[...] = fn(x_ref[...])

    @pl.when((rows > 0) & (rows <= half))
    def _():
        o_ref[:half, :] = fn(x_ref[:half, :])
        o_ref[half:, :] = jnp.zeros((MOE_ROWS - half, o_ref.shape[1]), o_ref.dtype)

    @pl.when(rows == 0)
    def _():
        o_ref[...] = jnp.zeros(o_ref.shape, o_ref.dtype)


def _moe_up_kernel(be_ref, nxt_ref, cnt_ref, x_ref, bg_ref, bl_ref, w_hbm, o_ref, wbuf, sems, slot_ref, wg_s, wl_s, *, tf, nf):
    j, i = pl.program_id(0), pl.program_id(1)
    e = be_ref[i]

    def copies(ee, jj, sl):
        return [pltpu.make_async_copy(w_hbm.at[ee, :, pl.ds(pl.multiple_of((g * nf + jj) * tf, tf), tf)],
                                      wbuf.at[sl, g], sems.at[sl, g]) for g in range(2)]

    @pl.when((j == 0) & (i == 0))
    def _():
        slot_ref[0] = 0
        for cp in copies(e, j, 0):
            cp.start()

    @pl.when((i == 0) | (e != be_ref[jnp.maximum(i - 1, 0)]))
    def _():
        sl = slot_ref[0]
        for cp in copies(e, j, sl):
            cp.wait()
        nx = nxt_ref[i]
        more = nx >= 0

        @pl.when(more | (j + 1 < nf))
        def _():
            for cp in copies(jnp.where(more, nx, be_ref[0]), jnp.where(more, j, j + 1), 1 - sl):
                cp.start()

        wg_s[...] = wbuf[sl, 0].astype(BF16)
        wl_s[...] = wbuf[sl, 1].astype(BF16)
        slot_ref[0] = 1 - sl

    def act(x):
        glu = jnp.minimum(_dot(x, wg_s[...]) + bg_ref[...], SWIGLU_LIMIT)
        lin = jnp.clip(_dot(x, wl_s[...]) + bl_ref[...], -SWIGLU_LIMIT, SWIGLU_LIMIT)
        return (glu * _sigmoid(SWIGLU_ALPHA * glu) * (lin + 1.0)).astype(o_ref.dtype)

    _by_live_rows(cnt_ref[i], x_ref, o_ref, act)


def _moe_down_kernel(be_ref, nxt_ref, cnt_ref, a_ref, b_ref, w_hbm, o_ref, wbuf, sems, slot_ref, w_s):
    i = pl.program_id(0)
    e = be_ref[i]

    def copy(ee, sl):
        return pltpu.make_async_copy(w_hbm.at[ee], wbuf.at[sl], sems.at[sl])

    @pl.when(i == 0)
    def _():
        slot_ref[0] = 0
        copy(e, 0).start()

    @pl.when((i == 0) | (e != be_ref[jnp.maximum(i - 1, 0)]))
    def _():
        sl = slot_ref[0]
        copy(e, sl).wait()
        nx = nxt_ref[i]

        @pl.when(nx >= 0)
        def _():
            copy(nx, 1 - sl).start()

        w_s[...] = wbuf[sl].astype(BF16)
        slot_ref[0] = 1 - sl

    _by_live_rows(cnt_ref[i], a_ref, o_ref, lambda a: _dot(a, w_s[...]) + b_ref[...])


def _moe_experts(block_e, block_rows, x_sorted, w_gate_up, b_gate_up, w_down, b_down):
    n_rows, d = x_sorted.shape
    n_blocks = n_rows // MOE_ROWS
    tf = 1024
    nf = D_FF // tf
    ids = jnp.arange(N_EXPERTS, dtype=jnp.int32)
    present = (block_e[None, :] == ids[:, None]).any(axis=1)
    later = jnp.where(present[None, :] & (ids[None, :] > ids[:, None]), ids[None, :], N_EXPERTS).min(axis=1)
    nxt = jnp.where(later == N_EXPERTS, -1, later).astype(jnp.int32)[block_e]
    bgu3 = b_gate_up.reshape(N_EXPERTS, 1, 2 * D_FF)
    hbm = pl.BlockSpec(memory_space=pl.ANY)
    act = pl.pallas_call(
        functools.partial(_moe_up_kernel, tf=tf, nf=nf),
        out_shape=jax.ShapeDtypeStruct((n_rows, D_FF), BF16),
        grid_spec=pltpu.PrefetchScalarGridSpec(
            num_scalar_prefetch=3,
            grid=(nf, n_blocks),
            in_specs=[pl.BlockSpec((MOE_ROWS, d), lambda j, i, be, nx, nu: (i, 0)),
                      pl.BlockSpec((None, 1, tf), lambda j, i, be, nx, nu: (be[i], 0, j)),
                      pl.BlockSpec((None, 1, tf), lambda j, i, be, nx, nu: (be[i], 0, nf + j)),
                      hbm],
            out_specs=pl.BlockSpec((MOE_ROWS, tf), lambda j, i, be, nx, nu: (i, j)),
            scratch_shapes=[pltpu.VMEM((2, 2, d, tf), F32), pltpu.SemaphoreType.DMA((2, 2)),
                            pltpu.SMEM((1,), jnp.int32), pltpu.VMEM((d, tf), BF16), pltpu.VMEM((d, tf), BF16)],
        ),
        compiler_params=_params(("arbitrary", "arbitrary")),
    )(block_e, nxt, block_rows, x_sorted, bgu3, bgu3, w_gate_up)
    return pl.pallas_call(
        _moe_down_kernel,
        out_shape=jax.ShapeDtypeStruct((n_rows, d), F32),
        grid_spec=pltpu.PrefetchScalarGridSpec(
            num_scalar_prefetch=3,
            grid=(n_blocks,),
            in_specs=[pl.BlockSpec((MOE_ROWS, D_FF), lambda i, be, nx, nu: (i, 0)),
                      pl.BlockSpec((None, 1, d), lambda i, be, nx, nu: (be[i], 0, 0)),
                      hbm],
            out_specs=pl.BlockSpec((MOE_ROWS, d), lambda i, be, nx, nu: (i, 0)),
            scratch_shapes=[pltpu.VMEM((2, D_FF, d), F32), pltpu.SemaphoreType.DMA((2,)),
                            pltpu.SMEM((1,), jnp.int32), pltpu.VMEM((D_FF, d), BF16)],
        ),
        compiler_params=_params(("arbitrary",)),
    )(block_e, nxt, block_rows, act, b_down.reshape(N_EXPERTS, 1, d), w_down)


def _combine_kernel(dest_ref, x1_ref, g2_ref, gate_ref, rows_ref, o_ref, buf, sem):
    tm = buf.shape[1]
    nt = pl.num_programs(1)
    base = (pl.program_id(0) * nt + pl.program_id(1)) * (tm * TOP_K)

    def start(t, carry):
        for k in range(TOP_K):
            _row_copy(rows_ref, buf.at[k], dest_ref[base + t * TOP_K + k], t, sem).start(priority=k % 2)
        return carry

    lax.fori_loop(0, tm, start, 0, unroll=2)
    for k in range(TOP_K):
        pltpu.make_async_copy(rows_ref.at[pl.ds(0, tm)], buf.at[k], sem).wait()
    gates = gate_ref[...]
    ff = buf[0] * gates[:, 0:1]
    for k in range(1, TOP_K):
        ff = ff + buf[k] * gates[:, k:k + 1]
    o_ref[...] = x1_ref[...] + g2_ref[...] * ff.reshape(x1_ref.shape)


def _combine(dest, x1, mod3, gates, out_rows, bb, tt):
    b, t, d = x1.shape
    nt = t // tt
    tm = bb * tt
    return pl.pallas_call(
        _combine_kernel,
        out_shape=jax.ShapeDtypeStruct((b, t, d), F32),
        grid_spec=pltpu.PrefetchScalarGridSpec(
            num_scalar_prefetch=1,
            grid=(b // bb, nt),
            in_specs=[pl.BlockSpec((bb, tt, d), lambda i, j, ds: (i, j, 0)),
                      pl.BlockSpec((bb, 1, d), lambda i, j, ds: (i, 0, 5)),
                      pl.BlockSpec((tm, LANES), lambda i, j, ds: (i * nt + j, 0)),
                      pl.BlockSpec(memory_space=pl.ANY)],
            out_specs=pl.BlockSpec((bb, tt, d), lambda i, j, ds: (i, j, 0)),
            scratch_shapes=[pltpu.VMEM((TOP_K, tm, d), F32), pltpu.SemaphoreType.DMA(())],
        ),
        compiler_params=_params(("arbitrary", "arbitrary")),
    )(dest, x1, mod3, gates, out_rows)


def _route(top_idx):
    t = top_idx.shape[0]
    n_assign = t * TOP_K
    flat_e = top_idx[:, :TOP_K].reshape(-1)
    flat_tok = jnp.arange(n_assign, dtype=jnp.int32) // TOP_K
    onehot = (flat_e[:, None] == jnp.arange(N_EXPERTS, dtype=jnp.int32)[None, :]).astype(jnp.int32)
    running = jnp.cumsum(onehot, axis=0)
    counts = running[-1]
    padded = (counts + MOE_ROWS - 1) // MOE_ROWS * MOE_ROWS
    pad_end = jnp.cumsum(padded)
    pad_start = pad_end - padded
    dest = jnp.sum(onehot * (running - 1 + pad_start[None, :]), axis=1)
    n_blocks = -(-(n_assign + N_EXPERTS * (MOE_ROWS - 1)) // MOE_ROWS)
    n_blocks = -(-n_blocks * MOE_ROWS // GATHER_ROWS) * GATHER_ROWS // MOE_ROWS
    row_tok = jnp.zeros((n_blocks * MOE_ROWS,), jnp.int32).at[dest].set(flat_tok, unique_indices=True)
    block_start = jnp.arange(n_blocks, dtype=jnp.int32) * MOE_ROWS
    block_e = jnp.minimum(jnp.sum((pad_end[None, :] <= block_start[:, None]).astype(jnp.int32), axis=1),
                          N_EXPERTS - 1)
    n_used = (pad_end[-1:] // MOE_ROWS).astype(jnp.int32)
    block_rows = jnp.clip(counts[block_e] - (block_start - pad_start[block_e]), 0, MOE_ROWS).astype(jnp.int32)
    return row_tok, block_e, block_rows, n_used, dest


def _mixers(x, mod, pos, p, *, prompt, caches=None, page_table=None):
    b, t, d = x.shape
    tokens = b * t
    mod3 = mod.reshape(b, 1, 6 * d)
    bb, tt = (1, 512) if prompt else (32, t)
    h = _normmod(x, p['g_norm1'], mod3, 1, 0, bb, tt)
    z = _mm(h, p['w_in'], min(tokens, 1024), Z_COLS // 5)
    lc = min(t, CHUNK)
    reps = CHUNK // lc
    ws_t = jnp.tile(p['w_s'][:, :lc, :lc], (1, reps, reps))
    bs_t = jnp.broadcast_to(jnp.tile(p['b_s'][:, :lc], (1, reps))[:, :, None], (A_GROUPS, CHUNK, CHUNK))
    gm = _gmlp(z, p['g_v_ln'], p['b_v_ln'], ws_t, bs_t, lc, want_vn=not prompt)
    a_out = gm[0]

    half = ROPE_DIM // 2
    inv_freq = jnp.exp(-math.log(ROPE_THETA) * jnp.arange(half, dtype=F32) / half)
    ang = pos.astype(F32)[:, None] * inv_freq[None, :]
    cos, sin = jnp.cos(ang), jnp.sin(ang)
    tab = jnp.concatenate([cos, cos, -sin, sin], axis=-1)
    tab = jnp.broadcast_to(tab[None], (b, t, LANES)).reshape(tokens, LANES)
    swap = lambda g: jnp.concatenate([g[half:], g[:half]])
    gains = [p['g_q_lat'].reshape(1, -1), p['g_kv_lat'].reshape(1, -1), p['g_q_nope'].reshape(1, -1),
             jnp.concatenate([p['g_q_rope'], swap(p['g_q_rope'])]).reshape(1, -1),
             jnp.concatenate([p['g_k_rope'], swap(p['g_k_rope'])]).reshape(1, -1),
             p['g_k_nope'].reshape(1, -1)]
    weights = [p['wq_n'], p['wq_r'], p['w_uk'], p['w_uv'].T if prompt else p['w_uv']]
    mla = _mla(z, tab, gains, weights, 256, prompt)
    c_kv, k_pe, rinv = mla[:3]
    if prompt:
        b_out = _attend_prompt(mla[3], mla[4], mla[5], min(t, 1024))
    else:
        cache_latent, cache_rope_t, cache_rinv_t = caches
        rows = N_HEADS * t
        qa = _absorb(mla[3], p['w_uk']).reshape(b, rows, KV_LORA)
        qr = mla[4].reshape(b, rows, LANES)
        pad = PAGE_SIZE - t
        nlat = jnp.pad(c_kv.reshape(b, t, KV_LORA), ((0, 0), (0, pad), (0, 0)))
        nrope_t = jnp.pad(jnp.swapaxes(k_pe.reshape(b, t, ROPE_DIM), 1, 2), ((0, 0), (0, 0), (0, pad)))
        nrinv_t = jnp.pad(jnp.swapaxes(rinv.reshape(b, t, N_HEADS), 1, 2), ((0, 0), (0, 0), (0, pad)))
        o_lat = _decode(page_table, qa, qr, nlat, nrope_t, nrinv_t, cache_latent, cache_rope_t, cache_rinv_t, t)
        b_out = _upv(o_lat.reshape(tokens, N_HEADS * KV_LORA), p['w_uv'])
    merged = _merge(a_out, b_out, p['w_proj_a'], p['w_proj_b'], z, min(tokens, 512), 1024)
    bb2, tt2 = (1, 256) if prompt else (32, t)
    x1, h2, top_idx, gates = _outproj(merged, p['w_out'], x, mod3, p['g_norm2'], p['wr_pad'], p['br_pad'], bb2, tt2)
    vn = None if prompt else gm[1]
    return (x1, h2, top_idx, gates, mod3), (c_kv, k_pe, rinv, vn)


def kernel(x_prompt, x_sample, c_prompt, c_sample, cache_latent, cache_k_rope, cache_k_rinv, page_table,
           w_ada, b_ada, g_norm1, w_in, g_v_ln, b_v_ln, w_s, b_s, g_q_lat, w_uq, g_q_nope, g_q_rope,
           g_kv_lat, g_k_rope, w_uk, w_uv, g_k_nope, w_proj_a, w_proj_b, w_out, g_norm2,
           w_router, b_router, w_gate_up, b_gate_up, w_down, b_down):
    depth = w_ada.shape[0]
    bp, sp, d = x_prompt.shape
    bs, ss, _ = x_sample.shape
    past_len = page_table.shape[1] * PAGE_SIZE
    pos_p = jnp.arange(sp, dtype=jnp.int32)
    pos_s = past_len + jnp.arange(ss, dtype=jnp.int32)
    half = ROPE_DIM // 2
    y_p, y_s = x_prompt, x_sample
    outs = [[] for _ in range(7)]
    for l in range(depth):
        wi = w_in[l]
        kr = wi[:, 2816:2880]
        w_in_perm = jnp.concatenate(
            [wi[:, 0:2048], wi[:, 2880:6976], wi[:, 2048:2816], kr, kr[:, half:], kr[:, :half]], axis=1).astype(BF16)
        wq = w_uq[l]
        wq_rope = wq[:, :, NOPE_DIM:]
        p = {
            'g_norm1': g_norm1[l], 'w_in': w_in_perm, 'g_v_ln': g_v_ln[l], 'b_v_ln': b_v_ln[l],
            'w_s': w_s[l], 'b_s': b_s[l], 'g_q_lat': g_q_lat[l], 'g_q_nope': g_q_nope[l], 'g_q_rope': g_q_rope[l],
            'g_kv_lat': g_kv_lat[l], 'g_k_rope': g_k_rope[l], 'g_k_nope': g_k_nope[l],
            'wq_n': wq[:, :, :NOPE_DIM].reshape(Q_LORA, -1).astype(BF16),
            'wq_r': jnp.concatenate([wq_rope, wq_rope[:, :, half:], wq_rope[:, :, :half]],
                                    axis=-1).reshape(Q_LORA, -1).astype(BF16),
            'w_uk': w_uk[l].reshape(KV_LORA, -1).astype(BF16), 'w_uv': w_uv[l].reshape(KV_LORA, -1).astype(BF16),
            'w_proj_a': w_proj_a[l].astype(BF16), 'w_proj_b': w_proj_b[l].astype(BF16),
            'w_out': w_out[l].astype(BF16), 'g_norm2': g_norm2[l],
            'wr_pad': jnp.pad(w_router[l], ((0, 0), (0, LANES - N_EXPERTS))).astype(BF16),
            'br_pad': jnp.pad(b_router[l], (0, LANES - N_EXPERTS), constant_values=NEG_INF).reshape(1, LANES),
        }
        n_c = bp + bs
        c_all = jnp.pad(jnp.concatenate([c_prompt, c_sample], axis=0), ((0, -n_c % 8), (0, 0)))
        mod = _ada(c_all, w_ada[l], b_ada[l])
        caches = (cache_latent[l], jnp.swapaxes(cache_k_rope[l], 1, 2), jnp.swapaxes(cache_k_rinv[l], 1, 2))
        (x1_p, h2_p, idx_p, gate_p, mod3_p), aux_p = _mixers(y_p, mod[:bp], pos_p, p, prompt=True)
        (x1_s, h2_s, idx_s, gate_s, mod3_s), aux_s = _mixers(y_s, mod[bp:n_c], pos_s, p, prompt=False,
                                                            caches=caches, page_table=page_table)
        h2 = jnp.concatenate([h2_p, h2_s], axis=0)
        row_tok, block_e, block_rows, n_used, dest = _route(jnp.concatenate([idx_p, idx_s], axis=0))
        x_sorted = _dispatch(row_tok, n_used, h2)
        out_rows = _moe_experts(block_e, block_rows, x_sorted, w_gate_up[l], b_gate_up[l], w_down[l], b_down[l])
        n_p = bp * sp * TOP_K
        y_p = _combine(dest[:n_p], x1_p, mod3_p, gate_p, out_rows, 1, COMBINE_TOKENS)
        y_s = _combine(dest[n_p:], x1_s, mod3_s, gate_s, out_rows, COMBINE_TOKENS // ss, ss)
        for o, a in zip(outs, (aux_p[0].reshape(bp, sp, KV_LORA), aux_p[1].reshape(bp, sp, ROPE_DIM),
                               aux_p[2].reshape(bp, sp, N_HEADS), aux_s[0].reshape(bs, ss, KV_LORA),
                               aux_s[1].reshape(bs, ss, ROPE_DIM), aux_s[2].reshape(bs, ss, N_HEADS),
                               aux_s[3].reshape(bs, ss, D_A))):
            o.append(a)
    return (y_p, y_s) + tuple(jnp.stack(o) for o in outs)
```

```python
import functools
import math

import jax
import jax.numpy as jnp
from jax import lax
from jax.experimental import pallas as pl
from jax.experimental.pallas import tpu as pltpu

F32 = jnp.float32
BF16 = jnp.bfloat16

D_MODEL = 2048
D_A = D_MODEL // 2
A_GROUP = 128
A_GROUPS = D_A // A_GROUP
CHUNK = 128
N_HEADS = 16
Q_LORA = D_MODEL // 4
KV_LORA = D_MODEL // 8
NOPE_DIM = 128
ROPE_DIM = 64
V_DIM = 128
QK_DIM = NOPE_DIM + ROPE_DIM
ROPE_THETA = 10000.0
SCALE = 1.0 / math.sqrt(QK_DIM)
LOG2_E = math.log2(math.e)
N_EXPERTS = 32
TOP_K = 4
D_FF = D_MODEL
SWIGLU_LIMIT = 7.0
SWIGLU_ALPHA = 1.702
EPS = 1e-6
NEG_INF = -1e30
PAGE_SIZE = 128

LANES = 128
MOE_ROWS = 512
GATHER_ROWS = 512
ROW_DMA_UNROLL = 8
COMBINE_TOKENS = 128
PAGES_PER_STEP = 64
DECODE_CHAINS = 4
PAGE_DMA_PRIORITY = (1, 0, 0)
VMEM_LIMIT = 56 * 1024 * 1024

ZO_U = 0
ZO_V = ZO_U + D_A
ZO_GA = ZO_V + D_A
ZO_GB = ZO_GA + D_MODEL
ZO_Q = ZO_GB + D_MODEL
ZO_KV = ZO_Q + Q_LORA
ZO_KR = ZO_KV + KV_LORA
Z_COLS = ZO_KR + 2 * ROPE_DIM


def _params(sem):
    return pltpu.CompilerParams(dimension_semantics=sem, vmem_limit_bytes=VMEM_LIMIT)


def _dot(a, b):
    return jnp.dot(a, b, preferred_element_type=F32)


def _dot_nt(a, b):
    return lax.dot_general(a, b, (((1,), (1,)), ((), ())), preferred_element_type=F32)


def _sigmoid(x):
    return 1.0 / (1.0 + jnp.exp(-x))


def _gelu(x):
    c = math.sqrt(2.0 / math.pi)
    return 0.5 * x * (1.0 + jnp.tanh(c * (x + 0.044715 * (x * x * x))))


def _rms(x):
    return x * lax.rsqrt(jnp.mean(x * x, axis=-1, keepdims=True) + EPS)


def _ada_kernel(c_ref, w_ref, b_ref, o_ref):
    c = c_ref[...]
    a = (c * _sigmoid(c)).astype(BF16)
    o_ref[...] = _dot(a, w_ref[...].astype(BF16)) + b_ref[...]


def _ada(c_all, w_ada, b_ada):
    rows, d = c_all.shape
    n = w_ada.shape[1]
    tn = 1024
    return pl.pallas_call(
        _ada_kernel,
        out_shape=jax.ShapeDtypeStruct((rows, n), F32),
        grid=(n // tn,),
        in_specs=[pl.BlockSpec((rows, d), lambda j: (0, 0)),
                  pl.BlockSpec((d, tn), lambda j: (0, j)),
                  pl.BlockSpec((1, tn), lambda j: (0, j))],
        out_specs=pl.BlockSpec((rows, tn), lambda j: (0, j)),
        compiler_params=_params(("arbitrary",)),
    )(c_all, w_ada, b_ada.reshape(1, n))


def _normmod_kernel(x_ref, g_ref, sc_ref, sh_ref, o_ref):
    x = x_ref[...]
    y = _rms(x) * g_ref[...]
    y = y * (1.0 + sc_ref[...]) + sh_ref[...]
    o_ref[...] = y.reshape(o_ref.shape).astype(o_ref.dtype)


def _normmod(x, g, mod3, sc_idx, sh_idx, bb, tt):
    b, t, d = x.shape
    return pl.pallas_call(
        _normmod_kernel,
        out_shape=jax.ShapeDtypeStruct((b * t, d), BF16),
        grid=(b // bb, t // tt),
        in_specs=[pl.BlockSpec((bb, tt, d), lambda i, j: (i, j, 0)),
                  pl.BlockSpec((1, 1, d), lambda i, j: (0, 0, 0)),
                  pl.BlockSpec((bb, 1, d), lambda i, j: (i, 0, sc_idx)),
                  pl.BlockSpec((bb, 1, d), lambda i, j: (i, 0, sh_idx))],
        out_specs=pl.BlockSpec((bb * tt, d), lambda i, j: (i * (t // tt) + j, 0)),
        compiler_params=_params(("arbitrary", "arbitrary")),
    )(x, g.reshape(1, 1, d), mod3, mod3)


def _mm_kernel(x_ref, w_ref, o_ref):
    o_ref[...] = _dot(x_ref[...], w_ref[...]).astype(o_ref.dtype)


def _mm(x, w, tm, tn, out_dtype=F32):
    m, k = x.shape
    n = w.shape[1]
    return pl.pallas_call(
        _mm_kernel,
        out_shape=jax.ShapeDtypeStruct((m, n), out_dtype),
        grid=(m // tm, n // tn),
        in_specs=[pl.BlockSpec((tm, k), lambda i, j: (i, 0)),
                  pl.BlockSpec((k, tn), lambda i, j: (0, j))],
        out_specs=pl.BlockSpec((tm, tn), lambda i, j: (i, j)),
        compiler_params=_params(("arbitrary", "arbitrary")),
    )(x, w)


def _gmlp_kernel(zu_ref, zv_ref, gln_ref, bln_ref, ws_ref, bs_ref, a_ref, *maybe_vn_ref, lc):
    u = _gelu(zu_ref[...])
    gv = _gelu(zv_ref[...])
    xc = gv - jnp.mean(gv, axis=-1, keepdims=True)
    vn = xc * lax.rsqrt(jnp.mean(xc * xc, axis=-1, keepdims=True) + EPS)
    vn = vn * gln_ref[...] + bln_ref[...]
    if maybe_vn_ref:
        maybe_vn_ref[0][...] = vn
    row = lax.broadcasted_iota(jnp.int32, (CHUNK, CHUNK), 0)
    col = lax.broadcasted_iota(jnp.int32, (CHUNK, CHUNK), 1)
    mask = (col <= row) & ((row // lc) == (col // lc))
    vb = vn.astype(BF16)
    for g in range(A_GROUPS):
        w = jnp.where(mask, ws_ref[g], 0.0).astype(BF16)
        sl = slice(g * A_GROUP, (g + 1) * A_GROUP)
        s = _dot(w, vb[:, sl]) + bs_ref[g]
        a_ref[:, sl] = (u[:, sl] * s).astype(a_ref.dtype)


def _gmlp(z, g_ln, b_ln, ws_t, bs_t, lc, want_vn):
    t = z.shape[0]
    out_shape = [jax.ShapeDtypeStruct((t, D_A), BF16)]
    out_specs = [pl.BlockSpec((CHUNK, D_A), lambda i: (i, 0))]
    if want_vn:
        out_shape.append(jax.ShapeDtypeStruct((t, D_A), F32))
        out_specs.append(pl.BlockSpec((CHUNK, D_A), lambda i: (i, 0)))
    return pl.pallas_call(
        functools.partial(_gmlp_kernel, lc=lc),
        out_shape=out_shape,
        grid=(t // CHUNK,),
        in_specs=[pl.BlockSpec((CHUNK, D_A), lambda i: (i, ZO_U // D_A)),
                  pl.BlockSpec((CHUNK, D_A), lambda i: (i, ZO_V // D_A)),
                  pl.BlockSpec((1, D_A), lambda i: (0, 0)),
                  pl.BlockSpec((1, D_A), lambda i: (0, 0)),
                  pl.BlockSpec((A_GROUPS, CHUNK, CHUNK), lambda i: (0, 0, 0)),
                  pl.BlockSpec((A_GROUPS, CHUNK, CHUNK), lambda i: (0, 0, 0))],
        out_specs=out_specs,
        compiler_params=_params(("arbitrary",)),
    )(z, z, g_ln.reshape(1, D_A), b_ln.reshape(1, D_A), ws_t, bs_t)


def _rope_pair(y2, gain2, tab):
    p = y2 * gain2 * tab
    return p + pltpu.roll(p, ROPE_DIM, 1)


def _mla_kernel(zq_ref, zkv_ref, zkr_ref, tab_ref, gql_ref, gkv_ref, gqn_ref, gq2_ref, gk2_ref, gkn_ref,
                wqn_ref, wqr_ref, wuk_ref, wuv_ref, ckv_ref, kpe_ref, rinv_ref, *qkv_refs, prompt):
    tab = tab_ref[...]
    q_scale = SCALE * LOG2_E
    q_lat = (_rms(zq_ref[...]) * gql_ref[...]).astype(BF16)
    c_kv = _rms(zkv_ref[...]) * gkv_ref[...]
    ckv_ref[...] = c_kv
    kr2 = zkr_ref[...]
    kr2 = kr2 * lax.rsqrt(jnp.sum(kr2 * kr2, axis=-1, keepdims=True) / (2 * ROPE_DIM) + EPS)
    kpe2 = _rope_pair(kr2, gk2_ref[...], tab)
    kpe_ref[...] = kpe2[:, :ROPE_DIM]
    cb = c_kv.astype(BF16)
    tm = cb.shape[0]
    lane = lax.broadcasted_iota(jnp.int32, (tm, N_HEADS), 1)
    rinv_all = jnp.zeros((tm, N_HEADS), F32)
    if prompt:
        col = lax.broadcasted_iota(jnp.int32, kpe2.shape, 1)
        kp_b = jnp.where(col < ROPE_DIM, kpe2, 0.0).astype(BF16)
    for h in range(N_HEADS):
        sl = slice(h * NOPE_DIM, (h + 1) * NOPE_DIM)
        qn = _rms(_dot(q_lat, wqn_ref[:, sl])) * gqn_ref[...]
        qr2 = _dot(q_lat, wqr_ref[:, sl])
        qr2 = qr2 * lax.rsqrt(jnp.sum(qr2 * qr2, axis=-1, keepdims=True) / (2 * ROPE_DIM) + EPS)
        qp = (_rope_pair(qr2, gq2_ref[...], tab) * q_scale).astype(BF16)
        k_raw = _dot(cb, wuk_ref[:, sl])
        rinv = lax.rsqrt(jnp.mean(k_raw * k_raw, axis=-1, keepdims=True) + EPS)
        rinv_all = jnp.where(lane == h, rinv, rinv_all)
        if prompt:
            q_ref, k_ref, v_ref = qkv_refs
            lo = slice(2 * h * NOPE_DIM, (2 * h + 1) * NOPE_DIM)
            hi = slice((2 * h + 1) * NOPE_DIM, (2 * h + 2) * NOPE_DIM)
            q_ref[:, lo] = (qn * q_scale).astype(BF16)
            q_ref[:, hi] = qp
            k_ref[:, lo] = (k_raw * rinv * gkn_ref[...]).astype(BF16)
            k_ref[:, hi] = kp_b
            v_ref[sl, :] = _dot_nt(wuv_ref[sl, :], cb).astype(BF16)
        else:
            qn_ref, qp_ref = qkv_refs
            qn_ref[:, sl] = (qn * gkn_ref[...]).astype(BF16)
            qp_ref[:, sl] = qp
    rinv_ref[...] = rinv_all


def _mla(z, tab, gains, weights, tm, prompt):
    t = z.shape[0]
    hd = N_HEADS * NOPE_DIM
    row = lambda w: pl.BlockSpec((tm, w), lambda i: (i, 0))
    full = lambda a: pl.BlockSpec(a.shape, lambda i: (0,) * a.ndim)
    out_shape = [jax.ShapeDtypeStruct((t, KV_LORA), F32), jax.ShapeDtypeStruct((t, ROPE_DIM), F32),
                 jax.ShapeDtypeStruct((t, N_HEADS), F32)]
    out_specs = [row(KV_LORA), row(ROPE_DIM), row(N_HEADS)]
    widths = (2 * hd, 2 * hd) if prompt else (hd, hd)
    out_shape += [jax.ShapeDtypeStruct((t, w), BF16) for w in widths]
    out_specs += [row(w) for w in widths]
    if prompt:
        out_shape.append(jax.ShapeDtypeStruct((hd, t), BF16))
        out_specs.append(pl.BlockSpec((hd, tm), lambda i: (0, i)))
    return pl.pallas_call(
        functools.partial(_mla_kernel, prompt=prompt),
        out_shape=out_shape,
        grid=(t // tm,),
        in_specs=[pl.BlockSpec((tm, Q_LORA), lambda i: (i, ZO_Q // Q_LORA)),
                  pl.BlockSpec((tm, KV_LORA), lambda i: (i, ZO_KV // KV_LORA)),
                  pl.BlockSpec((tm, LANES), lambda i: (i, ZO_KR // LANES)),
                  row(LANES)] + [full(a) for a in gains] + [full(a) for a in weights],
        out_specs=out_specs,
        compiler_params=_params(("arbitrary",)),
    )(z, z, z, tab, *gains, *weights)


def _fa_kernel(q_ref, k_ref, vt_ref, o_ref, sa_ref, sb_ref, m_ref, l_ref, acc_ref, *, tq, tk):
    i = pl.program_id(1)
    q = q_ref[...]
    m_ref[...] = jnp.full(m_ref.shape, NEG_INF, F32)
    l_ref[...] = jnp.zeros(l_ref.shape, F32)
    acc_ref[...] = jnp.zeros(acc_ref.shape, F32)

    def keys(j):
        return pl.ds(pl.multiple_of(j * tk, tk), tk)

    def qk(j):
        return _dot_nt(k_ref[keys(j), :], q)

    def update(s, j):
        m_old = m_ref[...]
        m_new = jnp.maximum(m_old, jnp.max(s, axis=0, keepdims=True))
        corr = jnp.exp2(m_old - m_new)
        p = jnp.exp2(s - m_new)
        l_ref[...] = l_ref[...] * corr + jnp.sum(p, axis=0, keepdims=True)
        m_ref[...] = m_new
        acc_ref[...] = acc_ref[...] * corr + _dot(vt_ref[:, keys(j)], p.astype(BF16))

    sa_ref[...] = qk(0)

    def pair(u, carry):
        sb_ref[...] = qk(2 * u + 1)
        update(sa_ref[...], 2 * u)
        sa_ref[...] = qk(2 * u + 2)
        update(sb_ref[...], 2 * u + 1)
        return carry

    lax.fori_loop(0, i, pair, 0)
    sb_ref[...] = qk(2 * i + 1)
    key = lax.broadcasted_iota(jnp.int32, (tk, tq), 0)
    qry = lax.broadcasted_iota(jnp.int32, (tk, tq), 1)
    update(jnp.where(key <= qry, sa_ref[...], NEG_INF), 2 * i)
    update(jnp.where(key + tk <= qry, sb_ref[...], NEG_INF), 2 * i + 1)
    o_ref[...] = (acc_ref[...] / l_ref[...]).T.astype(o_ref.dtype)


def _attend_prompt(q, k, vt, tq):
    t = q.shape[0]
    tk = tq // 2
    return pl.pallas_call(
        functools.partial(_fa_kernel, tq=tq, tk=tk),
        out_shape=jax.ShapeDtypeStruct((t, N_HEADS * V_DIM), BF16),
        grid=(N_HEADS, t // tq),
        in_specs=[pl.BlockSpec((tq, 2 * NOPE_DIM), lambda h, i: (i, h)),
                  pl.BlockSpec((t, 2 * NOPE_DIM), lambda h, i: (0, h)),
                  pl.BlockSpec((V_DIM, t), lambda h, i: (h, 0))],
        out_specs=pl.BlockSpec((tq, V_DIM), lambda h, i: (i, h)),
        scratch_shapes=[pltpu.VMEM((tk, tq), F32), pltpu.VMEM((tk, tq), F32),
                        pltpu.VMEM((1, tq), F32), pltpu.VMEM((1, tq), F32), pltpu.VMEM((V_DIM, tq), F32)],
        compiler_params=_params(("arbitrary", "arbitrary")),
    )(q, k, vt)


def _absorb_kernel(qn_ref, wuk_ref, qa_ref):
    qa_ref[...] = (_dot_nt(qn_ref[...], wuk_ref[...]) * (SCALE * LOG2_E)).astype(qa_ref.dtype)


def _absorb(qn, w_uk_b):
    tokens = qn.shape[0]
    return pl.pallas_call(
        _absorb_kernel,
        out_shape=jax.ShapeDtypeStruct((tokens, N_HEADS * KV_LORA), BF16),
        grid=(N_HEADS,),
        in_specs=[pl.BlockSpec((tokens, NOPE_DIM), lambda h: (0, h)),
                  pl.BlockSpec((KV_LORA, NOPE_DIM), lambda h: (0, h))],
        out_specs=pl.BlockSpec((tokens, KV_LORA), lambda h: (0, h)),
        compiler_params=_params(("arbitrary",)),
    )(qn, w_uk_b)


def _decode_kernel(pt_ref, qa_ref, qr_ref, nlat_ref, nrope_ref, nrinv_ref, lat_hbm, rope_hbm, rinv_hbm, o_ref,
                   lat_buf, rope_buf, rinv_buf, sems, ck_s, kr_s, ri_s, m_ref, l_ref, acc_ref, *, t_new):
    pc = PAGES_PER_STEP
    b, c = pl.program_id(0), pl.program_id(1)
    nb, nc = pl.num_programs(0), pl.num_programs(1)
    n = b * nc + c
    slot = n % 2
    rows = N_HEADS * t_new

    def page_copies(bi, ci, sl):
        copies = []
        for j in range(pc):
            page = pt_ref[bi, ci * pc + j]
            copies.append(pltpu.make_async_copy(lat_hbm.at[page], lat_buf.at[sl, j], sems.at[sl, 0]))
            copies.append(pltpu.make_async_copy(rope_hbm.at[page], rope_buf.at[sl, j], sems.at[sl, 1]))
            copies.append(pltpu.make_async_copy(rinv_hbm.at[page], rinv_buf.at[sl, j], sems.at[sl, 2]))
        return copies

    @pl.when(n == 0)
    def _():
        for q, cp in enumerate(page_copies(b, c, slot)):
            cp.start(priority=PAGE_DMA_PRIORITY[q % 3])

    @pl.when(n + 1 < nb * nc)
    def _():
        wrap = c + 1 == nc
        for q, cp in enumerate(page_copies(jnp.where(wrap, b + 1, b), jnp.where(wrap, 0, c + 1), 1 - slot)):
            cp.start(priority=PAGE_DMA_PRIORITY[q % 3])

    for cp in page_copies(b, c, slot):
        cp.wait()
    qa = qa_ref[...]
    qr = qr_ref[...][:, :ROPE_DIM]

    def scores(ck, krt, rit):
        n = ck.shape[0]
        return (_dot_nt(qa, ck).reshape(t_new, N_HEADS, n) * rit[None]).reshape(rows, n) + _dot(qr, krt)

    def fold(g, s, ck):
        m_old = m_ref[g]
        m_new = jnp.maximum(m_old, jnp.max(s, axis=-1, keepdims=True))
        corr = jnp.exp2(m_old - m_new)
        p = jnp.exp2(s - m_new)
        l_ref[g] = l_ref[g] * corr + jnp.sum(p, axis=-1, keepdims=True)
        m_ref[g] = m_new
        acc_ref[g] = acc_ref[g] * corr + _dot(p.astype(BF16), ck)

    @pl.when(c == 0)
    def _():
        m_ref[...] = jnp.full(m_ref.shape, NEG_INF, F32)
        l_ref[...] = jnp.zeros(l_ref.shape, F32)
        acc_ref[...] = jnp.zeros(acc_ref.shape, F32)
        ck = nlat_ref[...].astype(BF16)
        s = scores(ck, nrope_ref[...].astype(BF16), nrinv_ref[...])
        row = lax.broadcasted_iota(jnp.int32, s.shape, 0)
        col = lax.broadcasted_iota(jnp.int32, s.shape, 1)
        fold(0, jnp.where(col <= row // N_HEADS, s, NEG_INF), ck)

    for j in range(pc):
        ck_s[j * PAGE_SIZE:(j + 1) * PAGE_SIZE, :] = lat_buf[slot, j].astype(BF16)
        kr_s[:, j * PAGE_SIZE:(j + 1) * PAGE_SIZE] = rope_buf[slot, j].astype(BF16)
        ri_s[:, j * PAGE_SIZE:(j + 1) * PAGE_SIZE] = rinv_buf[slot, j]
    keys = pc * PAGE_SIZE // DECODE_CHAINS
    cks = [ck_s[g * keys:(g + 1) * keys, :] for g in range(DECODE_CHAINS)]
    ss = [scores(cks[g], kr_s[:, g * keys:(g + 1) * keys], ri_s[:, g * keys:(g + 1) * keys])
          for g in range(DECODE_CHAINS)]
    for g in range(DECODE_CHAINS):
        fold(g, ss[g], cks[g])

    @pl.when(c == nc - 1)
    def _():
        m = m_ref[0]
        for g in range(1, DECODE_CHAINS):
            m = jnp.maximum(m, m_ref[g])
        l = jnp.zeros(m.shape, F32)
        acc = jnp.zeros(acc_ref.shape[1:], F32)
        for g in range(DECODE_CHAINS):
            w = jnp.exp2(m_ref[g] - m)
            l = l + l_ref[g] * w
            acc = acc + acc_ref[g] * w
        o_ref[...] = (acc / l).astype(o_ref.dtype)


def _decode(page_table, qa, qr, nlat, nrope_t, nrinv_t, cache_latent, cache_rope_t, cache_rinv_t, t_new):
    b, n_pages = page_table.shape
    pc = PAGES_PER_STEP
    rows = N_HEADS * t_new
    seq = lambda r, w: pl.BlockSpec((None, r, w), lambda i, c, pt: (i, 0, 0))
    hbm = pl.BlockSpec(memory_space=pl.ANY)
    keys = pc * PAGE_SIZE
    grid_spec = pltpu.PrefetchScalarGridSpec(
        num_scalar_prefetch=1,
        grid=(b, n_pages // pc),
        in_specs=[seq(rows, KV_LORA), seq(rows, LANES),
                  seq(PAGE_SIZE, KV_LORA), seq(ROPE_DIM, PAGE_SIZE), seq(N_HEADS, PAGE_SIZE), hbm, hbm, hbm],
        out_specs=seq(rows, KV_LORA),
        scratch_shapes=[pltpu.VMEM((2, pc, PAGE_SIZE, KV_LORA), F32), pltpu.VMEM((2, pc, ROPE_DIM, PAGE_SIZE), F32),
                        pltpu.VMEM((2, pc, N_HEADS, PAGE_SIZE), F32), pltpu.SemaphoreType.DMA((2, 3)),
                        pltpu.VMEM((keys, KV_LORA), BF16), pltpu.VMEM((ROPE_DIM, keys), BF16),
                        pltpu.VMEM((N_HEADS, keys), F32),
                        pltpu.VMEM((DECODE_CHAINS, rows, 1), F32), pltpu.VMEM((DECODE_CHAINS, rows, 1), F32),
                        pltpu.VMEM((DECODE_CHAINS, rows, KV_LORA), F32)],
    )
    return pl.pallas_call(
        functools.partial(_decode_kernel, t_new=t_new),
        out_shape=jax.ShapeDtypeStruct((b, rows, KV_LORA), F32),
        grid_spec=grid_spec,
        compiler_params=_params(("arbitrary", "arbitrary")),
    )(page_table, qa, qr, nlat, nrope_t, nrinv_t, cache_latent, cache_rope_t, cache_rinv_t)


def _upv_kernel(o_ref, wuv_ref, out_ref):
    out_ref[...] = _dot(o_ref[...].astype(BF16), wuv_ref[...]).astype(out_ref.dtype)


def _upv(o_lat, w_uv_b):
    tokens = o_lat.shape[0]
    return pl.pallas_call(
        _upv_kernel,
        out_shape=jax.ShapeDtypeStruct((tokens, N_HEADS * V_DIM), BF16),
        grid=(N_HEADS,),
        in_specs=[pl.BlockSpec((tokens, KV_LORA), lambda h: (0, h)),
                  pl.BlockSpec((KV_LORA, V_DIM), lambda h: (0, h))],
        out_specs=pl.BlockSpec((tokens, V_DIM), lambda h: (0, h)),
        compiler_params=_params(("arbitrary",)),
    )(o_lat, w_uv_b)


def _merge_kernel(a_ref, b_ref, wa_ref, wb_ref, ga_ref, gb_ref, o_ref):
    pa = _dot(a_ref[...], wa_ref[...])
    pb = _dot(b_ref[...], wb_ref[...])
    o_ref[...] = (_sigmoid(ga_ref[...]) * pa + _sigmoid(gb_ref[...]) * pb).astype(o_ref.dtype)


def _merge(a_out, b_out, wpa, wpb, z, tm, tn):
    t = a_out.shape[0]
    return pl.pallas_call(
        _merge_kernel,
        out_shape=jax.ShapeDtypeStruct((t, D_MODEL), BF16),
        grid=(t // tm, D_MODEL // tn),
        in_specs=[pl.BlockSpec((tm, D_A), lambda i, j: (i, 0)),
                  pl.BlockSpec((tm, D_MODEL), lambda i, j: (i, 0)),
                  pl.BlockSpec((D_A, tn), lambda i, j: (0, j)),
                  pl.BlockSpec((D_MODEL, tn), lambda i, j: (0, j)),
                  pl.BlockSpec((tm, tn), lambda i, j: (i, ZO_GA // tn + j)),
                  pl.BlockSpec((tm, tn), lambda i, j: (i, ZO_GB // tn + j))],
        out_specs=pl.BlockSpec((tm, tn), lambda i, j: (i, j)),
        compiler_params=_params(("arbitrary", "arbitrary")),
    )(a_out, b_out, wpa, wpb, z, z)


def _outproj_kernel(m_ref, wo_ref, x_ref, g1_ref, gn_ref, sc_ref, sh_ref, wr_ref, br_ref,
                    x1_ref, h2_ref, idx_ref, gate_ref):
    shape3 = x_ref.shape
    y = _dot(m_ref[...], wo_ref[...])
    x1 = x_ref[...] + g1_ref[...] * y.reshape(shape3)
    x1_ref[...] = x1
    h2 = _rms(x1) * gn_ref[...]
    h2 = (h2 * (1.0 + sc_ref[...]) + sh_ref[...]).reshape(y.shape)
    h2_ref[...] = h2
    logits = _dot(h2.astype(BF16), wr_ref[...]) + br_ref[...]
    lane = lax.broadcasted_iota(jnp.int32, logits.shape, 1).astype(F32)
    vals, idxs = [], []
    for _ in range(TOP_K):
        mx = jnp.max(logits, axis=-1, keepdims=True)
        am = jnp.min(jnp.where(logits == mx, lane, float(LANES)), axis=-1, keepdims=True)
        vals.append(mx)
        idxs.append(am)
        logits = jnp.where(lane == am, -3.0e38, logits)
    es = [jnp.exp(v - vals[0]) for v in vals]
    den = es[0] + es[1] + es[2] + es[3]
    idx_out = jnp.zeros(lane.shape, F32)
    gate_out = jnp.zeros(lane.shape, F32)
    for k in range(TOP_K):
        idx_out = jnp.where(lane == float(k), idxs[k], idx_out)
        gate_out = jnp.where(lane == float(k), es[k] / den, gate_out)
    idx_ref[...] = idx_out.astype(jnp.int32)
    gate_ref[...] = gate_out


def _outproj(merged, w_out_b, x, mod3, g_norm2, wr_pad, br_pad, bb, tt):
    b, t, d = x.shape
    tm = bb * tt
    nt = t // tt
    tok = lambda w: pl.BlockSpec((tm, w), lambda i, j: (i * nt + j, 0))
    modspec = lambda k: pl.BlockSpec((bb, 1, d), lambda i, j: (i, 0, k))
    return pl.pallas_call(
        _outproj_kernel,
        out_shape=[jax.ShapeDtypeStruct((b, t, d), F32), jax.ShapeDtypeStruct((b * t, d), F32),
                   jax.ShapeDtypeStruct((b * t, LANES), jnp.int32), jax.ShapeDtypeStruct((b * t, LANES), F32)],
        grid=(b // bb, nt),
        in_specs=[tok(d),
                  pl.BlockSpec((d, d), lambda i, j: (0, 0)),
                  pl.BlockSpec((bb, tt, d), lambda i, j: (i, j, 0)),
                  modspec(2),
                  pl.BlockSpec((1, 1, d), lambda i, j: (0, 0, 0)),
                  modspec(4), modspec(3),
                  pl.BlockSpec((d, LANES), lambda i, j: (0, 0)),
                  pl.BlockSpec((1, LANES), lambda i, j: (0, 0))],
        out_specs=[pl.BlockSpec((bb, tt, d), lambda i, j: (i, j, 0)), tok(d), tok(LANES), tok(LANES)],
        compiler_params=_params(("arbitrary", "arbitrary")),
    )(merged, w_out_b, x, mod3, g_norm2.reshape(1, 1, d), mod3, mod3, wr_pad, br_pad)


def _row_copy(src_ref, dst_ref, src_row, dst_row, sem):
    return pltpu.make_async_copy(src_ref.at[pl.ds(src_row, 1)], dst_ref.at[pl.ds(dst_row, 1)], sem)


def _dispatch_kernel(used_ref, tok_ref, tok_next_ref, h_ref, o_ref, buf, sems):
    n = buf.shape[1]
    i = pl.program_id(0)
    slot = i % 2
    rows_used = used_ref[0] * MOE_ROWS
    live = i * n < rows_used
    live_next = ((i + 1) * n < rows_used) & (i + 1 < pl.num_programs(0))

    def issue(idx_ref, sl):
        def start(g, carry):
            for k in range(ROW_DMA_UNROLL):
                r = g * ROW_DMA_UNROLL + k
                _row_copy(h_ref, buf.at[sl], idx_ref[0, r], r, sems.at[sl]).start(priority=k % 2)
            return carry

        lax.fori_loop(0, n // ROW_DMA_UNROLL, start, 0)

    @pl.when((i == 0) & live)
    def _():
        issue(tok_ref, slot)

    @pl.when(live_next)
    def _():
        issue(tok_next_ref, 1 - slot)

    @pl.when(live)
    def _():
        pltpu.make_async_copy(h_ref.at[pl.ds(0, n)], buf.at[slot], sems.at[slot]).wait()
        o_ref[...] = buf[slot].astype(o_ref.dtype)

    @pl.when(jnp.logical_not(live))
    def _():
        o_ref[...] = jnp.zeros(o_ref.shape, o_ref.dtype)


def _dispatch(row_tok, n_used, h2):
    n_rows = row_tok.shape[0]
    d = h2.shape[1]
    steps = n_rows // GATHER_ROWS
    tok3 = row_tok.reshape(steps, 1, GATHER_ROWS)
    return pl.pallas_call(
        _dispatch_kernel,
        out_shape=jax.ShapeDtypeStruct((n_rows, d), BF16),
        grid_spec=pltpu.PrefetchScalarGridSpec(
            num_scalar_prefetch=1,
            grid=(steps,),
            in_specs=[pl.BlockSpec((None, 1, GATHER_ROWS), lambda i, nu: (i, 0, 0), memory_space=pltpu.SMEM),
                      pl.BlockSpec((None, 1, GATHER_ROWS), lambda i, nu: (jnp.minimum(i + 1, steps - 1), 0, 0),
                                   memory_space=pltpu.SMEM),
                      pl.BlockSpec(memory_space=pl.ANY)],
            out_specs=pl.BlockSpec((GATHER_ROWS, d), lambda i, nu: (i, 0)),
            scratch_shapes=[pltpu.VMEM((2, GATHER_ROWS, d), F32), pltpu.SemaphoreType.DMA((2,))],
        ),
        compiler_params=_params(("arbitrary",)),
    )(n_used, tok3, tok3, h2)


def _by_live_rows(rows, x_ref, o_ref, fn):
    half = MOE_ROWS // 2

    @pl.when(rows > half)
    def _():
        o_ref[...] = fn(x_ref[...])

    @pl.when((rows > 0) & (rows <= half))
    def _():
        o_ref[:half, :] = fn(x_ref[:half, :])
        o_ref[half:, :] = jnp.zeros((MOE_ROWS - half, o_ref.shape[1]), o_ref.dtype)

    @pl.when(rows == 0)
    def _():
        o_ref[...] = jnp.zeros(o_ref.shape, o_ref.dtype)


def _moe_up_kernel(be_ref, nxt_ref, cnt_ref, x_ref, bg_ref, bl_ref, w_hbm, o_ref, wbuf, sems, slot_ref, wg_s, wl_s, *, tf, nf):
    j, i = pl.program_id(0), pl.program_id(1)
    e = be_ref[i]

    def copies(ee, jj, sl):
        return [pltpu.make_async_copy(w_hbm.at[ee, :, pl.ds(pl.multiple_of((g * nf + jj) * tf, tf), tf)],
                                      wbuf.at[sl, g], sems.at[sl, g]) for g in range(2)]

    @pl.when((j == 0) & (i == 0))
    def _():
        slot_ref[0] = 0
        for cp in copies(e, j, 0):
            cp.start()

    @pl.when((i == 0) | (e != be_ref[jnp.maximum(i - 1, 0)]))
    def _():
        sl = slot_ref[0]
        for cp in copies(e, j, sl):
            cp.wait()
        nx = nxt_ref[i]
        more = nx >= 0

        @pl.when(more | (j + 1 < nf))
        def _():
            for cp in copies(jnp.where(more, nx, be_ref[0]), jnp.where(more, j, j + 1), 1 - sl):
                cp.start()

        wg_s[...] = wbuf[sl, 0].astype(BF16)
        wl_s[...] = wbuf[sl, 1].astype(BF16)
        slot_ref[0] = 1 - sl

    def act(x):
        glu = jnp.minimum(_dot(x, wg_s[...]) + bg_ref[...], SWIGLU_LIMIT)
        lin = jnp.clip(_dot(x, wl_s[...]) + bl_ref[...], -SWIGLU_LIMIT, SWIGLU_LIMIT)
        return (glu * _sigmoid(SWIGLU_ALPHA * glu) * (lin + 1.0)).astype(o_ref.dtype)

    _by_live_rows(cnt_ref[i], x_ref, o_ref, act)


def _moe_down_kernel(be_ref, nxt_ref, cnt_ref, a_ref, b_ref, w_hbm, o_ref, wbuf, sems, slot_ref, w_s):
    i = pl.program_id(0)
    e = be_ref[i]

    def copy(ee, sl):
        return pltpu.make_async_copy(w_hbm.at[ee], wbuf.at[sl], sems.at[sl])

    @pl.when(i == 0)
    def _():
        slot_ref[0] = 0
        copy(e, 0).start()

    @pl.when((i == 0) | (e != be_ref[jnp.maximum(i - 1, 0)]))
    def _():
        sl = slot_ref[0]
        copy(e, sl).wait()
        nx = nxt_ref[i]

        @pl.when(nx >= 0)
        def _():
            copy(nx, 1 - sl).start()

        w_s[...] = wbuf[sl].astype(BF16)
        slot_ref[0] = 1 - sl

    _by_live_rows(cnt_ref[i], a_ref, o_ref, lambda a: _dot(a, w_s[...]) + b_ref[...])


def _moe_experts(block_e, block_rows, x_sorted, w_gate_up, b_gate_up, w_down, b_down):
    n_rows, d = x_sorted.shape
    n_blocks = n_rows // MOE_ROWS
    tf = 1024
    nf = D_FF // tf
    ids = jnp.arange(N_EXPERTS, dtype=jnp.int32)
    present = (block_e[None, :] == ids[:, None]).any(axis=1)
    later = jnp.where(present[None, :] & (ids[None, :] > ids[:, None]), ids[None, :], N_EXPERTS).min(axis=1)
    nxt = jnp.where(later == N_EXPERTS, -1, later).astype(jnp.int32)[block_e]
    bgu3 = b_gate_up.reshape(N_EXPERTS, 1, 2 * D_FF)
    hbm = pl.BlockSpec(memory_space=pl.ANY)
    act = pl.pallas_call(
        functools.partial(_moe_up_kernel, tf=tf, nf=nf),
        out_shape=jax.ShapeDtypeStruct((n_rows, D_FF), BF16),
        grid_spec=pltpu.PrefetchScalarGridSpec(
            num_scalar_prefetch=3,
            grid=(nf, n_blocks),
            in_specs=[pl.BlockSpec((MOE_ROWS, d), lambda j, i, be, nx, nu: (i, 0)),
                      pl.BlockSpec((None, 1, tf), lambda j, i, be, nx, nu: (be[i], 0, j)),
                      pl.BlockSpec((None, 1, tf), lambda j, i, be, nx, nu: (be[i], 0, nf + j)),
                      hbm],
            out_specs=pl.BlockSpec((MOE_ROWS, tf), lambda j, i, be, nx, nu: (i, j)),
            scratch_shapes=[pltpu.VMEM((2, 2, d, tf), F32), pltpu.SemaphoreType.DMA((2, 2)),
                            pltpu.SMEM((1,), jnp.int32), pltpu.VMEM((d, tf), BF16), pltpu.VMEM((d, tf), BF16)],
        ),
        compiler_params=_params(("arbitrary", "arbitrary")),
    )(block_e, nxt, block_rows, x_sorted, bgu3, bgu3, w_gate_up)
    return pl.pallas_call(
        _moe_down_kernel,
        out_shape=jax.ShapeDtypeStruct((n_rows, d), F32),
        grid_spec=pltpu.PrefetchScalarGridSpec(
            num_scalar_prefetch=3,
            grid=(n_blocks,),
            in_specs=[pl.BlockSpec((MOE_ROWS, D_FF), lambda i, be, nx, nu: (i, 0)),
                      pl.BlockSpec((None, 1, d), lambda i, be, nx, nu: (be[i], 0, 0)),
                      hbm],
            out_specs=pl.BlockSpec((MOE_ROWS, d), lambda i, be, nx, nu: (i, 0)),
            scratch_shapes=[pltpu.VMEM((2, D_FF, d), F32), pltpu.SemaphoreType.DMA((2,)),
                            pltpu.SMEM((1,), jnp.int32), pltpu.VMEM((D_FF, d), BF16)],
        ),
        compiler_params=_params(("arbitrary",)),
    )(block_e, nxt, block_rows, act, b_down.reshape(N_EXPERTS, 1, d), w_down)


def _combine_kernel(dest_ref, x1_ref, g2_ref, gate_ref, rows_ref, o_ref, buf, sem):
    tm = buf.shape[1]
    nt = pl.num_programs(1)
    base = (pl.program_id(0) * nt + pl.program_id(1)) * (tm * TOP_K)

    def start(t, carry):
        for k in range(TOP_K):
            _row_copy(rows_ref, buf.at[k], dest_ref[base + t * TOP_K + k], t, sem).start(priority=k % 2)
        return carry

    lax.fori_loop(0, tm, start, 0, unroll=2)
    for k in range(TOP_K):
        pltpu.make_async_copy(rows_ref.at[pl.ds(0, tm)], buf.at[k], sem).wait()
    gates = gate_ref[...]
    ff = buf[0] * gates[:, 0:1]
    for k in range(1, TOP_K):
        ff = ff + buf[k] * gates[:, k:k + 1]
    o_ref[...] = x1_ref[...] + g2_ref[...] * ff.reshape(x1_ref.shape)


def _combine(dest, x1, mod3, gates, out_rows, bb, tt):
    b, t, d = x1.shape
    nt = t // tt
    tm = bb * tt
    return pl.pallas_call(
        _combine_kernel,
        out_shape=jax.ShapeDtypeStruct((b, t, d), F32),
        grid_spec=pltpu.PrefetchScalarGridSpec(
            num_scalar_prefetch=1,
            grid=(b // bb, nt),
            in_specs=[pl.BlockSpec((bb, tt, d), lambda i, j, ds: (i, j, 0)),
                      pl.BlockSpec((bb, 1, d), lambda i, j, ds: (i, 0, 5)),
                      pl.BlockSpec((tm, LANES), lambda i, j, ds: (i * nt + j, 0)),
                      pl.BlockSpec(memory_space=pl.ANY)],
            out_specs=pl.BlockSpec((bb, tt, d), lambda i, j, ds: (i, j, 0)),
            scratch_shapes=[pltpu.VMEM((TOP_K, tm, d), F32), pltpu.SemaphoreType.DMA(())],
        ),
        compiler_params=_params(("arbitrary", "arbitrary")),
    )(dest, x1, mod3, gates, out_rows)


def _route(top_idx):
    t = top_idx.shape[0]
    n_assign = t * TOP_K
    flat_e = top_idx[:, :TOP_K].reshape(-1)
    flat_tok = jnp.arange(n_assign, dtype=jnp.int32) // TOP_K
    onehot = (flat_e[:, None] == jnp.arange(N_EXPERTS, dtype=jnp.int32)[None, :]).astype(jnp.int32)
    running = jnp.cumsum(onehot, axis=0)
    counts = running[-1]
    padded = (counts + MOE_ROWS - 1) // MOE_ROWS * MOE_ROWS
    pad_end = jnp.cumsum(padded)
    pad_start = pad_end - padded
    dest = jnp.sum(onehot * (running - 1 + pad_start[None, :]), axis=1)
    n_blocks = -(-(n_assign + N_EXPERTS * (MOE_ROWS - 1)) // MOE_ROWS)
    n_blocks = -(-n_blocks * MOE_ROWS // GATHER_ROWS) * GATHER_ROWS // MOE_ROWS
    row_tok = jnp.zeros((n_blocks * MOE_ROWS,), jnp.int32).at[dest].set(flat_tok, unique_indices=True)
    block_start = jnp.arange(n_blocks, dtype=jnp.int32) * MOE_ROWS
    block_e = jnp.minimum(jnp.sum((pad_end[None, :] <= block_start[:, None]).astype(jnp.int32), axis=1),
                          N_EXPERTS - 1)
    n_used = (pad_end[-1:] // MOE_ROWS).astype(jnp.int32)
    block_rows = jnp.clip(counts[block_e] - (block_start - pad_start[block_e]), 0, MOE_ROWS).astype(jnp.int32)
    return row_tok, block_e, block_rows, n_used, dest


def _mixers(x, mod, pos, p, *, prompt, caches=None, page_table=None):
    b, t, d = x.shape
    tokens = b * t
    mod3 = mod.reshape(b, 1, 6 * d)
    bb, tt = (1, 512) if prompt else (32, t)
    h = _normmod(x, p['g_norm1'], mod3, 1, 0, bb, tt)
    z = _mm(h, p['w_in'], min(tokens, 1024), Z_COLS // 5)
    lc = min(t, CHUNK)
    reps = CHUNK // lc
    ws_t = jnp.tile(p['w_s'][:, :lc, :lc], (1, reps, reps))
    bs_t = jnp.broadcast_to(jnp.tile(p['b_s'][:, :lc], (1, reps))[:, :, None], (A_GROUPS, CHUNK, CHUNK))
    gm = _gmlp(z, p['g_v_ln'], p['b_v_ln'], ws_t, bs_t, lc, want_vn=not prompt)
    a_out = gm[0]

    half = ROPE_DIM // 2
    inv_freq = jnp.exp(-math.log(ROPE_THETA) * jnp.arange(half, dtype=F32) / half)
    ang = pos.astype(F32)[:, None] * inv_freq[None, :]
    cos, sin = jnp.cos(ang), jnp.sin(ang)
    tab = jnp.concatenate([cos, cos, -sin, sin], axis=-1)
    tab = jnp.broadcast_to(tab[None], (b, t, LANES)).reshape(tokens, LANES)
    swap = lambda g: jnp.concatenate([g[half:], g[:half]])
    gains = [p['g_q_lat'].reshape(1, -1), p['g_kv_lat'].reshape(1, -1), p['g_q_nope'].reshape(1, -1),
             jnp.concatenate([p['g_q_rope'], swap(p['g_q_rope'])]).reshape(1, -1),
             jnp.concatenate([p['g_k_rope'], swap(p['g_k_rope'])]).reshape(1, -1),
             p['g_k_nope'].reshape(1, -1)]
    weights = [p['wq_n'], p['wq_r'], p['w_uk'], p['w_uv'].T if prompt else p['w_uv']]
    mla = _mla(z, tab, gains, weights, 256, prompt)
    c_kv, k_pe, rinv = mla[:3]
    if prompt:
        b_out = _attend_prompt(mla[3], mla[4], mla[5], min(t, 1024))
    else:
        cache_latent, cache_rope_t, cache_rinv_t = caches
        rows = N_HEADS * t
        qa = _absorb(mla[3], p['w_uk']).reshape(b, rows, KV_LORA)
        qr = mla[4].reshape(b, rows, LANES)
        pad = PAGE_SIZE - t
        nlat = jnp.pad(c_kv.reshape(b, t, KV_LORA), ((0, 0), (0, pad), (0, 0)))
        nrope_t = jnp.pad(jnp.swapaxes(k_pe.reshape(b, t, ROPE_DIM), 1, 2), ((0, 0), (0, 0), (0, pad)))
        nrinv_t = jnp.pad(jnp.swapaxes(rinv.reshape(b, t, N_HEADS), 1, 2), ((0, 0), (0, 0), (0, pad)))
        o_lat = _decode(page_table, qa, qr, nlat, nrope_t, nrinv_t, cache_latent, cache_rope_t, cache_rinv_t, t)
        b_out = _upv(o_lat.reshape(tokens, N_HEADS * KV_LORA), p['w_uv'])
    merged = _merge(a_out, b_out, p['w_proj_a'], p['w_proj_b'], z, min(tokens, 512), 1024)
    bb2, tt2 = (1, 256) if prompt else (32, t)
    x1, h2, top_idx, gates = _outproj(merged, p['w_out'], x, mod3, p['g_norm2'], p['wr_pad'], p['br_pad'], bb2, tt2)
    vn = None if prompt else gm[1]
    return (x1, h2, top_idx, gates, mod3), (c_kv, k_pe, rinv, vn)


def kernel(x_prompt, x_sample, c_prompt, c_sample, cache_latent, cache_k_rope, cache_k_rinv, page_table,
           w_ada, b_ada, g_norm1, w_in, g_v_ln, b_v_ln, w_s, b_s, g_q_lat, w_uq, g_q_nope, g_q_rope,
           g_kv_lat, g_k_rope, w_uk, w_uv, g_k_nope, w_proj_a, w_proj_b, w_out, g_norm2,
           w_router, b_router, w_gate_up, b_gate_up, w_down, b_down):
    depth = w_ada.shape[0]
    bp, sp, d = x_prompt.shape
    bs, ss, _ = x_sample.shape
    past_len = page_table.shape[1] * PAGE_SIZE
    pos_p = jnp.arange(sp, dtype=jnp.int32)
    pos_s = past_len + jnp.arange(ss, dtype=jnp.int32)
    half = ROPE_DIM // 2
    y_p, y_s = x_prompt, x_sample
    outs = [[] for _ in range(7)]
    for l in range(depth):
        wi = w_in[l]
        kr = wi[:, 2816:2880]
        w_in_perm = jnp.concatenate(
            [wi[:, 0:2048], wi[:, 2880:6976], wi[:, 2048:2816], kr, kr[:, half:], kr[:, :half]], axis=1).astype(BF16)
        wq = w_uq[l]
        wq_rope = wq[:, :, NOPE_DIM:]
        p = {
            'g_norm1': g_norm1[l], 'w_in': w_in_perm, 'g_v_ln': g_v_ln[l], 'b_v_ln': b_v_ln[l],
            'w_s': w_s[l], 'b_s': b_s[l], 'g_q_lat': g_q_lat[l], 'g_q_nope': g_q_nope[l], 'g_q_rope': g_q_rope[l],
            'g_kv_lat': g_kv_lat[l], 'g_k_rope': g_k_rope[l], 'g_k_nope': g_k_nope[l],
            'wq_n': wq[:, :, :NOPE_DIM].reshape(Q_LORA, -1).astype(BF16),
            'wq_r': jnp.concatenate([wq_rope, wq_rope[:, :, half:], wq_rope[:, :, :half]],
                                    axis=-1).reshape(Q_LORA, -1).astype(BF16),
            'w_uk': w_uk[l].reshape(KV_LORA, -1).astype(BF16), 'w_uv': w_uv[l].reshape(KV_LORA, -1).astype(BF16),
            'w_proj_a': w_proj_a[l].astype(BF16), 'w_proj_b': w_proj_b[l].astype(BF16),
            'w_out': w_out[l].astype(BF16), 'g_norm2': g_norm2[l],
            'wr_pad': jnp.pad(w_router[l], ((0, 0), (0, LANES - N_EXPERTS))).astype(BF16),
            'br_pad': jnp.pad(b_router[l], (0, LANES - N_EXPERTS), constant_values=NEG_INF).reshape(1, LANES),
        }
        n_c = bp + bs
        c_all = jnp.pad(jnp.concatenate([c_prompt, c_sample], axis=0), ((0, -n_c % 8), (0, 0)))
        mod = _ada(c_all, w_ada[l], b_ada[l])
        caches = (cache_latent[l], jnp.swapaxes(cache_k_rope[l], 1, 2), jnp.swapaxes(cache_k_rinv[l], 1, 2))
        (x1_p, h2_p, idx_p, gate_p, mod3_p), aux_p = _mixers(y_p, mod[:bp], pos_p, p, prompt=True)
        (x1_s, h2_s, idx_s, gate_s, mod3_s), aux_s = _mixers(y_s, mod[bp:n_c], pos_s, p, prompt=False,
                                                            caches=caches, page_table=page_table)
        h2 = jnp.concatenate([h2_p, h2_s], axis=0)
        row_tok, block_e, block_rows, n_used, dest = _route(jnp.concatenate([idx_p, idx_s], axis=0))
        x_sorted = _dispatch(row_tok, n_used, h2)
        out_rows = _moe_experts(block_e, block_rows, x_sorted, w_gate_up[l], b_gate_up[l], w_down[l], b_down[l])
        n_p = bp * sp * TOP_K
        y_p = _combine(dest[:n_p], x1_p, mod3_p, gate_p, out_rows, 1, COMBINE_TOKENS)
        y_s = _combine(dest[n_p:], x1_s, mod3_s, gate_s, out_rows, COMBINE_TOKENS // ss, ss)
        for o, a in zip(outs, (aux_p[0].reshape(bp, sp, KV_LORA), aux_p[1].reshape(bp, sp, ROPE_DIM),
                               aux_p[2].reshape(bp, sp, N_HEADS), aux_s[0].reshape(bs, ss, KV_LORA),
                               aux_s[1].reshape(bs, ss, ROPE_DIM), aux_s[2].reshape(bs, ss, N_HEADS),
                               aux_s[3].reshape(bs, ss, D_A))):
            o.append(a)
    return (y_p, y_s) + tuple(jnp.stack(o) for o in outs)
```

```python
import functools
import math

import jax
import jax.numpy as jnp
from jax import lax
from jax.experimental import pallas as pl
from jax.experimental.pallas import tpu as pltpu

F32 = jnp.float32
BF16 = jnp.bfloat16

D_MODEL = 2048
D_A = D_MODEL // 2
A_GROUP = 128
A_GROUPS = D_A // A_GROUP
CHUNK = 128
N_HEADS = 16
Q_LORA = D_MODEL // 4
KV_LORA = D_MODEL // 8
NOPE_DIM = 128
ROPE_DIM = 64
V_DIM = 128
QK_DIM = NOPE_DIM + ROPE_DIM
ROPE_THETA = 10000.0
SCALE = 1.0 / math.sqrt(QK_DIM)
LOG2_E = math.log2(math.e)
N_EXPERTS = 32
TOP_K = 4
D_FF = D_MODEL
SWIGLU_LIMIT = 7.0
SWIGLU_ALPHA = 1.702
EPS = 1e-6
NEG_INF = -1e30
PAGE_SIZE = 128

LANES = 128
MOE_ROWS = 512
GATHER_ROWS = 512
ROW_DMA_UNROLL = 8
COMBINE_TOKENS = 128
PAGES_PER_STEP = 64
DECODE_CHAINS = 4
PAGE_DMA_PRIORITY = (1, 0, 0)
VMEM_LIMIT = 56 * 1024 * 1024

ZO_U = 0
ZO_V = ZO_U + D_A
ZO_GA = ZO_V + D_A
ZO_GB = ZO_GA + D_MODEL
ZO_Q = ZO_GB + D_MODEL
ZO_KV = ZO_Q + Q_LORA
ZO_KR = ZO_KV + KV_LORA
Z_COLS = ZO_KR + 2 * ROPE_DIM


def _params(sem):
    return pltpu.CompilerParams(dimension_semantics=sem, vmem_limit_bytes=VMEM_LIMIT)


def _dot(a, b):
    return jnp.dot(a, b, preferred_element_type=F32)


def _dot_nt(a, b):
    return lax.dot_general(a, b, (((1,), (1,)), ((), ())), preferred_element_type=F32)


def _sigmoid(x):
    return 1.0 / (1.0 + jnp.exp(-x))


def _gelu(x):
    c = math.sqrt(2.0 / math.pi)
    return 0.5 * x * (1.0 + jnp.tanh(c * (x + 0.044715 * (x * x * x))))


def _rms(x):
    return x * lax.rsqrt(jnp.mean(x * x, axis=-1, keepdims=True) + EPS)


def _ada_kernel(c_ref, w_ref, b_ref, o_ref):
    c = c_ref[...]
    a = (c * _sigmoid(c)).astype(BF16)
    o_ref[...] = _dot(a, w_ref[...].astype(BF16)) + b_ref[...]


def _ada(c_all, w_ada, b_ada):
    rows, d = c_all.shape
    n = w_ada.shape[1]
    tn = 1024
    return pl.pallas_call(
        _ada_kernel,
        out_shape=jax.ShapeDtypeStruct((rows, n), F32),
        grid=(n // tn,),
        in_specs=[pl.BlockSpec((rows, d), lambda j: (0, 0)),
                  pl.BlockSpec((d, tn), lambda j: (0, j)),
                  pl.BlockSpec((1, tn), lambda j: (0, j))],
        out_specs=pl.BlockSpec((rows, tn), lambda j: (0, j)),
        compiler_params=_params(("arbitrary",)),
    )(c_all, w_ada, b_ada.reshape(1, n))


def _normmod_kernel(x_ref, g_ref, sc_ref, sh_ref, o_ref):
    x = x_ref[...]
    y = _rms(x) * g_ref[...]
    y = y * (1.0 + sc_ref[...]) + sh_ref[...]
    o_ref[...] = y.reshape(o_ref.shape).astype(o_ref.dtype)


def _normmod(x, g, mod3, sc_idx, sh_idx, bb, tt):
    b, t, d = x.shape
    return pl.pallas_call(
        _normmod_kernel,
        out_shape=jax.ShapeDtypeStruct((b * t, d), BF16),
        grid=(b // bb, t // tt),
        in_specs=[pl.BlockSpec((bb, tt, d), lambda i, j: (i, j, 0)),
                  pl.BlockSpec((1, 1, d), lambda i, j: (0, 0, 0)),
                  pl.BlockSpec((bb, 1, d), lambda i, j: (i, 0, sc_idx)),
                  pl.BlockSpec((bb, 1, d), lambda i, j: (i, 0, sh_idx))],
        out_specs=pl.BlockSpec((bb * tt, d), lambda i, j: (i * (t // tt) + j, 0)),
        compiler_params=_params(("arbitrary", "arbitrary")),
    )(x, g.reshape(1, 1, d), mod3, mod3)


def _mm_kernel(x_ref, w_ref, o_ref):
    o_ref[...] = _dot(x_ref[...], w_ref[...]).astype(o_ref.dtype)


def _mm(x, w, tm, tn, out_dtype=F32):
    m, k = x.shape
    n = w.shape[1]
    return pl.pallas_call(
        _mm_kernel,
        out_shape=jax.ShapeDtypeStruct((m, n), out_dtype),
        grid=(m // tm, n // tn),
        in_specs=[pl.BlockSpec((tm, k), lambda i, j: (i, 0)),
                  pl.BlockSpec((k, tn), lambda i, j: (0, j))],
        out_specs=pl.BlockSpec((tm, tn), lambda i, j: (i, j)),
        compiler_params=_params(("arbitrary", "arbitrary")),
    )(x, w)


def _gmlp_kernel(zu_ref, zv_ref, gln_ref, bln_ref, ws_ref, bs_ref, a_ref, *maybe_vn_ref, lc):
    u = _gelu(zu_ref[...])
    gv = _gelu(zv_ref[...])
    xc = gv - jnp.mean(gv, axis=-1, keepdims=True)
    vn = xc * lax.rsqrt(jnp.mean(xc * xc, axis=-1, keepdims=True) + EPS)
    vn = vn * gln_ref[...] + bln_ref[...]
    if maybe_vn_ref:
        maybe_vn_ref[0][...] = vn
    row = lax.broadcasted_iota(jnp.int32, (CHUNK, CHUNK), 0)
    col = lax.broadcasted_iota(jnp.int32, (CHUNK, CHUNK), 1)
    mask = (col <= row) & ((row // lc) == (col // lc))
    vb = vn.astype(BF16)
    for g in range(A_GROUPS):
        w = jnp.where(mask, ws_ref[g], 0.0).astype(BF16)
        sl = slice(g * A_GROUP, (g + 1) * A_GROUP)
        s = _dot(w, vb[:, sl]) + bs_ref[g]
        a_ref[:, sl] = (u[:, sl] * s).astype(a_ref.dtype)


def _gmlp(z, g_ln, b_ln, ws_t, bs_t, lc, want_vn):
    t = z.shape[0]
    out_shape = [jax.ShapeDtypeStruct((t, D_A), BF16)]
    out_specs = [pl.BlockSpec((CHUNK, D_A), lambda i: (i, 0))]
    if want_vn:
        out_shape.append(jax.ShapeDtypeStruct((t, D_A), F32))
        out_specs.append(pl.BlockSpec((CHUNK, D_A), lambda i: (i, 0)))
    return pl.pallas_call(
        functools.partial(_gmlp_kernel, lc=lc),
        out_shape=out_shape,
        grid=(t // CHUNK,),
        in_specs=[pl.BlockSpec((CHUNK, D_A), lambda i: (i, ZO_U // D_A)),
                  pl.BlockSpec((CHUNK, D_A), lambda i: (i, ZO_V // D_A)),
                  pl.BlockSpec((1, D_A), lambda i: (0, 0)),
                  pl.BlockSpec((1, D_A), lambda i: (0, 0)),
                  pl.BlockSpec((A_GROUPS, CHUNK, CHUNK), lambda i: (0, 0, 0)),
                  pl.BlockSpec((A_GROUPS, CHUNK, CHUNK), lambda i: (0, 0, 0))],
        out_specs=out_specs,
        compiler_params=_params(("arbitrary",)),
    )(z, z, g_ln.reshape(1, D_A), b_ln.reshape(1, D_A), ws_t, bs_t)


def _rope_pair(y2, gain2, tab):
    p = y2 * gain2 * tab
    return p + pltpu.roll(p, ROPE_DIM, 1)


def _mla_kernel(zq_ref, zkv_ref, zkr_ref, tab_ref, gql_ref, gkv_ref, gqn_ref, gq2_ref, gk2_ref, gkn_ref,
                wqn_ref, wqr_ref, wuk_ref, wuv_ref, ckv_ref, kpe_ref, rinv_ref, *qkv_refs, prompt):
    tab = tab_ref[...]
    q_scale = SCALE * LOG2_E
    q_lat = (_rms(zq_ref[...]) * gql_ref[...]).astype(BF16)
    c_kv = _rms(zkv_ref[...]) * gkv_ref[...]
    ckv_ref[...] = c_kv
    kr2 = zkr_ref[...]
    kr2 = kr2 * lax.rsqrt(jnp.sum(kr2 * kr2, axis=-1, keepdims=True) / (2 * ROPE_DIM) + EPS)
    kpe2 = _rope_pair(kr2, gk2_ref[...], tab)
    kpe_ref[...] = kpe2[:, :ROPE_DIM]
    cb = c_kv.astype(BF16)
    tm = cb.shape[0]
    lane = lax.broadcasted_iota(jnp.int32, (tm, N_HEADS), 1)
    rinv_all = jnp.zeros((tm, N_HEADS), F32)
    if prompt:
        col = lax.broadcasted_iota(jnp.int32, kpe2.shape, 1)
        kp_b = jnp.where(col < ROPE_DIM, kpe2, 0.0).astype(BF16)
    for h in range(N_HEADS):
        sl = slice(h * NOPE_DIM, (h + 1) * NOPE_DIM)
        qn = _rms(_dot(q_lat, wqn_ref[:, sl])) * gqn_ref[...]
        qr2 = _dot(q_lat, wqr_ref[:, sl])
        qr2 = qr2 * lax.rsqrt(jnp.sum(qr2 * qr2, axis=-1, keepdims=True) / (2 * ROPE_DIM) + EPS)
        qp = (_rope_pair(qr2, gq2_ref[...], tab) * q_scale).astype(BF16)
        k_raw = _dot(cb, wuk_ref[:, sl])
        rinv = lax.rsqrt(jnp.mean(k_raw * k_raw, axis=-1, keepdims=True) + EPS)
        rinv_all = jnp.where(lane == h, rinv, rinv_all)
        if prompt:
            q_ref, k_ref, v_ref = qkv_refs
            lo = slice(2 * h * NOPE_DIM, (2 * h + 1) * NOPE_DIM)
            hi = slice((2 * h + 1) * NOPE_DIM, (2 * h + 2) * NOPE_DIM)
            q_ref[:, lo] = (qn * q_scale).astype(BF16)
            q_ref[:, hi] = qp
            k_ref[:, lo] = (k_raw * rinv * gkn_ref[...]).astype(BF16)
            k_ref[:, hi] = kp_b
            v_ref[sl, :] = _dot_nt(wuv_ref[sl, :], cb).astype(BF16)
        else:
            qn_ref, qp_ref = qkv_refs
            qn_ref[:, sl] = (qn * gkn_ref[...]).astype(BF16)
            qp_ref[:, sl] = qp
    rinv_ref[...] = rinv_all


def _mla(z, tab, gains, weights, tm, prompt):
    t = z.shape[0]
    hd = N_HEADS * NOPE_DIM
    row = lambda w: pl.BlockSpec((tm, w), lambda i: (i, 0))
    full = lambda a: pl.BlockSpec(a.shape, lambda i: (0,) * a.ndim)
    out_shape = [jax.ShapeDtypeStruct((t, KV_LORA), F32), jax.ShapeDtypeStruct((t, ROPE_DIM), F32),
                 jax.ShapeDtypeStruct((t, N_HEADS), F32)]
    out_specs = [row(KV_LORA), row(ROPE_DIM), row(N_HEADS)]
    widths = (2 * hd, 2 * hd) if prompt else (hd, hd)
    out_shape += [jax.ShapeDtypeStruct((t, w), BF16) for w in widths]
    out_specs += [row(w) for w in widths]
    if prompt:
        out_shape.append(jax.ShapeDtypeStruct((hd, t), BF16))
        out_specs.append(pl.BlockSpec((hd, tm), lambda i: (0, i)))
    return pl.pallas_call(
        functools.partial(_mla_kernel, prompt=prompt),
        out_shape=out_shape,
        grid=(t // tm,),
        in_specs=[pl.BlockSpec((tm, Q_LORA), lambda i: (i, ZO_Q // Q_LORA)),
                  pl.BlockSpec((tm, KV_LORA), lambda i: (i, ZO_KV // KV_LORA)),
                  pl.BlockSpec((tm, LANES), lambda i: (i, ZO_KR // LANES)),
                  row(LANES)] + [full(a) for a in gains] + [full(a) for a in weights],
        out_specs=out_specs,
        compiler_params=_params(("arbitrary",)),
    )(z, z, z, tab, *gains, *weights)


def _fa_kernel(q_ref, k_ref, vt_ref, o_ref, sa_ref, sb_ref, m_ref, l_ref, acc_ref, *, tq, tk):
    i = pl.program_id(1)
    q = q_ref[...]
    m_ref[...] = jnp.full(m_ref.shape, NEG_INF, F32)
    l_ref[...] = jnp.zeros(l_ref.shape, F32)
    acc_ref[...] = jnp.zeros(acc_ref.shape, F32)

    def keys(j):
        return pl.ds(pl.multiple_of(j * tk, tk), tk)

    def qk(j):
        return _dot_nt(k_ref[keys(j), :], q)

    def update(s, j):
        m_old = m_ref[...]
        m_new = jnp.maximum(m_old, jnp.max(s, axis=0, keepdims=True))
        corr = jnp.exp2(m_old - m_new)
        p = jnp.exp2(s - m_new)
        l_ref[...] = l_ref[...] * corr + jnp.sum(p, axis=0, keepdims=True)
        m_ref[...] = m_new
        acc_ref[...] = acc_ref[...] * corr + _dot(vt_ref[:, keys(j)], p.astype(BF16))

    sa_ref[...] = qk(0)

    def pair(u, carry):
        sb_ref[...] = qk(2 * u + 1)
        update(sa_ref[...], 2 * u)
        sa_ref[...] = qk(2 * u + 2)
        update(sb_ref[...], 2 * u + 1)
        return carry

    lax.fori_loop(0, i, pair, 0)
    sb_ref[...] = qk(2 * i + 1)
    key = lax.broadcasted_iota(jnp.int32, (tk, tq), 0)
    qry = lax.broadcasted_iota(jnp.int32, (tk, tq), 1)
    update(jnp.where(key <= qry, sa_ref[...], NEG_INF), 2 * i)
    update(jnp.where(key + tk <= qry, sb_ref[...], NEG_INF), 2 * i + 1)
    o_ref[...] = (acc_ref[...] / l_ref[...]).T.astype(o_ref.dtype)


def _attend_prompt(q, k, vt, tq):
    t = q.shape[0]
    tk = tq // 2
    return pl.pallas_call(
        functools.partial(_fa_kernel, tq=tq, tk=tk),
        out_shape=jax.ShapeDtypeStruct((t, N_HEADS * V_DIM), BF16),
        grid=(N_HEADS, t // tq),
        in_specs=[pl.BlockSpec((tq, 2 * NOPE_DIM), lambda h, i: (i, h)),
                  pl.BlockSpec((t, 2 * NOPE_DIM), lambda h, i: (0, h)),
                  pl.BlockSpec((V_DIM, t), lambda h, i: (h, 0))],
        out_specs=pl.BlockSpec((tq, V_DIM), lambda h, i: (i, h)),
        scratch_shapes=[pltpu.VMEM((tk, tq), F32), pltpu.VMEM((tk, tq), F32),
                        pltpu.VMEM((1, tq), F32), pltpu.VMEM((1, tq), F32), pltpu.VMEM((V_DIM, tq), F32)],
        compiler_params=_params(("arbitrary", "arbitrary")),
    )(q, k, vt)


def _absorb_kernel(qn_ref, wuk_ref, qa_ref):
    qa_ref[...] = (_dot_nt(qn_ref[...], wuk_ref[...]) * (SCALE * LOG2_E)).astype(qa_ref.dtype)


def _absorb(qn, w_uk_b):
    tokens = qn.shape[0]
    return pl.pallas_call(
        _absorb_kernel,
        out_shape=jax.ShapeDtypeStruct((tokens, N_HEADS * KV_LORA), BF16),
        grid=(N_HEADS,),
        in_specs=[pl.BlockSpec((tokens, NOPE_DIM), lambda h: (0, h)),
                  pl.BlockSpec((KV_LORA, NOPE_DIM), lambda h: (0, h))],
        out_specs=pl.BlockSpec((tokens, KV_LORA), lambda h: (0, h)),
        compiler_params=_params(("arbitrary",)),
    )(qn, w_uk_b)


def _decode_kernel(pt_ref, qa_ref, qr_ref, nlat_ref, nrope_ref, nrinv_ref, lat_hbm, rope_hbm, rinv_hbm, o_ref,
                   lat_buf, rope_buf, rinv_buf, sems, ck_s, kr_s, ri_s, m_ref, l_ref, acc_ref, *, t_new):
    pc = PAGES_PER_STEP
    b, c = pl.program_id(0), pl.program_id(1)
    nb, nc = pl.num_programs(0), pl.num_programs(1)
    n = b * nc + c
    slot = n % 2
    rows = N_HEADS * t_new

    def page_copies(bi, ci, sl):
        copies = []
        for j in range(pc):
            page = pt_ref[bi, ci * pc + j]
            copies.append(pltpu.make_async_copy(lat_hbm.at[page], lat_buf.at[sl, j], sems.at[sl, 0]))
            copies.append(pltpu.make_async_copy(rope_hbm.at[page], rope_buf.at[sl, j], sems.at[sl, 1]))
            copies.append(pltpu.make_async_copy(rinv_hbm.at[page], rinv_buf.at[sl, j], sems.at[sl, 2]))
        return copies

    @pl.when(n == 0)
    def _():
        for q, cp in enumerate(page_copies(b, c, slot)):
            cp.start(priority=PAGE_DMA_PRIORITY[q % 3])

    @pl.when(n + 1 < nb * nc)
    def _():
        wrap = c + 1 == nc
        for q, cp in enumerate(page_copies(jnp.where(wrap, b + 1, b), jnp.where(wrap, 0, c + 1), 1 - slot)):
            cp.start(priority=PAGE_DMA_PRIORITY[q % 3])

    for cp in page_copies(b, c, slot):
        cp.wait()
    qa = qa_ref[...]
    qr = qr_ref[...][:, :ROPE_DIM]

    def scores(ck, krt, rit):
        n = ck.shape[0]
        return (_dot_nt(qa, ck).reshape(t_new, N_HEADS, n) * rit[None]).reshape(rows, n) + _dot(qr, krt)

    def fold(g, s, ck):
        m_old = m_ref[g]
        m_new = jnp.maximum(m_old, jnp.max(s, axis=-1, keepdims=True))
        corr = jnp.exp2(m_old - m_new)
        p = jnp.exp2(s - m_new)
        l_ref[g] = l_ref[g] * corr + jnp.sum(p, axis=-1, keepdims=True)
        m_ref[g] = m_new
        acc_ref[g] = acc_ref[g] * corr + _dot(p.astype(BF16), ck)

    @pl.when(c == 0)
    def _():
        m_ref[...] = jnp.full(m_ref.shape, NEG_INF, F32)
        l_ref[...] = jnp.zeros(l_ref.shape, F32)
        acc_ref[...] = jnp.zeros(acc_ref.shape, F32)
        ck = nlat_ref[...].astype(BF16)
        s = scores(ck, nrope_ref[...].astype(BF16), nrinv_ref[...])
        row = lax.broadcasted_iota(jnp.int32, s.shape, 0)
        col = lax.broadcasted_iota(jnp.int32, s.shape, 1)
        fold(0, jnp.where(col <= row // N_HEADS, s, NEG_INF), ck)

    for j in range(pc):
        ck_s[j * PAGE_SIZE:(j + 1) * PAGE_SIZE, :] = lat_buf[slot, j].astype(BF16)
        kr_s[:, j * PAGE_SIZE:(j + 1) * PAGE_SIZE] = rope_buf[slot, j].astype(BF16)
        ri_s[:, j * PAGE_SIZE:(j + 1) * PAGE_SIZE] = rinv_buf[slot, j]
    keys = pc * PAGE_SIZE // DECODE_CHAINS
    cks = [ck_s[g * keys:(g + 1) * keys, :] for g in range(DECODE_CHAINS)]
    ss = [scores(cks[g], kr_s[:, g * keys:(g + 1) * keys], ri_s[:, g * keys:(g + 1) * keys])
          for g in range(DECODE_CHAINS)]
    for g in range(DECODE_CHAINS):
        fold(g, ss[g], cks[g])

    @pl.when(c == nc - 1)
    def _():
        m = m_ref[0]
        for g in range(1, DECODE_CHAINS):
            m = jnp.maximum(m, m_ref[g])
        l = jnp.zeros(m.shape, F32)
        acc = jnp.zeros(acc_ref.shape[1:], F32)
        for g in range(DECODE_CHAINS):
            w = jnp.exp2(m_ref[g] - m)
            l = l + l_ref[g] * w
            acc = acc + acc_ref[g] * w
        o_ref[...] = (acc / l).astype(o_ref.dtype)


def _decode(page_table, qa, qr, nlat, nrope_t, nrinv_t, cache_latent, cache_rope_t, cache_rinv_t, t_new):
    b, n_pages = page_table.shape
    pc = PAGES_PER_STEP
    rows = N_HEADS * t_new
    seq = lambda r, w: pl.BlockSpec((None, r, w), lambda i, c, pt: (i, 0, 0))
    hbm = pl.BlockSpec(memory_space=pl.ANY)
    keys = pc * PAGE_SIZE
    grid_spec = pltpu.PrefetchScalarGridSpec(
        num_scalar_prefetch=1,
        grid=(b, n_pages // pc),
        in_specs=[seq(rows, KV_LORA), seq(rows, LANES),
                  seq(PAGE_SIZE, KV_LORA), seq(ROPE_DIM, PAGE_SIZE), seq(N_HEADS, PAGE_SIZE), hbm, hbm, hbm],
        out_specs=seq(rows, KV_LORA),
        scratch_shapes=[pltpu.VMEM((2, pc, PAGE_SIZE, KV_LORA), F32), pltpu.VMEM((2, pc, ROPE_DIM, PAGE_SIZE), F32),
                        pltpu.VMEM((2, pc, N_HEADS, PAGE_SIZE), F32), pltpu.SemaphoreType.DMA((2, 3)),
                        pltpu.VMEM((keys, KV_LORA), BF16), pltpu.VMEM((ROPE_DIM, keys), BF16),
                        pltpu.VMEM((N_HEADS, keys), F32),
                        pltpu.VMEM((DECODE_CHAINS, rows, 1), F32), pltpu.VMEM((DECODE_CHAINS, rows, 1), F32),
                        pltpu.VMEM((DECODE_CHAINS, rows, KV_LORA), F32)],
    )
    return pl.pallas_call(
        functools.partial(_decode_kernel, t_new=t_new),
        out_shape=jax.ShapeDtypeStruct((b, rows, KV_LORA), F32),
        grid_spec=grid_spec,
        compiler_params=_params(("arbitrary", "arbitrary")),
    )(page_table, qa, qr, nlat, nrope_t, nrinv_t, cache_latent, cache_rope_t, cache_rinv_t)


def _upv_kernel(o_ref, wuv_ref, out_ref):
    out_ref[...] = _dot(o_ref[...].astype(BF16), wuv_ref[...]).astype(out_ref.dtype)


def _upv(o_lat, w_uv_b):
    tokens = o_lat.shape[0]
    return pl.pallas_call(
        _upv_kernel,
        out_shape=jax.ShapeDtypeStruct((tokens, N_HEADS * V_DIM), BF16),
        grid=(N_HEADS,),
        in_specs=[pl.BlockSpec((tokens, KV_LORA), lambda h: (0, h)),
                  pl.BlockSpec((KV_LORA, V_DIM), lambda h: (0, h))],
        out_specs=pl.BlockSpec((tokens, V_DIM), lambda h: (0, h)),
        compiler_params=_params(("arbitrary",)),
    )(o_lat, w_uv_b)


def _merge_kernel(a_ref, b_ref, wa_ref, wb_ref, ga_ref, gb_ref, o_ref):
    pa = _dot(a_ref[...], wa_ref[...])
    pb = _dot(b_ref[...], wb_ref[...])
    o_ref[...] = (_sigmoid(ga_ref[...]) * pa + _sigmoid(gb_ref[...]) * pb).astype(o_ref.dtype)


def _merge(a_out, b_out, wpa, wpb, z, tm, tn):
    t = a_out.shape[0]
    return pl.pallas_call(
        _merge_kernel,
        out_shape=jax.ShapeDtypeStruct((t, D_MODEL), BF16),
        grid=(t // tm, D_MODEL // tn),
        in_specs=[pl.BlockSpec((tm, D_A), lambda i, j: (i, 0)),
                  pl.BlockSpec((tm, D_MODEL), lambda i, j: (i, 0)),
                  pl.BlockSpec((D_A, tn), lambda i, j: (0, j)),
                  pl.BlockSpec((D_MODEL, tn), lambda i, j: (0, j)),
                  pl.BlockSpec((tm, tn), lambda i, j: (i, ZO_GA // tn + j)),
                  pl.BlockSpec((tm, tn), lambda i, j: (i, ZO_GB // tn + j))],
        out_specs=pl.BlockSpec((tm, tn), lambda i, j: (i, j)),
        compiler_params=_params(("arbitrary", "arbitrary")),
    )(a_out, b_out, wpa, wpb, z, z)


def _outproj_kernel(m_ref, wo_ref, x_ref, g1_ref, gn_ref, sc_ref, sh_ref, wr_ref, br_ref,
                    x1_ref, h2_ref, idx_ref, gate_ref):
    shape3 = x_ref.shape
    y = _dot(m_ref[...], wo_ref[...])
    x1 = x_ref[...] + g1_ref[...] * y.reshape(shape3)
    x1_ref[...] = x1
    h2 = _rms(x1) * gn_ref[...]
    h2 = (h2 * (1.0 + sc_ref[...]) + sh_ref[...]).reshape(y.shape)
    h2_ref[...] = h2
    logits = _dot(h2.astype(BF16), wr_ref[...]) + br_ref[...]
    lane = lax.broadcasted_iota(jnp.int32, logits.shape, 1).astype(F32)
    vals, idxs = [], []
    for _ in range(TOP_K):
        mx = jnp.max(logits, axis=-1, keepdims=True)
        am = jnp.min(jnp.where(logits == mx, lane, float(LANES)), axis=-1, keepdims=True)
        vals.append(mx)
        idxs.append(am)
        logits = jnp.where(lane == am, -3.0e38, logits)
    es = [jnp.exp(v - vals[0]) for v in vals]
    den = es[0] + es[1] + es[2] + es[3]
    idx_out = jnp.zeros(lane.shape, F32)
    gate_out = jnp.zeros(lane.shape, F32)
    for k in range(TOP_K):
        idx_out = jnp.where(lane == float(k), idxs[k], idx_out)
        gate_out = jnp.where(lane == float(k), es[k] / den, gate_out)
    idx_ref[...] = idx_out.astype(jnp.int32)
    gate_ref[...] = gate_out


def _outproj(merged, w_out_b, x, mod3, g_norm2, wr_pad, br_pad, bb, tt):
    b, t, d = x.shape
    tm = bb * tt
    nt = t // tt
    tok = lambda w: pl.BlockSpec((tm, w), lambda i, j: (i * nt + j, 0))
    modspec = lambda k: pl.BlockSpec((bb, 1, d), lambda i, j: (i, 0, k))
    return pl.pallas_call(
        _outproj_kernel,
        out_shape=[jax.ShapeDtypeStruct((b, t, d), F32), jax.ShapeDtypeStruct((b * t, d), F32),
                   jax.ShapeDtypeStruct((b * t, LANES), jnp.int32), jax.ShapeDtypeStruct((b * t, LANES), F32)],
        grid=(b // bb, nt),
        in_specs=[tok(d),
                  pl.BlockSpec((d, d), lambda i, j: (0, 0)),
                  pl.BlockSpec((bb, tt, d), lambda i, j: (i, j, 0)),
                  modspec(2),
                  pl.BlockSpec((1, 1, d), lambda i, j: (0, 0, 0)),
                  modspec(4), modspec(3),
                  pl.BlockSpec((d, LANES), lambda i, j: (0, 0)),
                  pl.BlockSpec((1, LANES), lambda i, j: (0, 0))],
        out_specs=[pl.BlockSpec((bb, tt, d), lambda i, j: (i, j, 0)), tok(d), tok(LANES), tok(LANES)],
        compiler_params=_params(("arbitrary", "arbitrary")),
    )(merged, w_out_b, x, mod3, g_norm2.reshape(1, 1, d), mod3, mod3, wr_pad, br_pad)


def _row_copy(src_ref, dst_ref, src_row, dst_row, sem):
    return pltpu.make_async_copy(src_ref.at[pl.ds(src_row, 1)], dst_ref.at[pl.ds(dst_row, 1)], sem)


def _dispatch_kernel(used_ref, tok_ref, tok_next_ref, h_ref, o_ref, buf, sems):
    n = buf.shape[1]
    i = pl.program_id(0)
    slot = i % 2
    rows_used = used_ref[0] * MOE_ROWS
    live = i * n < rows_used
    live_next = ((i + 1) * n < rows_used) & (i + 1 < pl.num_programs(0))

    def issue(idx_ref, sl):
        def start(g, carry):
            for k in range(ROW_DMA_UNROLL):
                r = g * ROW_DMA_UNROLL + k
                _row_copy(h_ref, buf.at[sl], idx_ref[0, r], r, sems.at[sl]).start(priority=k % 2)
            return carry

        lax.fori_loop(0, n // ROW_DMA_UNROLL, start, 0)

    @pl.when((i == 0) & live)
    def _():
        issue(tok_ref, slot)

    @pl.when(live_next)
    def _():
        issue(tok_next_ref, 1 - slot)

    @pl.when(live)
    def _():
        pltpu.make_async_copy(h_ref.at[pl.ds(0, n)], buf.at[slot], sems.at[slot]).wait()
        o_ref[...] = buf[slot].astype(o_ref.dtype)

    @pl.when(jnp.logical_not(live))
    def _():
        o_ref[...] = jnp.zeros(o_ref.shape, o_ref.dtype)


def _dispatch(row_tok, n_used, h2):
    n_rows = row_tok.shape[0]
    d = h2.shape[1]
    steps = n_rows // GATHER_ROWS
    tok3 = row_tok.reshape(steps, 1, GATHER_ROWS)
    return pl.pallas_call(
        _dispatch_kernel,
        out_shape=jax.ShapeDtypeStruct((n_rows, d), BF16),
        grid_spec=pltpu.PrefetchScalarGridSpec(
            num_scalar_prefetch=1,
            grid=(steps,),
            in_specs=[pl.BlockSpec((None, 1, GATHER_ROWS), lambda i, nu: (i, 0, 0), memory_space=pltpu.SMEM),
                      pl.BlockSpec((None, 1, GATHER_ROWS), lambda i, nu: (jnp.minimum(i + 1, steps - 1), 0, 0),
                                   memory_space=pltpu.SMEM),
                      pl.BlockSpec(memory_space=pl.ANY)],
            out_specs=pl.BlockSpec((GATHER_ROWS, d), lambda i, nu: (i, 0)),
            scratch_shapes=[pltpu.VMEM((2, GATHER_ROWS, d), F32), pltpu.SemaphoreType.DMA((2,))],
        ),
        compiler_params=_params(("arbitrary",)),
    )(n_used, tok3, tok3, h2)


def _by_live_rows(rows, x_ref, o_ref, fn):
    half = MOE_ROWS // 2

    @pl.when(rows > half)
    def _():
        o_ref[...] = fn(x_ref[...])

    @pl.when((rows > 0) & (rows <= half))
    def _():
        o_ref[:half, :] = fn(x_ref[:half, :])
        o_ref[half:, :] = jnp.zeros((MOE_ROWS - half, o_ref.shape[1]), o_ref.dtype)

    @pl.when(rows == 0)
    def _():
        o_ref[...] = jnp.zeros(o_ref.shape, o_ref.dtype)


def _moe_up_kernel(be_ref, nxt_ref, cnt_ref, x_ref, bg_ref, bl_ref, w_hbm, o_ref, wbuf, sems, slot_ref, wg_s, wl_s, *, tf, nf):
    j, i = pl.program_id(0), pl.program_id(1)
    e = be_ref[i]

    def copies(ee, jj, sl):
        return [pltpu.make_async_copy(w_hbm.at[ee, :, pl.ds(pl.multiple_of((g * nf + jj) * tf, tf), tf)],
                                      wbuf.at[sl, g], sems.at[sl, g]) for g in range(2)]

    @pl.when((j == 0) & (i == 0))
    def _():
        slot_ref[0] = 0
        for cp in copies(e, j, 0):
            cp.start()

    @pl.when((i == 0) | (e != be_ref[jnp.maximum(i - 1, 0)]))
    def _():
        sl = slot_ref[0]
        for cp in copies(e, j, sl):
            cp.wait()
        nx = nxt_ref[i]
        more = nx >= 0

        @pl.when(more | (j + 1 < nf))
        def _():
            for cp in copies(jnp.where(more, nx, be_ref[0]), jnp.where(more, j, j + 1), 1 - sl):
                cp.start()

        wg_s[...] = wbuf[sl, 0].astype(BF16)
        wl_s[...] = wbuf[sl, 1].astype(BF16)
        slot_ref[0] = 1 - sl

    def act(x):
        glu = jnp.minimum(_dot(x, wg_s[...]) + bg_ref[...], SWIGLU_LIMIT)
        lin = jnp.clip(_dot(x, wl_s[...]) + bl_ref[...], -SWIGLU_LIMIT, SWIGLU_LIMIT)
        return (glu * _sigmoid(SWIGLU_ALPHA * glu) * (lin + 1.0)).astype(o_ref.dtype)

    _by_live_rows(cnt_ref[i], x_ref, o_ref, act)


def _moe_down_kernel(be_ref, nxt_ref, cnt_ref, a_ref, b_ref, w_hbm, o_ref, wbuf, sems, slot_ref, w_s):
    i = pl.program_id(0)
    e = be_ref[i]

    def copy(ee, sl):
        return pltpu.make_async_copy(w_hbm.at[ee], wbuf.at[sl], sems.at[sl])

    @pl.when(i == 0)
    def _():
        slot_ref[0] = 0
        copy(e, 0).start()

    @pl.when((i == 0) | (e != be_ref[jnp.maximum(i - 1, 0)]))
    def _():
        sl = slot_ref[0]
        copy(e, sl).wait()
        nx = nxt_ref[i]

        @pl.when(nx >= 0)
        def _():
            copy(nx, 1 - sl).start()

        w_s[...] = wbuf[sl].astype(BF16)
        slot_ref[0] = 1 - sl

    _by_live_rows(cnt_ref[i], a_ref, o_ref, lambda a: _dot(a, w_s[...]) + b_ref[...])


def _moe_experts(block_e, block_rows, x_sorted, w_gate_up, b_gate_up, w_down, b_down):
    n_rows, d = x_sorted.shape
    n_blocks = n_rows // MOE_ROWS
    tf = 1024
    nf = D_FF // tf
    ids = jnp.arange(N_EXPERTS, dtype=jnp.int32)
    present = (block_e[None, :] == ids[:, None]).any(axis=1)
    later = jnp.where(present[None, :] & (ids[None, :] > ids[:, None]), ids[None, :], N_EXPERTS).min(axis=1)
    nxt = jnp.where(later == N_EXPERTS, -1, later).astype(jnp.int32)[block_e]
    bgu3 = b_gate_up.reshape(N_EXPERTS, 1, 2 * D_FF)
    hbm = pl.BlockSpec(memory_space=pl.ANY)
    act = pl.pallas_call(
        functools.partial(_moe_up_kernel, tf=tf, nf=nf),
        out_shape=jax.ShapeDtypeStruct((n_rows, D_FF), BF16),
        grid_spec=pltpu.PrefetchScalarGridSpec(
            num_scalar_prefetch=3,
            grid=(nf, n_blocks),
            in_specs=[pl.BlockSpec((MOE_ROWS, d), lambda j, i, be, nx, nu: (i, 0)),
                      pl.BlockSpec((None, 1, tf), lambda j, i, be, nx, nu: (be[i], 0, j)),
                      pl.BlockSpec((None, 1, tf), lambda j, i, be, nx, nu: (be[i], 0, nf + j)),
                      hbm],
            out_specs=pl.BlockSpec((MOE_ROWS, tf), lambda j, i, be, nx, nu: (i, j)),
            scratch_shapes=[pltpu.VMEM((2, 2, d, tf), F32), pltpu.SemaphoreType.DMA((2, 2)),
                            pltpu.SMEM((1,), jnp.int32), pltpu.VMEM((d, tf), BF16), pltpu.VMEM((d, tf), BF16)],
        ),
        compiler_params=_params(("arbitrary", "arbitrary")),
    )(block_e, nxt, block_rows, x_sorted, bgu3, bgu3, w_gate_up)
    return pl.pallas_call(
        _moe_down_kernel,
        out_shape=jax.ShapeDtypeStruct((n_rows, d), F32),
        grid_spec=pltpu.PrefetchScalarGridSpec(
            num_scalar_prefetch=3,
            grid=(n_blocks,),
            in_specs=[pl.BlockSpec((MOE_ROWS, D_FF), lambda i, be, nx, nu: (i, 0)),
                      pl.BlockSpec((None, 1, d), lambda i, be, nx, nu: (be[i], 0, 0)),
                      hbm],
            out_specs=pl.BlockSpec((MOE_ROWS, d), lambda i, be, nx, nu: (i, 0)),
            scratch_shapes=[pltpu.VMEM((2, D_FF, d), F32), pltpu.SemaphoreType.DMA((2,)),
                            pltpu.SMEM((1,), jnp.int32), pltpu.VMEM((D_FF, d), BF16)],
        ),
        compiler_params=_params(("arbitrary",)),
    )(block_e, nxt, block_rows, act, b_down.reshape(N_EXPERTS, 1, d), w_down)


def _combine_kernel(dest_ref, x1_ref, g2_ref, gate_ref, rows_ref, o_ref, buf, sems):
    tm = buf.shape[2]
    nt = pl.num_programs(1)
    n = pl.program_id(0) * nt + pl.program_id(1)
    slot = n % 2

    def issue(step, sl):
        base = step * (tm * TOP_K)

        def start(t, carry):
            for k in range(TOP_K):
                _row_copy(rows_ref, buf.at[sl, k], dest_ref[base + t * TOP_K + k], t, sems.at[sl]).start(priority=k % 2)
            return carry

        lax.fori_loop(0, tm, start, 0, unroll=2)

    @pl.when(n == 0)
    def _():
        issue(n, slot)

    @pl.when(n + 1 < pl.num_programs(0) * nt)
    def _():
        issue(n + 1, 1 - slot)

    for k in range(TOP_K):
        pltpu.make_async_copy(rows_ref.at[pl.ds(0, tm)], buf.at[slot, k], sems.at[slot]).wait()
    gates = gate_ref[...]
    ff = buf[slot, 0] * gates[:, 0:1]
    for k in range(1, TOP_K):
        ff = ff + buf[slot, k] * gates[:, k:k + 1]
    o_ref[...] = x1_ref[...] + g2_ref[...] * ff.reshape(x1_ref.shape)


def _combine(dest, x1, mod3, gates, out_rows, bb, tt):
    b, t, d = x1.shape
    nt = t // tt
    tm = bb * tt
    return pl.pallas_call(
        _combine_kernel,
        out_shape=jax.ShapeDtypeStruct((b, t, d), F32),
        grid_spec=pltpu.PrefetchScalarGridSpec(
            num_scalar_prefetch=1,
            grid=(b // bb, nt),
            in_specs=[pl.BlockSpec((bb, tt, d), lambda i, j, ds: (i, j, 0)),
                      pl.BlockSpec((bb, 1, d), lambda i, j, ds: (i, 0, 5)),
                      pl.BlockSpec((tm, LANES), lambda i, j, ds: (i * nt + j, 0)),
                      pl.BlockSpec(memory_space=pl.ANY)],
            out_specs=pl.BlockSpec((bb, tt, d), lambda i, j, ds: (i, j, 0)),
            scratch_shapes=[pltpu.VMEM((2, TOP_K, tm, d), F32), pltpu.SemaphoreType.DMA((2,))],
        ),
        compiler_params=_params(("arbitrary", "arbitrary")),
    )(dest, x1, mod3, gates, out_rows)


def _route(top_idx):
    t = top_idx.shape[0]
    n_assign = t * TOP_K
    flat_e = top_idx[:, :TOP_K].reshape(-1)
    flat_tok = jnp.arange(n_assign, dtype=jnp.int32) // TOP_K
    onehot = (flat_e[:, None] == jnp.arange(N_EXPERTS, dtype=jnp.int32)[None, :]).astype(jnp.int32)
    running = jnp.cumsum(onehot, axis=0)
    counts = running[-1]
    padded = (counts + MOE_ROWS - 1) // MOE_ROWS * MOE_ROWS
    pad_end = jnp.cumsum(padded)
    pad_start = pad_end - padded
    dest = jnp.sum(onehot * (running - 1 + pad_start[None, :]), axis=1)
    n_blocks = -(-(n_assign + N_EXPERTS * (MOE_ROWS - 1)) // MOE_ROWS)
    n_blocks = -(-n_blocks * MOE_ROWS // GATHER_ROWS) * GATHER_ROWS // MOE_ROWS
    row_tok = jnp.zeros((n_blocks * MOE_ROWS,), jnp.int32).at[dest].set(flat_tok, unique_indices=True)
    block_start = jnp.arange(n_blocks, dtype=jnp.int32) * MOE_ROWS
    block_e = jnp.minimum(jnp.sum((pad_end[None, :] <= block_start[:, None]).astype(jnp.int32), axis=1),
                          N_EXPERTS - 1)
    n_used = (pad_end[-1:] // MOE_ROWS).astype(jnp.int32)
    block_rows = jnp.clip(counts[block_e] - (block_start - pad_start[block_e]), 0, MOE_ROWS).astype(jnp.int32)
    return row_tok, block_e, block_rows, n_used, dest


def _mixers(x, mod, pos, p, *, prompt, caches=None, page_table=None):
    b, t, d = x.shape
    tokens = b * t
    mod3 = mod.reshape(b, 1, 6 * d)
    bb, tt = (1, 512) if prompt else (32, t)
    h = _normmod(x, p['g_norm1'], mod3, 1, 0, bb, tt)
    z = _mm(h, p['w_in'], min(tokens, 1024), Z_COLS // 5)
    lc = min(t, CHUNK)
    reps = CHUNK // lc
    ws_t = jnp.tile(p['w_s'][:, :lc, :lc], (1, reps, reps))
    bs_t = jnp.broadcast_to(jnp.tile(p['b_s'][:, :lc], (1, reps))[:, :, None], (A_GROUPS, CHUNK, CHUNK))
    gm = _gmlp(z, p['g_v_ln'], p['b_v_ln'], ws_t, bs_t, lc, want_vn=not prompt)
    a_out = gm[0]

    half = ROPE_DIM // 2
    inv_freq = jnp.exp(-math.log(ROPE_THETA) * jnp.arange(half, dtype=F32) / half)
    ang = pos.astype(F32)[:, None] * inv_freq[None, :]
    cos, sin = jnp.cos(ang), jnp.sin(ang)
    tab = jnp.concatenate([cos, cos, -sin, sin], axis=-1)
    tab = jnp.broadcast_to(tab[None], (b, t, LANES)).reshape(tokens, LANES)
    swap = lambda g: jnp.concatenate([g[half:], g[:half]])
    gains = [p['g_q_lat'].reshape(1, -1), p['g_kv_lat'].reshape(1, -1), p['g_q_nope'].reshape(1, -1),
             jnp.concatenate([p['g_q_rope'], swap(p['g_q_rope'])]).reshape(1, -1),
             jnp.concatenate([p['g_k_rope'], swap(p['g_k_rope'])]).reshape(1, -1),
             p['g_k_nope'].reshape(1, -1)]
    weights = [p['wq_n'], p['wq_r'], p['w_uk'], p['w_uv'].T if prompt else p['w_uv']]
    mla = _mla(z, tab, gains, weights, 256, prompt)
    c_kv, k_pe, rinv = mla[:3]
    if prompt:
        b_out = _attend_prompt(mla[3], mla[4], mla[5], min(t, 1024))
    else:
        cache_latent, cache_rope_t, cache_rinv_t = caches
        rows = N_HEADS * t
        qa = _absorb(mla[3], p['w_uk']).reshape(b, rows, KV_LORA)
        qr = mla[4].reshape(b, rows, LANES)
        pad = PAGE_SIZE - t
        nlat = jnp.pad(c_kv.reshape(b, t, KV_LORA), ((0, 0), (0, pad), (0, 0)))
        nrope_t = jnp.pad(jnp.swapaxes(k_pe.reshape(b, t, ROPE_DIM), 1, 2), ((0, 0), (0, 0), (0, pad)))
        nrinv_t = jnp.pad(jnp.swapaxes(rinv.reshape(b, t, N_HEADS), 1, 2), ((0, 0), (0, 0), (0, pad)))
        o_lat = _decode(page_table, qa, qr, nlat, nrope_t, nrinv_t, cache_latent, cache_rope_t, cache_rinv_t, t)
        b_out = _upv(o_lat.reshape(tokens, N_HEADS * KV_LORA), p['w_uv'])
    merged = _merge(a_out, b_out, p['w_proj_a'], p['w_proj_b'], z, min(tokens, 512), 1024)
    bb2, tt2 = (1, 256) if prompt else (32, t)
    x1, h2, top_idx, gates = _outproj(merged, p['w_out'], x, mod3, p['g_norm2'], p['wr_pad'], p['br_pad'], bb2, tt2)
    vn = None if prompt else gm[1]
    return (x1, h2, top_idx, gates, mod3), (c_kv, k_pe, rinv, vn)


def kernel(x_prompt, x_sample, c_prompt, c_sample, cache_latent, cache_k_rope, cache_k_rinv, page_table,
           w_ada, b_ada, g_norm1, w_in, g_v_ln, b_v_ln, w_s, b_s, g_q_lat, w_uq, g_q_nope, g_q_rope,
           g_kv_lat, g_k_rope, w_uk, w_uv, g_k_nope, w_proj_a, w_proj_b, w_out, g_norm2,
           w_router, b_router, w_gate_up, b_gate_up, w_down, b_down):
    depth = w_ada.shape[0]
    bp, sp, d = x_prompt.shape
    bs, ss, _ = x_sample.shape
    past_len = page_table.shape[1] * PAGE_SIZE
    pos_p = jnp.arange(sp, dtype=jnp.int32)
    pos_s = past_len + jnp.arange(ss, dtype=jnp.int32)
    half = ROPE_DIM // 2
    y_p, y_s = x_prompt, x_sample
    outs = [[] for _ in range(7)]
    for l in range(depth):
        wi = w_in[l]
        kr = wi[:, 2816:2880]
        w_in_perm = jnp.concatenate(
            [wi[:, 0:2048], wi[:, 2880:6976], wi[:, 2048:2816], kr, kr[:, half:], kr[:, :half]], axis=1).astype(BF16)
        wq = w_uq[l]
        wq_rope = wq[:, :, NOPE_DIM:]
        p = {
            'g_norm1': g_norm1[l], 'w_in': w_in_perm, 'g_v_ln': g_v_ln[l], 'b_v_ln': b_v_ln[l],
            'w_s': w_s[l], 'b_s': b_s[l], 'g_q_lat': g_q_lat[l], 'g_q_nope': g_q_nope[l], 'g_q_rope': g_q_rope[l],
            'g_kv_lat': g_kv_lat[l], 'g_k_rope': g_k_rope[l], 'g_k_nope': g_k_nope[l],
            'wq_n': wq[:, :, :NOPE_DIM].reshape(Q_LORA, -1).astype(BF16),
            'wq_r': jnp.concatenate([wq_rope, wq_rope[:, :, half:], wq_rope[:, :, :half]],
                                    axis=-1).reshape(Q_LORA, -1).astype(BF16),
            'w_uk': w_uk[l].reshape(KV_LORA, -1).astype(BF16), 'w_uv': w_uv[l].reshape(KV_LORA, -1).astype(BF16),
            'w_proj_a': w_proj_a[l].astype(BF16), 'w_proj_b': w_proj_b[l].astype(BF16),
            'w_out': w_out[l].astype(BF16), 'g_norm2': g_norm2[l],
            'wr_pad': jnp.pad(w_router[l], ((0, 0), (0, LANES - N_EXPERTS))).astype(BF16),
            'br_pad': jnp.pad(b_router[l], (0, LANES - N_EXPERTS), constant_values=NEG_INF).reshape(1, LANES),
        }
        n_c = bp + bs
        c_all = jnp.pad(jnp.concatenate([c_prompt, c_sample], axis=0), ((0, -n_c % 8), (0, 0)))
        mod = _ada(c_all, w_ada[l], b_ada[l])
        caches = (cache_latent[l], jnp.swapaxes(cache_k_rope[l], 1, 2), jnp.swapaxes(cache_k_rinv[l], 1, 2))
        (x1_p, h2_p, idx_p, gate_p, mod3_p), aux_p = _mixers(y_p, mod[:bp], pos_p, p, prompt=True)
        (x1_s, h2_s, idx_s, gate_s, mod3_s), aux_s = _mixers(y_s, mod[bp:n_c], pos_s, p, prompt=False,
                                                            caches=caches, page_table=page_table)
        h2 = jnp.concatenate([h2_p, h2_s], axis=0)
        row_tok, block_e, block_rows, n_used, dest = _route(jnp.concatenate([idx_p, idx_s], axis=0))
        x_sorted = _dispatch(row_tok, n_used, h2)
        out_rows = _moe_experts(block_e, block_rows, x_sorted, w_gate_up[l], b_gate_up[l], w_down[l], b_down[l])
        n_p = bp * sp * TOP_K
        y_p = _combine(dest[:n_p], x1_p, mod3_p, gate_p, out_rows, 1, COMBINE_TOKENS)
        y_s = _combine(dest[n_p:], x1_s, mod3_s, gate_s, out_rows, COMBINE_TOKENS // ss, ss)
        for o, a in zip(outs, (aux_p[0].reshape(bp, sp, KV_LORA), aux_p[1].reshape(bp, sp, ROPE_DIM),
                               aux_p[2].reshape(bp, sp, N_HEADS), aux_s[0].reshape(bs, ss, KV_LORA),
                               aux_s[1].reshape(bs, ss, ROPE_DIM), aux_s[2].reshape(bs, ss, N_HEADS),
                               aux_s[3].reshape(bs, ss, D_A))):
            o.append(a)
    return (y_p, y_s) + tuple(jnp.stack(o) for o in outs)
```
